```python
import math
import jax, jax.numpy as jnp
from jax import lax
import numpy as np

D_MODEL = 2048
BATCH = 2
SEQ = 16384
DEPTH = 2
DEC_BATCH = 2
DEC_SEQ = 8192
PAST_LEN = 128

GRID_W = 64
HEAD_DIM = 64
Q_BLOCK = 128
EPS = 1e-6
NEG_INF = -1e30
SCALE = HEAD_DIM ** -0.5
GROUP_WIDTH = D_MODEL // 4
MIX_WIDTH = 4 * GROUP_WIDTH
A_HEADS = GROUP_WIDTH // HEAD_DIM
A_KV_HEADS = A_HEADS // 4
ROPE_THETA = 10000.0
B_QK_DIM = HEAD_DIM
B_V_DIM = 2 * B_QK_DIM
B_HEADS = GROUP_WIDTH // B_V_DIM
C_HEADS = GROUP_WIDTH // HEAD_DIM
C_WIN_ROWS = 8
C_WIN_COLS = 16
D_HEADS = GROUP_WIDTH // HEAD_DIM
D_BRANCHES = ((128, 1), (512, 4), (2048, 16))
T5_BUCKETS = 32
T5_MAX_DIST = 1024
T5_HEADS = B_HEADS + D_HEADS
D_FF = 5632

IN_SPLITS = (A_HEADS * HEAD_DIM, A_KV_HEADS * HEAD_DIM, A_KV_HEADS * HEAD_DIM,
             B_HEADS * 2 * B_QK_DIM, B_HEADS * 2 * B_QK_DIM, B_HEADS * B_V_DIM,
             C_HEADS * HEAD_DIM, C_HEADS * HEAD_DIM, C_HEADS * HEAD_DIM,
             D_HEADS * HEAD_DIM, D_HEADS * HEAD_DIM, D_HEADS * HEAD_DIM)
IN_WIDTH = sum(IN_SPLITS)
IN_SPLIT_POINTS = tuple(int(v) for v in np.cumsum(IN_SPLITS)[:-1])

kernel_name = 'hymba_style_hybrid_bidirectional_encoder'


def rms_norm(x, g):
    xf = x.astype(jnp.float32)
    y = xf * lax.rsqrt(jnp.mean(xf * xf, axis=-1, keepdims=True) + EPS)
    return (y * g.astype(jnp.float32)).astype(x.dtype)


def swiglu(h, w_gate, w_up, w_down):
    return (jax.nn.silu(h @ w_gate) * (h @ w_up)) @ w_down


def to_blocks(x):
    b, t = x.shape[:2]
    return jnp.swapaxes(x.reshape((b, t // Q_BLOCK, Q_BLOCK) + x.shape[2:]), 0, 1)


def from_blocks(y):
    nb, b, qb = y.shape[:3]
    return jnp.swapaxes(y, 0, 1).reshape((b, nb * qb) + y.shape[3:])


def t5_bucket(rel):
    nb = T5_BUCKETS // 2
    max_exact = nb // 2
    side = (rel > 0).astype(jnp.int32) * nb
    n = jnp.abs(rel)
    large = max_exact + (jnp.log(jnp.maximum(n, 1).astype(jnp.float32) / max_exact)
                         / math.log(T5_MAX_DIST / max_exact) * (nb - max_exact)).astype(jnp.int32)
    large = jnp.minimum(large, nb - 1)
    return side + jnp.where(n < max_exact, n, large)


def axial_rope(x):
    t_len = x.shape[1]
    t = jnp.arange(t_len, dtype=jnp.int32)
    n_freq = HEAD_DIM // 4
    inv_freq = ROPE_THETA ** (-jnp.arange(n_freq, dtype=jnp.float32) / n_freq)
    ang = jnp.concatenate([(t // GRID_W).astype(jnp.float32)[:, None] * inv_freq[None, :],
                           (t % GRID_W).astype(jnp.float32)[:, None] * inv_freq[None, :]], axis=-1)
    cos = jnp.cos(ang)[None, :, None, :]
    sin = jnp.sin(ang)[None, :, None, :]
    xf = x.astype(jnp.float32).reshape(x.shape[:-1] + (HEAD_DIM // 2, 2))
    x1, x2 = xf[..., 0], xf[..., 1]
    out = jnp.stack([x1 * cos - x2 * sin, x1 * sin + x2 * cos], axis=-1).reshape(x.shape)
    return out.astype(x.dtype)


def gqa_axial_attention(q, k, v, g_q, g_k):
    b, t_len = q.shape[:2]
    q = axial_rope(rms_norm(q, g_q))
    k = axial_rope(rms_norm(k, g_k))
    qg = q.reshape(b, t_len, A_KV_HEADS, A_HEADS // A_KV_HEADS, HEAD_DIM)

    def block(qi):
        s = jnp.einsum('bqkgd,bskd->bkgqs', qi, k, preferred_element_type=jnp.float32) * SCALE
        p = jax.nn.softmax(s, axis=-1).astype(v.dtype)
        return jnp.einsum('bkgqs,bskd->bqkgd', p, v)

    o = from_blocks(lax.map(block, to_blocks(qg)))
    return o.reshape(b, t_len, A_HEADS * HEAD_DIM)


def differential_attention(q, k, v, lam_params, subln_g, lam_init, t5_b):
    b, t_len = q.shape[:2]
    lp = lam_params.astype(jnp.float32)
    lam = jnp.exp(jnp.sum(lp[0] * lp[1])) - jnp.exp(jnp.sum(lp[2] * lp[3])) + lam_init
    key_pos = jnp.arange(t_len, dtype=jnp.int32)
    starts = jnp.arange(t_len // Q_BLOCK, dtype=jnp.int32) * Q_BLOCK

    def block(args):
        qi, q0 = args
        rel = key_pos[None, :] - (q0 + jnp.arange(Q_BLOCK, dtype=jnp.int32))[:, None]
        bias = jnp.moveaxis(t5_b[t5_bucket(rel)].astype(jnp.float32), -1, 0)
        s = jnp.einsum('bqhmd,bshmd->bhmqs', qi, k, preferred_element_type=jnp.float32) * SCALE
        p = jax.nn.softmax(s + bias[None, :, None], axis=-1)
        a = (p[:, :, 0] - lam * p[:, :, 1]).astype(v.dtype)
        return jnp.einsum('bhqs,bshe->bqhe', a, v)

    o = from_blocks(lax.map(block, (to_blocks(q), starts)))
    o = rms_norm(o, subln_g) * (1.0 - lam_init)
    return o.reshape(b, t_len, B_HEADS * B_V_DIM)


def neighbourhood_attention(q, k, v, rpb):
    b, t_len = q.shape[:2]
    rows = t_len // GRID_W
    kr = min(C_WIN_ROWS, rows)
    n_keys = kr * C_WIN_COLS
    nb = t_len // Q_BLOCK
    t = jnp.arange(t_len, dtype=jnp.int32)
    r, c = t // GRID_W, t % GRID_W
    rs = jnp.clip(r - kr // 2, 0, rows - kr)
    cs = jnp.clip(c - C_WIN_COLS // 2, 0, GRID_W - C_WIN_COLS)
    key_r = jnp.broadcast_to(rs[:, None, None] + jnp.arange(kr, dtype=jnp.int32)[None, :, None], (t_len, kr, C_WIN_COLS))
    key_c = jnp.broadcast_to(cs[:, None, None] + jnp.arange(C_WIN_COLS, dtype=jnp.int32)[None, None, :], (t_len, kr, C_WIN_COLS))
    idx = (key_r * GRID_W + key_c).reshape(nb, Q_BLOCK, n_keys)
    rel_r = (key_r - r[:, None, None] + (C_WIN_ROWS - 1)).reshape(nb, Q_BLOCK, n_keys)
    rel_c = (key_c - c[:, None, None] + (C_WIN_COLS - 1)).reshape(nb, Q_BLOCK, n_keys)

    def block(args):
        qi, idx_b, rr, cc = args
        kg = jnp.take(k, idx_b, axis=1)
        vg = jnp.take(v, idx_b, axis=1)
        s = jnp.einsum('bqhd,bqkhd->bhqk', qi, kg, preferred_element_type=jnp.float32) * SCALE
        s = s + rpb[:, rr, cc].astype(jnp.float32)[None]
        p = jax.nn.softmax(s, axis=-1).astype(v.dtype)
        return jnp.einsum('bhqk,bqkhd->bqhd', p, vg)

    o = from_blocks(lax.map(block, (to_blocks(q), idx, rel_r, rel_c)))
    return o.reshape(b, t_len, C_HEADS * HEAD_DIM)


def dilated_mixture_attention(q, k, v, t5_d):
    b, t_len = q.shape[:2]
    starts = jnp.arange(t_len // Q_BLOCK, dtype=jnp.int32) * Q_BLOCK
    branch_offs = [jnp.arange(-(w // (2 * d)), w // (2 * d) + 1, dtype=jnp.int32) * d for w, d in D_BRANCHES]
    branch_bias = [jnp.transpose(t5_d[t5_bucket(o)]).astype(jnp.float32) for o in branch_offs]

    def block(args):
        qi, q0 = args
        tq = q0 + jnp.arange(Q_BLOCK, dtype=jnp.int32)
        outs, lses = [], []
        for offs, bias in zip(branch_offs, branch_bias):
            pos = tq[:, None] + offs[None, :]
            valid = (pos >= 0) & (pos < t_len)
            pos = jnp.clip(pos, 0, t_len - 1)
            kg = jnp.take(k, pos, axis=1)
            vg = jnp.take(v, pos, axis=1)
            s = jnp.einsum('bqhd,bqkhd->bhqk', qi, kg, preferred_element_type=jnp.float32) * SCALE
            s = jnp.where(valid[None, None], s + bias[None, :, None, :], NEG_INF)
            lse = jax.nn.logsumexp(s, axis=-1)
            p = jnp.exp(s - lse[..., None]).astype(v.dtype)
            outs.append(jnp.einsum('bhqk,bqkhd->bqhd', p, vg))
            lses.append(lse)
        wts = jax.nn.softmax(jnp.stack(lses, axis=0), axis=0).astype(v.dtype)
        return jnp.einsum('gbhq,gbqhd->bqhd', wts, jnp.stack(outs, axis=0))

    o = from_blocks(lax.map(block, (to_blocks(q), starts)))
    return o.reshape(b, t_len, D_HEADS * HEAD_DIM)


def mixing_sublayer(h, layer, w_in, a_q_norm, a_k_norm, b_lambda, b_subln, c_rpb, w_out, t5_table):
    b, t_len, _ = h.shape
    qa, ka, va, qb, kb, vb, qc, kc, vc, qd, kd, vd = jnp.split(h @ w_in, IN_SPLIT_POINTS, axis=-1)
    hd4 = lambda z, nh: z.reshape(b, t_len, nh, HEAD_DIM)
    out_a = gqa_axial_attention(hd4(qa, A_HEADS), hd4(ka, A_KV_HEADS), hd4(va, A_KV_HEADS), a_q_norm, a_k_norm)
    lam_init = 0.8 - 0.6 * math.exp(-0.3 * layer)
    out_b = differential_attention(qb.reshape(b, t_len, B_HEADS, 2, B_QK_DIM),
                                   kb.reshape(b, t_len, B_HEADS, 2, B_QK_DIM),
                                   vb.reshape(b, t_len, B_HEADS, B_V_DIM),
                                   b_lambda, b_subln, lam_init, t5_table[:, :B_HEADS])
    out_c = neighbourhood_attention(hd4(qc, C_HEADS), hd4(kc, C_HEADS), hd4(vc, C_HEADS), c_rpb)
    out_d = dilated_mixture_attention(hd4(qd, D_HEADS), hd4(kd, D_HEADS), hd4(vd, D_HEADS), t5_table[:, B_HEADS:])
    return jnp.concatenate([out_a, out_b, out_c, out_d], axis=-1) @ w_out


def encoder_trunk(x, ffn1_norm, ffn1_w_gate, ffn1_w_up, ffn1_w_down, mix_norm, w_in, a_q_norm, a_k_norm,
                  b_lambda, b_subln, c_rpb, w_out, ffn2_norm, ffn2_w_gate, ffn2_w_up, ffn2_w_down,
                  t5_table, final_norm):
    for l in range(DEPTH):
        x = x + 0.5 * swiglu(rms_norm(x, ffn1_norm[l]), ffn1_w_gate[l], ffn1_w_up[l], ffn1_w_down[l])
        x = x + mixing_sublayer(rms_norm(x, mix_norm[l]), l, w_in[l], a_q_norm[l], a_k_norm[l],
                                b_lambda[l], b_subln[l], c_rpb[l], w_out[l], t5_table)
        x = x + 0.5 * swiglu(rms_norm(x, ffn2_norm[l]), ffn2_w_gate[l], ffn2_w_up[l], ffn2_w_down[l])
    return rms_norm(x, final_norm)


def setup_inputs(seed: int = 0) -> dict:
    key = jax.random.key(seed)
    ks = jax.random.split(key, 24)
    f32 = jnp.float32

    def nrm(k, shape, scale):
        return scale * jax.random.normal(k, shape, dtype=f32)

    def gain(k, shape):
        return 1.0 + 0.02 * jax.random.normal(k, shape, dtype=f32)

    return {
        'x_prompt': nrm(ks[0], (BATCH, SEQ, D_MODEL), 1.0),
        'x_sample': nrm(ks[1], (DEC_BATCH, DEC_SEQ, D_MODEL), 1.0),
        'ffn1_norm': gain(ks[2], (DEPTH, D_MODEL)),
        'ffn1_w_gate': nrm(ks[3], (DEPTH, D_MODEL, D_FF), D_MODEL ** -0.5),
        'ffn1_w_up': nrm(ks[4], (DEPTH, D_MODEL, D_FF), D_MODEL ** -0.5),
        'ffn1_w_down': nrm(ks[5], (DEPTH, D_FF, D_MODEL), D_FF ** -0.5),
        'mix_norm': gain(ks[6], (DEPTH, D_MODEL)),
        'w_in': nrm(ks[7], (DEPTH, D_MODEL, IN_WIDTH), D_MODEL ** -0.5),
        'a_q_norm': gain(ks[8], (DEPTH, HEAD_DIM)),
        'a_k_norm': gain(ks[9], (DEPTH, HEAD_DIM)),
        'b_lambda': nrm(ks[10], (DEPTH, 4, B_QK_DIM), 0.1),
        'b_subln': gain(ks[11], (DEPTH, B_V_DIM)),
        'c_rpb': nrm(ks[12], (DEPTH, C_HEADS, 2 * C_WIN_ROWS - 1, 2 * C_WIN_COLS - 1), 0.02),
        'w_out': nrm(ks[13], (DEPTH, MIX_WIDTH, D_MODEL), MIX_WIDTH ** -0.5),
        'ffn2_norm': gain(ks[14], (DEPTH, D_MODEL)),
        'ffn2_w_gate': nrm(ks[15], (DEPTH, D_MODEL, D_FF), D_MODEL ** -0.5),
        'ffn2_w_up': nrm(ks[16], (DEPTH, D_MODEL, D_FF), D_MODEL ** -0.5),
        'ffn2_w_down': nrm(ks[17], (DEPTH, D_FF, D_MODEL), D_FF ** -0.5),
        't5_table': nrm(ks[18], (T5_BUCKETS, T5_HEADS), 0.1),
        'final_norm': gain(ks[19], (D_MODEL,)),
    }


def reference(x_prompt, x_sample, ffn1_norm, ffn1_w_gate, ffn1_w_up, ffn1_w_down, mix_norm, w_in,
              a_q_norm, a_k_norm, b_lambda, b_subln, c_rpb, w_out, ffn2_norm, ffn2_w_gate, ffn2_w_up,
              ffn2_w_down, t5_table, final_norm):
    y_prompt = encoder_trunk(x_prompt, ffn1_norm, ffn1_w_gate, ffn1_w_up, ffn1_w_down, mix_norm, w_in,
                             a_q_norm, a_k_norm, b_lambda, b_subln, c_rpb, w_out, ffn2_norm, ffn2_w_gate,
                             ffn2_w_up, ffn2_w_down, t5_table, final_norm)
    y_sample = encoder_trunk(x_sample, ffn1_norm, ffn1_w_gate, ffn1_w_up, ffn1_w_down, mix_norm, w_in,
                             a_q_norm, a_k_norm, b_lambda, b_subln, c_rpb, w_out, ffn2_norm, ffn2_w_gate,
                             ffn2_w_up, ffn2_w_down, t5_table, final_norm)
    return (y_prompt, y_sample)
```

```python
import functools
import math

import jax
import jax.numpy as jnp
import numpy as np
from jax import lax
from jax.experimental import pallas as pl
from jax.experimental.pallas import tpu as pltpu

F32 = jnp.float32
BF16 = jnp.bfloat16

HEAD_DIM = 64
GRID_W = 64
EPS = 1e-6
NEG_INF = -1e30
SCALE = HEAD_DIM ** -0.5
ROPE_THETA = 10000.0
A_HEADS, A_KV_HEADS = 8, 2
B_HEADS, B_V_DIM = 4, 128
C_HEADS, C_WIN_ROWS, C_WIN_COLS = 8, 8, 16
D_HEADS = 8
D_BRANCHES = ((128, 1), (512, 4), (2048, 16))
T5_BUCKETS, T5_MAX_DIST = 32, 1024
GROUP_WIDTH = 512

LANES = 128
MXU_DIM = 256
HEADS_PER_GROUP = MXU_DIM // HEAD_DIM
VMEM_LIMIT = 56 * 1024 * 1024
FFN_TM, FFN_TF = 1024, 256
PROJ_TM, PROJ_TN = 1024, 1536
APROJ_TM = 512
OUT_TM = 512
ATT_T = 512
C_ROWS_PER_TILE = 4
C_T = C_ROWS_PER_TILE * GRID_W


def _cparams(sem):
    return pltpu.CompilerParams(dimension_semantics=sem, vmem_limit_bytes=VMEM_LIMIT)


def _rms(x, g):
    return x * lax.rsqrt(jnp.mean(x * x, axis=-1, keepdims=True) + EPS) * g


def _ffn_kernel(x_ref, g_ref, wgu_ref, wd_ref, fg_ref, o_ref, xn_ref, *, final):
    j = pl.program_id(1)

    @pl.when(j == 0)
    def _():
        x = x_ref[...]
        xn_ref[...] = _rms(x, g_ref[...]).astype(BF16)
        o_ref[...] = x

    h = jnp.dot(xn_ref[...], wgu_ref[...], preferred_element_type=F32)
    tf = h.shape[1] // 2
    hg, hu = h[:, :tf], h[:, tf:]
    a = (hg * jax.nn.sigmoid(hg)) * hu
    o_ref[...] += jnp.dot((0.5 * a).astype(BF16), wd_ref[...], preferred_element_type=F32)

    if final:
        @pl.when(j == pl.num_programs(1) - 1)
        def _():
            o_ref[...] = _rms(o_ref[...], fg_ref[...])


def _ffn(x, g, wgu, wd, fg, *, final):
    n, d = x.shape
    dff = wd.shape[0]
    tm, tf = min(FFN_TM, n), FFN_TF
    return pl.pallas_call(
        functools.partial(_ffn_kernel, final=final),
        out_shape=jax.ShapeDtypeStruct((n, d), F32),
        grid=(n // tm, dff // tf),
        in_specs=[
            pl.BlockSpec((tm, d), lambda i, j: (i, 0)),
            pl.BlockSpec((1, d), lambda i, j: (0, 0)),
            pl.BlockSpec((d, 2 * tf), lambda i, j: (0, j)),
            pl.BlockSpec((tf, d), lambda i, j: (j, 0)),
            pl.BlockSpec((1, d), lambda i, j: (0, 0)),
        ],
        out_specs=pl.BlockSpec((tm, d), lambda i, j: (i, 0)),
        scratch_shapes=[pltpu.VMEM((tm, d), BF16)],
        compiler_params=_cparams(("parallel", "arbitrary")),
        name="ffn",
    )(x, g, wgu, wd, fg)


def _proj_kernel(x_ref, g_ref, w_ref, o_ref, xn_ref):
    @pl.when(pl.program_id(1) == 0)
    def _():
        xn_ref[...] = _rms(x_ref[...], g_ref[...]).astype(BF16)

    o_ref[...] = jnp.dot(xn_ref[...], w_ref[...], preferred_element_type=F32).astype(BF16)


def _proj(x, g, w):
    n, d = x.shape
    nc = w.shape[1]
    tm, tn = min(PROJ_TM, n), PROJ_TN
    return pl.pallas_call(
        _proj_kernel,
        out_shape=jax.ShapeDtypeStruct((n, nc), BF16),
        grid=(n // tm, nc // tn),
        in_specs=[
            pl.BlockSpec((tm, d), lambda i, j: (i, 0)),
            pl.BlockSpec((1, d), lambda i, j: (0, 0)),
            pl.BlockSpec((d, tn), lambda i, j: (0, j)),
        ],
        out_specs=pl.BlockSpec((tm, tn), lambda i, j: (i, j)),
        scratch_shapes=[pltpu.VMEM((tm, d), BF16)],
        compiler_params=_cparams(("parallel", "arbitrary")),
        name="proj_bcd",
    )(x, g, w)


def _aproj_kernel(x_ref, g_ref, w_ref, wsw_ref, e_ref, gc_ref, gs_ref, cos_ref, sin_ref,
                  q_ref, k_ref, v_ref):
    xn = _rms(x_ref[...], g_ref[...]).astype(BF16)
    y = jnp.dot(xn, w_ref[...], preferred_element_type=F32)
    ysw = jnp.dot(xn, wsw_ref[...], preferred_element_type=F32)
    nqk = ysw.shape[1]
    v_ref[...] = y[:, nqk:].astype(BF16)
    e = e_ref[...]
    cos = jnp.concatenate([cos_ref[...]] * (MXU_DIM // LANES), axis=1)
    sin = jnp.concatenate([sin_ref[...]] * (MXU_DIM // LANES), axis=1)
    half = nqk // 2
    for lo in range(0, nqk, MXU_DIM):
        yc = y[:, lo:lo + MXU_DIM]
        sq = yc * yc
        hi = sq.astype(BF16)
        rest = (sq - hi.astype(F32)).astype(BF16)
        ssum = (jnp.dot(hi, e, preferred_element_type=F32)
                + jnp.dot(rest, e, preferred_element_type=F32))
        r = lax.rsqrt(ssum * (1.0 / HEAD_DIM) + EPS)
        out = (yc * gc_ref[:, lo:lo + MXU_DIM] * cos
               + ysw[:, lo:lo + MXU_DIM] * gs_ref[:, lo:lo + MXU_DIM] * sin) * r
        dst, off = (q_ref, lo) if lo < half else (k_ref, lo - half)
        dst[:, off:off + MXU_DIM] = out.astype(BF16)


def _aproj(x, g, w, wsw, e, gc, gs, cos, sin, t_len):
    n, d = x.shape
    tm = min(APROJ_TM, t_len)
    npos = t_len // tm
    nqk = wsw.shape[1]
    const = lambda i: (0, 0)
    return pl.pallas_call(
        _aproj_kernel,
        out_shape=(jax.ShapeDtypeStruct((n, nqk // 2), BF16),
                   jax.ShapeDtypeStruct((n, nqk // 2), BF16),
                   jax.ShapeDtypeStruct((n, w.shape[1] - nqk), BF16)),
        grid=(n // tm,),
        in_specs=[
            pl.BlockSpec((tm, d), lambda i: (i, 0)),
            pl.BlockSpec((1, d), const),
            pl.BlockSpec(w.shape, const),
            pl.BlockSpec(wsw.shape, const),
            pl.BlockSpec(e.shape, const),
            pl.BlockSpec(gc.shape, const),
            pl.BlockSpec(gs.shape, const),
            pl.BlockSpec((tm, LANES), lambda i: (i % npos, 0)),
            pl.BlockSpec((tm, LANES), lambda i: (i % npos, 0)),
        ],
        out_specs=(pl.BlockSpec((tm, nqk // 2), lambda i: (i, 0)),
                   pl.BlockSpec((tm, nqk // 2), lambda i: (i, 0)),
                   pl.BlockSpec((tm, w.shape[1] - nqk), lambda i: (i, 0))),
        compiler_params=_cparams(("parallel",)),
        name="proj_a",
    )(x, g, w, wsw, e, gc, gs, cos, sin)


def _outproj_kernel(x_ref, oa_ref, ob_ref, oc_ref, od_ref, w_ref, o_ref):
    acc = x_ref[...]
    for m, ref in enumerate((oa_ref, ob_ref, oc_ref, od_ref)):
        acc = acc + jnp.dot(ref[...], w_ref[m * GROUP_WIDTH:(m + 1) * GROUP_WIDTH, :],
                            preferred_element_type=F32)
    o_ref[...] = acc


def _outproj(x, oa, ob, oc, od, w):
    n, d = x.shape
    tm = min(OUT_TM, n)
    mix = pl.BlockSpec((tm, GROUP_WIDTH), lambda i: (i, 0))
    return pl.pallas_call(
        _outproj_kernel,
        out_shape=jax.ShapeDtypeStruct((n, d), F32),
        grid=(n // tm,),
        in_specs=[pl.BlockSpec((tm, d), lambda i: (i, 0)), mix, mix, mix, mix,
                  pl.BlockSpec(w.shape, lambda i: (0, 0))],
        out_specs=pl.BlockSpec((tm, d), lambda i: (i, 0)),
        compiler_params=_cparams(("parallel",)),
        name="out_proj",
    )(x, oa, ob, oc, od, w)


def _band_offset(s):
    mag = (s + 1) // 2
    return jnp.where(s % 2 == 1, -mag, mag)


def _attn_kernel(*refs, n_units, dv, bias_heads, banded, diff, lam_init):
    it = iter(refs)
    q_ref, k_ref, vt_ref = next(it), next(it), next(it)
    bias_ref = next(it) if bias_heads else None
    lam_ref, subg_ref = (next(it), next(it)) if diff else (None, None)
    o_ref, qm_ref, m_ref, l_ref, acc_ref = next(it), next(it), next(it), next(it), next(it)

    i, s = pl.program_id(2), pl.program_id(3)
    n_tiles, n_steps = pl.num_programs(2), pl.num_programs(3)

    @pl.when(s == 0)
    def _():
        q = q_ref[...]
        lane = lax.broadcasted_iota(jnp.int32, q.shape, 1) // HEAD_DIM
        for u in range(n_units):
            qm_ref[u] = jnp.where(lane == u, q, jnp.zeros_like(q))
        m_ref[...] = jnp.full(m_ref.shape, -jnp.inf, F32)
        l_ref[...] = jnp.zeros(l_ref.shape, F32)
        acc_ref[...] = jnp.zeros(acc_ref.shape, F32)

    def step():
        k = k_ref[...]
        for u in range(n_units):
            st = lax.dot_general(k, qm_ref[u], (((1,), (1,)), ((), ())),
                                 preferred_element_type=F32)
            if bias_heads:
                st = st + bias_ref[u * bias_heads // n_units]
            m_old = m_ref[u]
            m_new = jnp.maximum(m_old, jnp.max(st, axis=0, keepdims=True))
            alpha = jnp.exp(m_old - m_new)
            p = jnp.exp(st - m_new)
            l_ref[u] = alpha * l_ref[u] + jnp.sum(p, axis=0, keepdims=True)
            vrow = (u * vt_ref.shape[0] // n_units) // dv * dv
            pv =jnp.dot(vt_ref[vrow:vrow + dv, :], p.astype(BF16), preferred_element_type=F32)
            acc_ref[u] = alpha * acc_ref[u] + pv
            m_ref[u] = m_new

    if banded:
        jt = i + _band_offset(s)
        pl.when((jt >= 0) & (jt < n_tiles))(step)
    else:
        step()

    @pl.when(s == n_steps - 1)
    def _():
        outs = [acc_ref[u] / l_ref[u] for u in range(n_units)]
        if diff:
            lp = lam_ref[...]
            lam = (jnp.exp(jnp.sum(lp[0:1] * lp[1:2], keepdims=True))
                   - jnp.exp(jnp.sum(lp[2:3] * lp[3:4], keepdims=True)) + lam_init)
            outs = [outs[2 * h] - lam * outs[2 * h + 1] for h in range(n_units // 2)]
        o = jnp.concatenate(outs, axis=0).T
        if diff:
            heads = []
            for h in range(o.shape[1] // dv):
                oh = o[:, h * dv:(h + 1) * dv]
                heads.append(_rms(oh, subg_ref[...]) * (1.0 - lam_init))
            o = jnp.concatenate(heads, axis=1)
        o_ref[...] = o.astype(BF16)


def _attention(q, k, vt, bias, *, t_len, tile, qcol, kcol, shared_v, banded_steps=None,
               bias_tile_fn=None, diff=None):
    n = q.shape[0]
    batch = n // t_len
    nt = t_len // tile
    groups = GROUP_WIDTH // MXU_DIM
    n_units = HEADS_PER_GROUP
    dv = B_V_DIM if diff else HEAD_DIM
    banded = banded_steps is not None
    n_steps = banded_steps if banded else nt
    vrows = HEAD_DIM if shared_v else MXU_DIM

    def key_tile(i, s):
        if banded:
            return jnp.clip(i + _band_offset(s), 0, nt - 1)
        return s

    def offset(i, s):
        return _band_offset(s) if banded else s - i

    in_specs = [
        pl.BlockSpec((tile, MXU_DIM), lambda b, g, i, s: (b * nt + i, qcol + g)),
        pl.BlockSpec((tile, MXU_DIM), lambda b, g, i, s: (b * nt + key_tile(i, s), kcol + g)),
        pl.BlockSpec((None, vrows, tile), lambda b, g, i, s: (b, g, key_tile(i, s))),
    ]
    args = [q, k, vt]
    bias_heads = 0
    if bias is not None:
        bias_heads = bias.shape[2]
        in_specs.append(pl.BlockSpec(
            (None, None, bias_heads, tile, tile),
            lambda b, g, i, s: (bias_tile_fn(i, offset(i, s), nt), g, 0, 0, 0)))
        args.append(bias)
    lam_init = 0.0
    if diff:
        lam_p, subg, lam_init = diff
        in_specs += [pl.BlockSpec(lam_p.shape, lambda b, g, i, s: (0, 0)),
                     pl.BlockSpec(subg.shape, lambda b, g, i, s: (0, 0))]
        args += [lam_p, subg]

    kern = functools.partial(_attn_kernel, n_units=n_units, dv=dv, bias_heads=bias_heads,
                             banded=banded, diff=bool(diff), lam_init=lam_init)
    return pl.pallas_call(
        kern,
        out_shape=jax.ShapeDtypeStruct((n, GROUP_WIDTH), BF16),
        grid=(batch, groups, nt, n_steps),
        in_specs=in_specs,
        out_specs=pl.BlockSpec((tile, MXU_DIM), lambda b, g, i, s: (b * nt + i, g)),
        scratch_shapes=[pltpu.VMEM((n_units, tile, MXU_DIM), BF16),
                        pltpu.VMEM((n_units, 1, tile), F32),
                        pltpu.VMEM((n_units, 1, tile), F32),
                        pltpu.VMEM((n_units, dv, tile), F32)],
        compiler_params=_cparams(("parallel", "parallel", "parallel", "arbitrary")),
        name="attention",
    )(*args)


def _t5_bucket(rel):
    nb = T5_BUCKETS // 2
    max_exact = nb // 2
    side = (rel > 0).astype(jnp.int32) * nb
    n = jnp.abs(rel)
    large = max_exact + (jnp.log(jnp.maximum(n, 1).astype(F32) / max_exact)
                         / math.log(T5_MAX_DIST / max_exact) * (nb - max_exact)).astype(jnp.int32)
    large = jnp.minimum(large, nb - 1)
    return side + jnp.where(n < max_exact, n, large)


def _rel_tiles(offsets, tile):
    kk = jnp.arange(tile, dtype=jnp.int32)[:, None]
    qq = jnp.arange(tile, dtype=jnp.int32)[None, :]
    return jnp.stack([o * tile + kk - qq for o in offsets])


def _group_heads(b):
    t, h = b.shape[:2]
    groups = GROUP_WIDTH // MXU_DIM
    return b.reshape(t, groups, h // groups, *b.shape[2:])


B_FAR = 3


def _bias_b(t5_b):
    rel = _rel_tiles(range(-B_FAR, B_FAR + 1), ATT_T)
    b = jnp.moveaxis(t5_b[_t5_bucket(rel)].astype(F32), -1, 1)
    return _group_heads(b)


def _bias_d(t5_d):
    reach = max(w // 2 for w, _ in D_BRANCHES) // ATT_T
    rel = _rel_tiles(range(-reach, reach + 1), ATT_T)
    mult = sum(((jnp.abs(rel) <= w // 2) & (rel % d == 0)).astype(F32) for w, d in D_BRANCHES)
    b = jnp.moveaxis(t5_d[_t5_bucket(rel)].astype(F32), -1, 1)
    b = jnp.where(mult[:, None] > 0, b + jnp.log(jnp.maximum(mult, 1.0))[:, None], NEG_INF)
    return _group_heads(b), reach


def _bias_c(rpb, rows):
    kr = min(C_WIN_ROWS, rows)
    n_row_tiles = rows // C_ROWS_PER_TILE
    tiles = []
    for it in (0, min(1, n_row_tiles - 1), n_row_tiles - 1):
        for off in (-1, 0, 1):
            qq = jnp.arange(C_T, dtype=jnp.int32)[None, :]
            kk = jnp.arange(C_T, dtype=jnp.int32)[:, None]
            r = it * C_ROWS_PER_TILE + qq // GRID_W
            c = qq % GRID_W
            key_r = (it + off) * C_ROWS_PER_TILE + kk // GRID_W
            key_c = kk % GRID_W
            rs = jnp.clip(r - kr // 2, 0, rows - kr)
            cs = jnp.clip(c - C_WIN_COLS // 2, 0, GRID_W - C_WIN_COLS)
            member = ((key_r >= rs) & (key_r < rs + kr) & (key_c >= cs) & (key_c < cs + C_WIN_COLS)
                      & (key_r >= 0) & (key_r < rows))
            rr = jnp.clip(key_r - r + (C_WIN_ROWS - 1), 0, 2 * C_WIN_ROWS - 2)
            cc = jnp.clip(key_c - c + (C_WIN_COLS - 1), 0, 2 * C_WIN_COLS - 2)
            tiles.append(jnp.where(member[None], rpb[:, rr, cc].astype(F32), NEG_INF))
    return _group_heads(jnp.stack(tiles))


def _c_tile_index(i, off, nt):
    variant = jnp.where(i == 0, 0, jnp.where(i == nt - 1, 2, 1))
    return variant * 3 + off + 1


def _rope_tables(t_len):
    t = jnp.arange(t_len, dtype=jnp.int32)
    n_freq = HEAD_DIM // 4
    inv_freq = ROPE_THETA ** (-jnp.arange(n_freq, dtype=F32) / n_freq)
    ang = jnp.concatenate([(t // GRID_W).astype(F32)[:, None] * inv_freq[None, :],
                           (t % GRID_W).astype(F32)[:, None] * inv_freq[None, :]], axis=-1)
    cos, sin = jnp.cos(ang), jnp.sin(ang)
    reps = LANES // HEAD_DIM
    return (jnp.tile(jnp.concatenate([cos, cos], axis=-1), (1, reps)),
            jnp.tile(jnp.concatenate([-sin, sin], axis=-1), (1, reps)))


_DEINTERLEAVE = np.concatenate([np.arange(0, HEAD_DIM, 2), np.arange(1, HEAD_DIM, 2)])
_SWAP_HALVES = np.concatenate([np.arange(HEAD_DIM // 2, HEAD_DIM), np.arange(0, HEAD_DIM // 2)])


def _prep_ffn(w_gate, w_up, w_down):
    d, dff = w_gate.shape
    nt = dff // FFN_TF
    wgu = jnp.stack([w_gate.reshape(d, nt, FFN_TF), w_up.reshape(d, nt, FFN_TF)], axis=2)
    return wgu.reshape(d, 2 * dff).astype(BF16), w_down.astype(BF16)


def _prep_mixer_a(w_in, g_q, g_k):
    d = w_in.shape[0]
    qw = A_HEADS * HEAD_DIM
    kw = A_KV_HEADS * HEAD_DIM
    rep = A_HEADS // A_KV_HEADS
    wq = w_in[:, :qw].reshape(d, A_HEADS, HEAD_DIM)[:, :, _DEINTERLEAVE]
    wk = w_in[:, qw:qw + kw].reshape(d, A_KV_HEADS, 1, HEAD_DIM)[..., _DEINTERLEAVE]
    wk = jnp.broadcast_to(wk, (d, A_KV_HEADS, rep, HEAD_DIM)).reshape(d, A_HEADS, HEAD_DIM)
    wqk = jnp.concatenate([wq, wk], axis=1)
    wv = w_in[:, qw + kw:qw + 2 * kw]
    w = jnp.concatenate([wqk.reshape(d, -1), wv], axis=1).astype(BF16)
    wsw = wqk[:, :, _SWAP_HALVES].reshape(d, -1).astype(BF16)
    g = jnp.concatenate([jnp.tile(g_q[_DEINTERLEAVE] * SCALE, A_HEADS),
                         jnp.tile(g_k[_DEINTERLEAVE], A_HEADS)])
    gsw = jnp.concatenate([jnp.tile(g_q[_DEINTERLEAVE][_SWAP_HALVES] * SCALE, A_HEADS),
                           jnp.tile(g_k[_DEINTERLEAVE][_SWAP_HALVES], A_HEADS)])
    return w, wsw, g[None, :].astype(F32), gsw[None, :].astype(F32)


def _prep_mixers_bcd(w_in):
    a_cols = (A_HEADS + 2 * A_KV_HEADS) * HEAD_DIM
    w = w_in[:, a_cols:]
    col = jnp.arange(w.shape[1])
    is_q = (col // GROUP_WIDTH) % 3 == 0
    return (w * jnp.where(is_q, SCALE, 1.0)[None, :]).astype(BF16)


def _head_sum_matrix():
    idx = np.arange(MXU_DIM) // HEAD_DIM
    return jnp.asarray(idx[:, None] == idx[None, :], dtype=BF16)


def _feature_major(x, batch, t_len):
    return jnp.swapaxes(x.reshape(batch, t_len, x.shape[-1]), 1, 2)


def _trunk(x3, layers, bias_b, bias_d, d_reach, final_norm):
    batch, t_len, d = x3.shape
    x = x3.reshape(batch * t_len, d)
    cos, sin = _rope_tables(t_len)
    e = _head_sum_matrix()
    depth = len(layers)
    blocks = GROUP_WIDTH // MXU_DIM
    for l, p in enumerate(layers):
        x = _ffn(x, p["ffn1_norm"], *p["ffn1"], final_norm, final=False)
        qa, ka, va = _aproj(x, p["mix_norm"], p["wa"], p["wa_sw"], e, p["ga"], p["ga_sw"], cos, sin, t_len)
        qkv = _proj(x, p["mix_norm"], p["w_bcd"])

        def values(m):
            lo = (3 * m + 2) * GROUP_WIDTH
            return _feature_major(qkv[:, lo:lo + GROUP_WIDTH], batch, t_len)

        oa = _attention(qa, ka, _feature_major(va, batch, t_len), None, t_len=t_len, tile=ATT_T,
                        qcol=0, kcol=0, shared_v=True)
        ob = _attention(qkv, qkv, values(0), bias_b, t_len=t_len, tile=ATT_T,
                        qcol=0, kcol=blocks, shared_v=False,
                        bias_tile_fn=lambda i, off, nt: jnp.clip(off, -B_FAR, B_FAR) + B_FAR,
                        diff=(p["b_lambda"], p["b_subln"], 0.8 - 0.6 * math.exp(-0.3 * l)))
        oc = _attention(qkv, qkv, values(1), _bias_c(p["c_rpb"], t_len // GRID_W), t_len=t_len, tile=C_T,
                        qcol=3 * blocks, kcol=4 * blocks, shared_v=False,
                        banded_steps=3, bias_tile_fn=_c_tile_index)
        od = _attention(qkv, qkv, values(2), bias_d, t_len=t_len, tile=ATT_T,
                        qcol=6 * blocks, kcol=7 * blocks, shared_v=False,
                        banded_steps=2 * d_reach + 1, bias_tile_fn=lambda i, off, nt: off + d_reach)
        x = _outproj(x, oa, ob, oc, od, p["w_out"])
        x = _ffn(x, p["ffn2_norm"], *p["ffn2"], final_norm, final=(l == depth - 1))
    return x.reshape(batch, t_len, d)


def kernel(x_prompt, x_sample, ffn1_norm, ffn1_w_gate, ffn1_w_up, ffn1_w_down, mix_norm, w_in, a_q_norm, a_k_norm, b_lambda, b_subln, c_rpb, w_out, ffn2_norm, ffn2_w_gate, ffn2_w_up, ffn2_w_down, t5_table, final_norm):
    depth = w_in.shape[0]
    layers = []
    for l in range(depth):
        wa, wa_sw, ga, ga_sw = _prep_mixer_a(w_in[l], a_q_norm[l], a_k_norm[l])
        layers.append(dict(
            ffn1_norm=ffn1_norm[l][None, :], ffn2_norm=ffn2_norm[l][None, :], mix_norm=mix_norm[l][None, :],
            ffn1=_prep_ffn(ffn1_w_gate[l], ffn1_w_up[l], ffn1_w_down[l]),
            ffn2=_prep_ffn(ffn2_w_gate[l], ffn2_w_up[l], ffn2_w_down[l]),
            wa=wa, wa_sw=wa_sw, ga=ga, ga_sw=ga_sw,
            w_bcd=_prep_mixers_bcd(w_in[l]),
            b_lambda=b_lambda[l].astype(F32), b_subln=b_subln[l][None, :].astype(F32),
            c_rpb=c_rpb[l], w_out=w_out[l].astype(BF16)))
    bias_b = _bias_b(t5_table[:, :B_HEADS])
    bias_d, d_reach = _bias_d(t5_table[:, B_HEADS:])
    fn = final_norm[None, :]
    return tuple(_trunk(x3, layers, bias_b, bias_d, d_reach, fn) for x3 in (x_prompt, x_sample))
```

```python
import functools
import math

import jax
import jax.numpy as jnp
import numpy as np
from jax import lax
from jax.experimental import pallas as pl
from jax.experimental.pallas import tpu as pltpu

F32 = jnp.float32
BF16 = jnp.bfloat16

HEAD_DIM = 64
GRID_W = 64
EPS = 1e-6
NEG_INF = -1e30
SCALE = HEAD_DIM ** -0.5
LOG2E = math.log2(math.e)
ROPE_THETA = 10000.0
A_HEADS, A_KV_HEADS = 8, 2
B_HEADS, B_V_DIM = 4, 128
C_HEADS, C_WIN_ROWS, C_WIN_COLS = 8, 8, 16
D_HEADS = 8
D_BRANCHES = ((128, 1), (512, 4), (2048, 16))
T5_BUCKETS, T5_MAX_DIST = 32, 1024
GROUP_WIDTH = 512

LANES = 128
BF16_SUBLANES = 16
MXU_DIM = 256
HEADS_PER_GROUP = MXU_DIM // HEAD_DIM
GROUPS = GROUP_WIDTH // MXU_DIM
VMEM_LIMIT = 56 * 1024 * 1024
FFN_TM, FFN_TF = 1024, 256
PROJ_TM, PROJ_TN = 1024, 1536
APROJ_TM = 512
OUT_TM = 512
ATT_T = 512
C_ROWS_PER_TILE = 4
C_T = C_ROWS_PER_TILE * GRID_W
ONES_ROWS = BF16_SUBLANES
B_NEAR = 2


def _cparams(sem):
    return pltpu.CompilerParams(dimension_semantics=sem, vmem_limit_bytes=VMEM_LIMIT)


def _rms(x, g):
    return x * lax.rsqrt(jnp.mean(x * x, axis=-1, keepdims=True) + EPS) * g


def _ffn_kernel(x_ref, g_ref, wgu_ref, wd_ref, fg_ref, o_ref, xn_ref, *, final):
    j = pl.program_id(1)

    @pl.when(j == 0)
    def _():
        x = x_ref[...]
        xn_ref[...] = _rms(x, g_ref[...]).astype(BF16)
        o_ref[...] = x

    h = jnp.dot(xn_ref[...], wgu_ref[...], preferred_element_type=F32)
    tf = h.shape[1] // 2
    hg, hu = h[:, :tf], h[:, tf:]
    a = (hg * jax.nn.sigmoid(hg)) * hu
    o_ref[...] += jnp.dot((0.5 * a).astype(BF16), wd_ref[...], preferred_element_type=F32)

    if final:
        @pl.when(j == pl.num_programs(1) - 1)
        def _():
            o_ref[...] = _rms(o_ref[...], fg_ref[...])


def _ffn(x, g, wgu, wd, fg, *, final):
    n, d = x.shape
    dff = wd.shape[0]
    tm, tf = min(FFN_TM, n), FFN_TF
    return pl.pallas_call(
        functools.partial(_ffn_kernel, final=final),
        out_shape=jax.ShapeDtypeStruct((n, d), F32),
        grid=(n // tm, dff // tf),
        in_specs=[
            pl.BlockSpec((tm, d), lambda i, j: (i, 0)),
            pl.BlockSpec((1, d), lambda i, j: (0, 0)),
            pl.BlockSpec((d, 2 * tf), lambda i, j: (0, j)),
            pl.BlockSpec((tf, d), lambda i, j: (j, 0)),
            pl.BlockSpec((1, d), lambda i, j: (0, 0)),
        ],
        out_specs=pl.BlockSpec((tm, d), lambda i, j: (i, 0)),
        scratch_shapes=[pltpu.VMEM((tm, d), BF16)],
        compiler_params=_cparams(("parallel", "arbitrary")),
        name="ffn",
    )(x, g, wgu, wd, fg)


def _proj_kernel(x_ref, g_ref, w_ref, o_ref, xn_ref):
    @pl.when(pl.program_id(1) == 0)
    def _():
        xn_ref[...] = _rms(x_ref[...], g_ref[...]).astype(BF16)

    o_ref[...] = jnp.dot(xn_ref[...], w_ref[...], preferred_element_type=F32).astype(BF16)


def _proj(x, g, w):
    n, d = x.shape
    nc = w.shape[1]
    tm, tn = min(PROJ_TM, n), PROJ_TN
    return pl.pallas_call(
        _proj_kernel,
        out_shape=jax.ShapeDtypeStruct((n, nc), BF16),
        grid=(n // tm, nc // tn),
        in_specs=[
            pl.BlockSpec((tm, d), lambda i, j: (i, 0)),
            pl.BlockSpec((1, d), lambda i, j: (0, 0)),
            pl.BlockSpec((d, tn), lambda i, j: (0, j)),
        ],
        out_specs=pl.BlockSpec((tm, tn), lambda i, j: (i, j)),
        scratch_shapes=[pltpu.VMEM((tm, d), BF16)],
        compiler_params=_cparams(("parallel", "arbitrary")),
        name="proj_bcd",
    )(x, g, w)


def _aproj_kernel(x_ref, g_ref, w_ref, wsw_ref, e_ref, gc_ref, gs_ref, cos_ref, sin_ref,
                  q_ref, k_ref, v_ref):
    xn = _rms(x_ref[...], g_ref[...]).astype(BF16)
    y = jnp.dot(xn, w_ref[...], preferred_element_type=F32)
    ysw = jnp.dot(xn, wsw_ref[...], preferred_element_type=F32)
    nqk = ysw.shape[1]
    v_ref[...] = y[:, nqk:].astype(BF16)
    e = e_ref[...]
    cos = jnp.concatenate([cos_ref[...]] * (MXU_DIM // LANES), axis=1)
    sin = jnp.concatenate([sin_ref[...]] * (MXU_DIM // LANES), axis=1)
    half = nqk // 2
    for lo in range(0, nqk, MXU_DIM):
        yc = y[:, lo:lo + MXU_DIM]
        sq = yc * yc
        hi = sq.astype(BF16)
        rest = (sq - hi.astype(F32)).astype(BF16)
        ssum = (jnp.dot(hi, e, preferred_element_type=F32)
                + jnp.dot(rest, e, preferred_element_type=F32))
        r = lax.rsqrt(ssum * (1.0 / HEAD_DIM) + EPS)
        out = (yc * gc_ref[:, lo:lo + MXU_DIM] * cos
               + ysw[:, lo:lo + MXU_DIM] * gs_ref[:, lo:lo + MXU_DIM] * sin) * r
        dst, off = (q_ref, lo) if lo < half else (k_ref, lo - half)
        dst[:, off:off + MXU_DIM] = out.astype(BF16)


def _aproj(x, g, w, wsw, e, gc, gs, cos, sin, t_len):
    n, d = x.shape
    tm = min(APROJ_TM, t_len)
    npos = t_len // tm
    nqk = wsw.shape[1]
    const = lambda i: (0, 0)
    return pl.pallas_call(
        _aproj_kernel,
        out_shape=(jax.ShapeDtypeStruct((n, nqk // 2), BF16),
                   jax.ShapeDtypeStruct((n, nqk // 2), BF16),
                   jax.ShapeDtypeStruct((n, w.shape[1] - nqk), BF16)),
        grid=(n // tm,),
        in_specs=[
            pl.BlockSpec((tm, d), lambda i: (i, 0)),
            pl.BlockSpec((1, d), const),
            pl.BlockSpec(w.shape, const),
            pl.BlockSpec(wsw.shape, const),
            pl.BlockSpec(e.shape, const),
            pl.BlockSpec(gc.shape, const),
            pl.BlockSpec(gs.shape, const),
            pl.BlockSpec((tm, LANES), lambda i: (i % npos, 0)),
            pl.BlockSpec((tm, LANES), lambda i: (i % npos, 0)),
        ],
        out_specs=(pl.BlockSpec((tm, nqk // 2), lambda i: (i, 0)),
                   pl.BlockSpec((tm, nqk // 2), lambda i: (i, 0)),
                   pl.BlockSpec((tm, w.shape[1] - nqk), lambda i: (i, 0))),
        compiler_params=_cparams(("parallel",)),
        name="proj_a",
    )(x, g, w, wsw, e, gc, gs, cos, sin)


def _outproj_kernel(x_ref, oa_ref, ob_ref, oc_ref, od_ref, w_ref, o_ref):
    acc = x_ref[...]
    for m, ref in enumerate((oa_ref, ob_ref, oc_ref, od_ref)):
        acc = acc + jnp.dot(ref[...], w_ref[m * GROUP_WIDTH:(m + 1) * GROUP_WIDTH, :],
                            preferred_element_type=F32)
    o_ref[...] = acc


def _outproj(x, oa, ob, oc, od, w):
    n, d = x.shape
    tm = min(OUT_TM, n)
    mix = pl.BlockSpec((tm, GROUP_WIDTH), lambda i: (i, 0))
    return pl.pallas_call(
        _outproj_kernel,
        out_shape=jax.ShapeDtypeStruct((n, d), F32),
        grid=(n // tm,),
        in_specs=[pl.BlockSpec((tm, d), lambda i: (i, 0)), mix, mix, mix, mix,
                  pl.BlockSpec(w.shape, lambda i: (0, 0))],
        out_specs=pl.BlockSpec((tm, d), lambda i: (i, 0)),
        compiler_params=_cparams(("parallel",)),
        name="out_proj",
    )(x, oa, ob, oc, od, w)


def _attn_init(q_ref, qm_ref, m_ref, acc_ref):
    q = q_ref[...]
    unit = lax.broadcasted_iota(jnp.int32, q.shape, 1) // HEAD_DIM
    for u in range(qm_ref.shape[0]):
        qm_ref[u] = jnp.where(unit == u, q, jnp.zeros_like(q))
    m_ref[...] = jnp.full(m_ref.shape, -jnp.inf, F32)
    acc_ref[...] = jnp.zeros(acc_ref.shape, F32)


def _scores(k, qm_ref, u):
    return lax.dot_general(k, qm_ref[u], (((1,), (1,)), ((), ())), preferred_element_type=F32)


def _attn_units(k, qm_ref, update, st0_ref=None, k_next=None):
    n_units = qm_ref.shape[0]
    st_next = _scores(k, qm_ref, 0) if st0_ref is None else st0_ref[...]
    for u in range(n_units):
        st = st_next
        if u + 1 < n_units:
            st_next = _scores(k, qm_ref, u + 1)
        elif st0_ref is not None:
            st0_ref[...] = _scores(k_next, qm_ref, 0)
        update(u, st)


def _attn_tile(st, vt, u, m_ref, acc_ref, bias=None, shift=None):
    if bias is not None:
        st = st + bias
    smax = jnp.max(st, axis=0, keepdims=True)
    if shift is not None:
        smax = smax + shift
    m_old = m_ref[u]
    m_new = jnp.maximum(m_old, smax)
    sub = m_new if shift is None else m_new - shift
    p = jnp.exp2((st - sub).astype(BF16))
    acc_ref[u] = jnp.exp2(m_old - m_new) * acc_ref[u] + jnp.dot(vt, p, preferred_element_type=F32)
    m_ref[u] = m_new


def _attn_finish(o_ref, acc_ref, dv, diff_refs, lam_init):
    n_units = acc_ref.shape[0]
    outs = [acc_ref[u, :dv, :] / acc_ref[u, dv:dv + 1, :] for u in range(n_units)]
    if diff_refs:
        lam_ref, subg_ref = diff_refs
        lp = lam_ref[...]
        lam = (jnp.exp(jnp.sum(lp[0:1] * lp[1:2], keepdims=True))
               - jnp.exp(jnp.sum(lp[2:3] * lp[3:4], keepdims=True)) + lam_init)
        outs = [outs[2 * h] - lam * outs[2 * h + 1] for h in range(n_units // 2)]
    o = jnp.concatenate(outs, axis=0).T
    if diff_refs:
        o = jnp.concatenate([_rms(o[:, lo:lo + dv], subg_ref[...]) * (1.0 - lam_init)
                             for lo in range(0, o.shape[1], dv)], axis=1)
    o_ref[...] = o.astype(BF16)


def _value_rows(u, n_units, rows_in_block, dv):
    per_head = dv + ONES_ROWS
    heads_in_block = rows_in_block // per_head
    lo = (u * heads_in_block // n_units) * per_head
    return lo, lo + per_head


def _band_offset(s):
    mag = (s + 1) // 2
    return jnp.where(s % 2 == 1, -mag, mag)


def _banded_kernel(q_ref, k_ref, vt_ref, bias_ref, o_ref, qm_ref, m_ref, acc_ref, *, dv):
    i, s = pl.program_id(2), pl.program_id(3)
    n_units = qm_ref.shape[0]

    @pl.when(s == 0)
    def _():
        _attn_init(q_ref, qm_ref, m_ref, acc_ref)

    jt = i + _band_offset(s)

    @pl.when((jt >= 0) & (jt < pl.num_programs(2)))
    def _():
        def update(u, st):
            lo, hi = _value_rows(u, n_units, vt_ref.shape[0], dv)
            _attn_tile(st, vt_ref[lo:hi, :], u, m_ref, acc_ref, bias=bias_ref[u])
        _attn_units(k_ref[...], qm_ref, update)

    @pl.when(s == pl.num_programs(3) - 1)
    def _():
        _attn_finish(o_ref, acc_ref, dv, None, 0.0)


def _banded_attention(qkv, vt, bias, *, t_len, tile, qcol, kcol, n_steps, bias_tile_fn):
    n = qkv.shape[0]
    batch, nt = n // t_len, t_len // tile
    dv = HEAD_DIM
    rows = vt.shape[1] // GROUPS
    jt = lambda i, s: jnp.clip(i + _band_offset(s), 0, nt - 1)
    return pl.pallas_call(
        functools.partial(_banded_kernel, dv=dv),
        out_shape=jax.ShapeDtypeStruct((n, GROUP_WIDTH), BF16),
        grid=(batch, GROUPS, nt, n_steps),
        in_specs=[
            pl.BlockSpec((tile, MXU_DIM), lambda b, g, i, s: (b * nt + i, qcol + g)),
            pl.BlockSpec((tile, MXU_DIM), lambda b, g, i, s: (b * nt + jt(i, s), kcol + g)),
            pl.BlockSpec((None, rows, tile), lambda b, g, i, s: (b, g, jt(i, s))),
            pl.BlockSpec((None, None, HEADS_PER_GROUP, tile, tile),
                         lambda b, g, i, s: (bias_tile_fn(i, _band_offset(s), nt), g, 0, 0, 0)),
        ],
        out_specs=pl.BlockSpec((tile, MXU_DIM), lambda b, g, i, s: (b * nt + i, g)),
        scratch_shapes=[pltpu.VMEM((HEADS_PER_GROUP, tile, MXU_DIM), BF16),
                        pltpu.VMEM((HEADS_PER_GROUP, 1, tile), F32),
                        pltpu.VMEM((HEADS_PER_GROUP, dv + ONES_ROWS, tile), F32)],
        compiler_params=_cparams(("parallel", "parallel", "parallel", "arbitrary")),
        name="attn_banded",
    )(qkv, qkv, vt, bias)


def _dense_kernel(*refs, dv, bias_heads, lam_init):
    it = iter(refs)
    q_ref, k_ref, vt_ref = next(it), next(it), next(it)
    bias_ref, far_ref = (next(it), next(it)) if bias_heads else (None, None)
    diff_refs = (next(it), next(it)) if lam_init is not None else None
    o_ref, qm_ref, m_ref, acc_ref, st0_ref = next(it), next(it), next(it), next(it), next(it)

    i = pl.program_id(2)
    n_units = qm_ref.shape[0]
    nk, _, tile = vt_ref.shape
    _attn_init(q_ref, qm_ref, m_ref, acc_ref)

    def keys(j):
        return k_ref[pl.ds(pl.multiple_of(j * tile, tile), tile), :]

    st0_ref[...] = _scores(keys(0), qm_ref, 0)

    def visit(j_lo, j_hi, side):
        def body(j, carry):
            def update(u, st):
                lo, hi = _value_rows(u, n_units, vt_ref.shape[1], dv)
                h = u * bias_heads // n_units
                bias = bias_ref[j - i + B_NEAR, h] if side == 2 else None
                shift = far_ref[side * bias_heads + h:side * bias_heads + h + 1, :] if side in (0, 1) else None
                _attn_tile(st, vt_ref[j, lo:hi, :], u, m_ref, acc_ref, bias=bias, shift=shift)
            _attn_units(keys(j), qm_ref, update, st0_ref=st0_ref, k_next=keys(jnp.minimum(j + 1, nk - 1)))
            return carry
        lax.fori_loop(j_lo, j_hi, body, 0)

    if bias_heads:
        near_lo, near_hi = jnp.maximum(i - B_NEAR, 0), jnp.minimum(i + B_NEAR + 1, nk)
        visit(0, near_lo, 0)
        visit(near_lo, near_hi, 2)
        visit(near_hi, nk, 1)
    else:
        visit(0, nk, None)
    _attn_finish(o_ref, acc_ref, dv, diff_refs, lam_init)


def _dense_attention(q, k, vt, *, t_len, qcol, kcol, bias=None, far=None, diff=None):
    n = q.shape[0]
    tile = ATT_T
    batch, nt = n // t_len, t_len // tile
    dv = B_V_DIM if diff else HEAD_DIM
    rows = vt.shape[3]
    once = pl.Buffered(1)
    in_specs = [
        pl.BlockSpec((tile, MXU_DIM), lambda b, g, i: (b * nt + i, qcol + g)),
        pl.BlockSpec((t_len, MXU_DIM), lambda b, g, i: (b, kcol + g), pipeline_mode=once),
        pl.BlockSpec((None, None, nt, rows, tile), lambda b, g, i: (b, g, 0, 0, 0), pipeline_mode=once),
    ]
    args = [q, k, vt]
    bias_heads = 0
    if bias is not None:
        bias_heads = bias.shape[2]
        in_specs += [pl.BlockSpec((bias.shape[0], None, bias_heads, tile, tile),
                                  lambda b, g, i: (0, g, 0, 0, 0), pipeline_mode=once),
                     pl.BlockSpec((None,) + far.shape[1:], lambda b, g, i: (g, 0, 0))]
        args += [bias, far]
    lam_init = None
    if diff:
        lam_p, subg, lam_init = diff
        in_specs += [pl.BlockSpec(lam_p.shape, lambda b, g, i: (0, 0)),
                     pl.BlockSpec(subg.shape, lambda b, g, i: (0, 0))]
        args += [lam_p, subg]
    return pl.pallas_call(
        functools.partial(_dense_kernel, dv=dv, bias_heads=bias_heads, lam_init=lam_init),
        out_shape=jax.ShapeDtypeStruct((n, GROUP_WIDTH), BF16),
        grid=(batch, GROUPS, nt),
        in_specs=in_specs,
        out_specs=pl.BlockSpec((tile, MXU_DIM), lambda b, g, i: (b * nt + i, g)),
        scratch_shapes=[pltpu.VMEM((HEADS_PER_GROUP, tile, MXU_DIM), BF16),
                        pltpu.VMEM((HEADS_PER_GROUP, 1, tile), F32),
                        pltpu.VMEM((HEADS_PER_GROUP, dv + ONES_ROWS, tile), F32),
                        pltpu.VMEM((tile, tile), F32)],
        compiler_params=_cparams(("parallel", "parallel", "arbitrary")),
        name="attn_dense",
    )(*args)


def _t5_bucket(rel):
    nb = T5_BUCKETS // 2
    max_exact = nb // 2
    side = (rel > 0).astype(jnp.int32) * nb
    n = jnp.abs(rel)
    large = max_exact + (jnp.log(jnp.maximum(n, 1).astype(F32) / max_exact)
                         / math.log(T5_MAX_DIST / max_exact) * (nb - max_exact)).astype(jnp.int32)
    large = jnp.minimum(large, nb - 1)
    return side + jnp.where(n < max_exact, n, large)


def _toeplitz(v, n):
    lead = v.shape[:-1]
    period = 2 * n
    vp = jnp.concatenate([v, jnp.zeros(lead + (1,), v.dtype)], axis=-1)
    flat = jnp.broadcast_to(vp[..., None, :], lead + (n + 1, period)).reshape(lead + ((n + 1) * period,))
    hankel = flat[..., :n * (period + 1)].reshape(lead + (n, period + 1))[..., :n]
    return hankel[..., ::-1]


def _rel_tiles(table, reach, n):
    return jnp.stack([_toeplitz(table[:, o * n:o * n + 2 * n - 1], n) for o in range(2 * reach + 1)])


def _group_heads(b):
    t, h = b.shape[:2]
    return b.reshape(t, GROUPS, h // GROUPS, *b.shape[2:])


def _t5_saturation():
    nb = T5_BUCKETS // 2
    max_exact = nb // 2
    return math.ceil(max_exact * (T5_MAX_DIST / max_exact) ** ((nb - 1 - max_exact) / (nb - max_exact))) + 1


def _bias_b(t5_b):
    n, reach = ATT_T, B_NEAR
    assert reach * n + 1 >= _t5_saturation()
    rel = jnp.arange(-(reach * n + n - 1), reach * n + n, dtype=jnp.int32)
    table = (t5_b[_t5_bucket(rel)].astype(F32) * LOG2E).T
    far = (t5_b[_t5_bucket(jnp.array([-T5_MAX_DIST * 2, T5_MAX_DIST * 2], jnp.int32))].astype(F32) * LOG2E)
    heads = t5_b.shape[1]
    far = far.reshape(2, GROUPS, heads // GROUPS).transpose(1, 0, 2).reshape(GROUPS, 2 * heads // GROUPS, 1)
    return _group_heads(_rel_tiles(table, reach, n)), jnp.broadcast_to(far, far.shape[:2] + (n,))


def _bias_d(t5_d):
    n = ATT_T
    reach = max(w // 2 for w, _ in D_BRANCHES) // n
    rel = jnp.arange(-(reach * n + n - 1), reach * n + n, dtype=jnp.int32)
    mult = sum(((jnp.abs(rel) <= w // 2) & (rel % d == 0)).astype(F32) for w, d in D_BRANCHES)
    b = t5_d[_t5_bucket(rel)].astype(F32).T
    table = jnp.where(mult[None] > 0, (b + jnp.log(jnp.maximum(mult, 1.0))[None]) * LOG2E, NEG_INF)
    return _group_heads(_rel_tiles(table, reach, n)), reach


def _bias_c(rpb, rows):
    heads = rpb.shape[0]
    kr = min(C_WIN_ROWS, rows)
    n_row_tiles = rows // C_ROWS_PER_TILE
    pad = GRID_W - C_WIN_COLS
    col_tiles = _toeplitz(jnp.pad(rpb.astype(F32) * LOG2E, ((0, 0), (0, 0), (pad, pad))), GRID_W)
    c = np.arange(GRID_W)[None, :]
    kc = np.arange(GRID_W)[:, None]
    cs = np.clip(c - C_WIN_COLS // 2, 0, GRID_W - C_WIN_COLS)
    col_ok = jnp.asarray((kc >= cs) & (kc < cs + C_WIN_COLS))
    col_tiles = jnp.where(col_ok, col_tiles, NEG_INF)
    masked = jnp.full((heads, GRID_W, GRID_W), NEG_INF, F32)
    tiles = []
    for it in (0, min(1, n_row_tiles - 1), n_row_tiles - 1):
        for off in (-1, 0, 1):
            key_rows = []
            for a in range(C_ROWS_PER_TILE):
                blocks = []
                for b in range(C_ROWS_PER_TILE):
                    r = it * C_ROWS_PER_TILE + b
                    key_r = (it + off) * C_ROWS_PER_TILE + a
                    rs = min(max(r - kr // 2, 0), rows - kr)
                    ok = 0 <= key_r < rows and rs <= key_r < rs + kr
                    blocks.append(col_tiles[:, key_r - r + C_WIN_ROWS - 1] if ok else masked)
                key_rows.append(jnp.concatenate(blocks, axis=2))
            tiles.append(jnp.concatenate(key_rows, axis=1))
    return _group_heads(jnp.stack(tiles))


def _c_tile_index(i, off, nt):
    variant = jnp.where(i == 0, 0, jnp.where(i == nt - 1, 2, 1))
    return variant * 3 + off + 1


def _rope_tables(t_len):
    t = jnp.arange(t_len, dtype=jnp.int32)
    n_freq = HEAD_DIM // 4
    inv_freq = ROPE_THETA ** (-jnp.arange(n_freq, dtype=F32) / n_freq)
    ang = jnp.concatenate([(t // GRID_W).astype(F32)[:, None] * inv_freq[None, :],
                           (t % GRID_W).astype(F32)[:, None] * inv_freq[None, :]], axis=-1)
    cos, sin = jnp.cos(ang), jnp.sin(ang)
    reps = LANES // HEAD_DIM
    return (jnp.tile(jnp.concatenate([cos, cos], axis=-1), (1, reps)),
            jnp.tile(jnp.concatenate([-sin, sin], axis=-1), (1, reps)))


def _deinterleave(w):
    return jnp.swapaxes(w.reshape(w.shape[:-1] + (HEAD_DIM // 2, 2)), -1, -2).reshape(w.shape)


def _swap_halves(w):
    return jnp.concatenate([w[..., HEAD_DIM // 2:], w[..., :HEAD_DIM // 2]], axis=-1)


def _prep_ffn(w_gate, w_up, w_down):
    d, dff = w_gate.shape
    nt = dff // FFN_TF
    wgu = jnp.stack([w_gate.reshape(d, nt, FFN_TF), w_up.reshape(d, nt, FFN_TF)], axis=2)
    return wgu.reshape(d, 2 * dff).astype(BF16), w_down.astype(BF16)


def _prep_mixer_a(w_in, g_q, g_k):
    d = w_in.shape[0]
    qw = A_HEADS * HEAD_DIM
    kw = A_KV_HEADS * HEAD_DIM
    rep = A_HEADS // A_KV_HEADS
    wq = _deinterleave(w_in[:, :qw].reshape(d, A_HEADS, HEAD_DIM))
    wk = _deinterleave(w_in[:, qw:qw + kw].reshape(d, A_KV_HEADS, 1, HEAD_DIM))
    wk = jnp.broadcast_to(wk, (d, A_KV_HEADS, rep, HEAD_DIM)).reshape(d, A_HEADS, HEAD_DIM)
    wqk = jnp.concatenate([wq, wk], axis=1)
    wv = w_in[:, qw + kw:qw + 2 * kw]
    w = jnp.concatenate([wqk.reshape(d, -1), wv], axis=1).astype(BF16)
    wsw = _swap_halves(wqk).reshape(d, -1).astype(BF16)
    gq, gk = _deinterleave(g_q) * (SCALE * LOG2E), _deinterleave(g_k)
    g = jnp.concatenate([jnp.tile(gq, A_HEADS), jnp.tile(gk, A_HEADS)])
    gsw = jnp.concatenate([jnp.tile(_swap_halves(gq), A_HEADS), jnp.tile(_swap_halves(gk), A_HEADS)])
    return w, wsw, g[None, :].astype(F32), gsw[None, :].astype(F32)


def _prep_mixers_bcd(w_in):
    a_cols = (A_HEADS + 2 * A_KV_HEADS) * HEAD_DIM
    w = w_in[:, a_cols:]
    is_q = (np.arange(w.shape[1]) // GROUP_WIDTH) % 3 == 0
    return (w * jnp.asarray(np.where(is_q, SCALE * LOG2E, 1.0), F32)[None, :]).astype(BF16)


def _head_sum_matrix():
    idx = np.arange(MXU_DIM) // HEAD_DIM
    return jnp.asarray(idx[:, None] == idx[None, :], dtype=BF16)


def _values_t(v, batch, t_len, dv):
    heads = v.shape[1] // dv
    v4 = v.reshape(batch, t_len, heads, dv)
    v4 = jnp.concatenate([v4, jnp.ones((batch, t_len, heads, ONES_ROWS), v.dtype)], axis=-1)
    return jnp.transpose(v4, (0, 2, 3, 1)).reshape(batch, heads * (dv + ONES_ROWS), t_len)


def _key_tiled(vt, tile):
    batch, r, t_len = vt.shape
    return jnp.transpose(vt.reshape(batch, GROUPS, r // GROUPS, t_len // tile, tile), (0, 1, 3, 2, 4))


def _trunk(x3, layers, bias_b, far_b, bias_d, d_reach, final_norm):
    batch, t_len, d = x3.shape
    x = x3.reshape(batch * t_len, d)
    cos, sin = _rope_tables(t_len)
    e = _head_sum_matrix()
    depth = len(layers)
    for l, p in enumerate(layers):
        x = _ffn(x, p["ffn1_norm"], *p["ffn1"], final_norm, final=False)
        qa, ka, va = _aproj(x, p["mix_norm"], p["wa"], p["wa_sw"], e, p["ga"], p["ga_sw"], cos, sin, t_len)
        qkv = _proj(x, p["mix_norm"], p["w_bcd"])

        def values(m, dv):
            lo = (3 * m + 2) * GROUP_WIDTH
            return _values_t(qkv[:, lo:lo + GROUP_WIDTH], batch, t_len, dv)

        oa = _dense_attention(qa, ka, _key_tiled(_values_t(va, batch, t_len, HEAD_DIM), ATT_T),
                              t_len=t_len, qcol=0, kcol=0)
        ob = _dense_attention(qkv, qkv, _key_tiled(values(0, B_V_DIM), ATT_T), t_len=t_len,
                              qcol=0, kcol=GROUPS, bias=bias_b, far=far_b,
                              diff=(p["b_lambda"], p["b_subln"], 0.8 - 0.6 * math.exp(-0.3 * l)))
        oc = _banded_attention(qkv, values(1, HEAD_DIM), _bias_c(p["c_rpb"], t_len // GRID_W), t_len=t_len,
                               tile=C_T, qcol=3 * GROUPS, kcol=4 * GROUPS, n_steps=3,
                               bias_tile_fn=_c_tile_index)
        od = _banded_attention(qkv, values(2, HEAD_DIM), bias_d, t_len=t_len, tile=ATT_T,
                               qcol=6 * GROUPS, kcol=7 * GROUPS, n_steps=2 * d_reach + 1,
                               bias_tile_fn=lambda i, off, nt: off + d_reach)
        x = _outproj(x, oa, ob, oc, od, p["w_out"])
        x = _ffn(x, p["ffn2_norm"], *p["ffn2"], final_norm, final=(l == depth - 1))
    return x.reshape(batch, t_len, d)


def kernel(x_prompt, x_sample, ffn1_norm, ffn1_w_gate, ffn1_w_up, ffn1_w_down, mix_norm, w_in, a_q_norm, a_k_norm, b_lambda, b_subln, c_rpb, w_out, ffn2_norm, ffn2_w_gate, ffn2_w_up, ffn2_w_down, t5_table, final_norm):
    depth = w_in.shape[0]
    layers = []
    for l in range(depth):
        wa, wa_sw, ga, ga_sw = _prep_mixer_a(w_in[l], a_q_norm[l], a_k_norm[l])
        layers.append(dict(
            ffn1_norm=ffn1_norm[l][None, :], ffn2_norm=ffn2_norm[l][None, :], mix_norm=mix_norm[l][None, :],
            ffn1=_prep_ffn(ffn1_w_gate[l], ffn1_w_up[l], ffn1_w_down[l]),
            ffn2=_prep_ffn(ffn2_w_gate[l], ffn2_w_up[l], ffn2_w_down[l]),
            wa=wa, wa_sw=wa_sw, ga=ga, ga_sw=ga_sw,
            w_bcd=_prep_mixers_bcd(w_in[l]),
            b_lambda=b_lambda[l].astype(F32), b_subln=b_subln[l][None, :].astype(F32),
            c_rpb=c_rpb[l], w_out=w_out[l].astype(BF16)))
    bias_b, far_b = _bias_b(t5_table[:, :B_HEADS])
    bias_d, d_reach = _bias_d(t5_table[:, B_HEADS:])
    fn = final_norm[None, :]
    return tuple(_trunk(x3, layers, bias_b, far_b, bias_d, d_reach, fn) for x3 in (x_prompt, x_sample))
```

```python
import functools
import math

import jax
import jax.numpy as jnp
import numpy as np
from jax import lax
from jax.experimental import pallas as pl
from jax.experimental.pallas import tpu as pltpu

F32 = jnp.float32
BF16 = jnp.bfloat16

HEAD_DIM = 64
GRID_W = 64
EPS = 1e-6
NEG_INF = -1e30
SCALE = HEAD_DIM ** -0.5
LOG2E = math.log2(math.e)
ROPE_THETA = 10000.0
A_HEADS, A_KV_HEADS = 8, 2
B_HEADS, B_V_DIM = 4, 128
C_HEADS, C_WIN_ROWS, C_WIN_COLS = 8, 8, 16
D_HEADS = 8
D_BRANCHES = ((128, 1), (512, 4), (2048, 16))
T5_BUCKETS, T5_MAX_DIST = 32, 1024
GROUP_WIDTH = 512

LANES = 128
BF16_SUBLANES = 16
MXU_DIM = 256
HEADS_PER_GROUP = MXU_DIM // HEAD_DIM
GROUPS = GROUP_WIDTH // MXU_DIM
VMEM_LIMIT = 56 * 1024 * 1024
FFN_TM, FFN_TF = 1024, 256
PROJ_TM, PROJ_TN = 1024, 1536
APROJ_TM = 512
OUT_TM = 512
ATT_T = 512
C_ROWS_PER_TILE = 4
C_T = C_ROWS_PER_TILE * GRID_W
ONES_ROWS = BF16_SUBLANES
B_NEAR = 2


def _cparams(sem):
    return pltpu.CompilerParams(dimension_semantics=sem, vmem_limit_bytes=VMEM_LIMIT)


def _rms(x, g):
    return x * lax.rsqrt(jnp.mean(x * x, axis=-1, keepdims=True) + EPS) * g


def _ffn_kernel(x_ref, g_ref, wgu_ref, wd_ref, fg_ref, o_ref, xn_ref, *, final):
    j = pl.program_id(1)

    @pl.when(j == 0)
    def _():
        x = x_ref[...]
        xn_ref[...] = _rms(x, g_ref[...]).astype(BF16)
        o_ref[...] = x

    h = jnp.dot(xn_ref[...], wgu_ref[...], preferred_element_type=F32)
    tf = h.shape[1] // 2
    hg, hu = h[:, :tf], h[:, tf:]
    a = (hg * jax.nn.sigmoid(hg)) * hu
    o_ref[...] += jnp.dot((0.5 * a).astype(BF16), wd_ref[...], preferred_element_type=F32)

    if final:
        @pl.when(j == pl.num_programs(1) - 1)
        def _():
            o_ref[...] = _rms(o_ref[...], fg_ref[...])


def _ffn(x, g, wgu, wd, fg, *, final):
    n, d = x.shape
    dff = wd.shape[0]
    tm, tf = min(FFN_TM, n), FFN_TF
    return pl.pallas_call(
        functools.partial(_ffn_kernel, final=final),
        out_shape=jax.ShapeDtypeStruct((n, d), F32),
        grid=(n // tm, dff // tf),
        in_specs=[
            pl.BlockSpec((tm, d), lambda i, j: (i, 0)),
            pl.BlockSpec((1, d), lambda i, j: (0, 0)),
            pl.BlockSpec((d, 2 * tf), lambda i, j: (0, j)),
            pl.BlockSpec((tf, d), lambda i, j: (j, 0)),
            pl.BlockSpec((1, d), lambda i, j: (0, 0)),
        ],
        out_specs=pl.BlockSpec((tm, d), lambda i, j: (i, 0)),
        scratch_shapes=[pltpu.VMEM((tm, d), BF16)],
        compiler_params=_cparams(("parallel", "arbitrary")),
        name="ffn",
    )(x, g, wgu, wd, fg)


def _proj_kernel(x_ref, g_ref, w_ref, o_ref, xn_ref):
    @pl.when(pl.program_id(1) == 0)
    def _():
        xn_ref[...] = _rms(x_ref[...], g_ref[...]).astype(BF16)

    o_ref[...] = jnp.dot(xn_ref[...], w_ref[...], preferred_element_type=F32).astype(BF16)


def _proj(x, g, w):
    n, d = x.shape
    nc = w.shape[1]
    tm, tn = min(PROJ_TM, n), PROJ_TN
    return pl.pallas_call(
        _proj_kernel,
        out_shape=jax.ShapeDtypeStruct((n, nc), BF16),
        grid=(n // tm, nc // tn),
        in_specs=[
            pl.BlockSpec((tm, d), lambda i, j: (i, 0)),
            pl.BlockSpec((1, d), lambda i, j: (0, 0)),
            pl.BlockSpec((d, tn), lambda i, j: (0, j)),
        ],
        out_specs=pl.BlockSpec((tm, tn), lambda i, j: (i, j)),
        scratch_shapes=[pltpu.VMEM((tm, d), BF16)],
        compiler_params=_cparams(("parallel", "arbitrary")),
        name="proj_bcd",
    )(x, g, w)


def _aproj_kernel(x_ref, g_ref, w_ref, wsw_ref, e_ref, gc_ref, gs_ref, cos_ref, sin_ref,
                  q_ref, k_ref, v_ref):
    xn = _rms(x_ref[...], g_ref[...]).astype(BF16)
    y = jnp.dot(xn, w_ref[...], preferred_element_type=F32)
    ysw = jnp.dot(xn, wsw_ref[...], preferred_element_type=F32)
    nqk = ysw.shape[1]
    v_ref[...] = y[:, nqk:].astype(BF16)
    e = e_ref[...]
    cos = jnp.concatenate([cos_ref[...]] * (MXU_DIM // LANES), axis=1)
    sin = jnp.concatenate([sin_ref[...]] * (MXU_DIM // LANES), axis=1)
    half = nqk // 2
    for lo in range(0, nqk, MXU_DIM):
        yc = y[:, lo:lo + MXU_DIM]
        sq = yc * yc
        hi = sq.astype(BF16)
        rest = (sq - hi.astype(F32)).astype(BF16)
        ssum = (jnp.dot(hi, e, preferred_element_type=F32)
                + jnp.dot(rest, e, preferred_element_type=F32))
        r = lax.rsqrt(ssum * (1.0 / HEAD_DIM) + EPS)
        out = (yc * gc_ref[:, lo:lo + MXU_DIM] * cos
               + ysw[:, lo:lo + MXU_DIM] * gs_ref[:, lo:lo + MXU_DIM] * sin) * r
        dst, off = (q_ref, lo) if lo < half else (k_ref, lo - half)
        dst[:, off:off + MXU_DIM] = out.astype(BF16)


def _aproj(x, g, w, wsw, e, gc, gs, cos, sin, t_len):
    n, d = x.shape
    tm = min(APROJ_TM, t_len)
    npos = t_len // tm
    nqk = wsw.shape[1]
    const = lambda i: (0, 0)
    return pl.pallas_call(
        _aproj_kernel,
        out_shape=(jax.ShapeDtypeStruct((n, nqk // 2), BF16),
                   jax.ShapeDtypeStruct((n, nqk // 2), BF16),
                   jax.ShapeDtypeStruct((n, w.shape[1] - nqk), BF16)),
        grid=(n // tm,),
        in_specs=[
            pl.BlockSpec((tm, d), lambda i: (i, 0)),
            pl.BlockSpec((1, d), const),
            pl.BlockSpec(w.shape, const),
            pl.BlockSpec(wsw.shape, const),
            pl.BlockSpec(e.shape, const),
            pl.BlockSpec(gc.shape, const),
            pl.BlockSpec(gs.shape, const),
            pl.BlockSpec((tm, LANES), lambda i: (i % npos, 0)),
            pl.BlockSpec((tm, LANES), lambda i: (i % npos, 0)),
        ],
        out_specs=(pl.BlockSpec((tm, nqk // 2), lambda i: (i, 0)),
                   pl.BlockSpec((tm, nqk // 2), lambda i: (i, 0)),
                   pl.BlockSpec((tm, w.shape[1] - nqk), lambda i: (i, 0))),
        compiler_params=_cparams(("parallel",)),
        name="proj_a",
    )(x, g, w, wsw, e, gc, gs, cos, sin)


def _outproj_kernel(x_ref, oa_ref, ob_ref, oc_ref, od_ref, w_ref, o_ref):
    acc = x_ref[...]
    for m, ref in enumerate((oa_ref, ob_ref, oc_ref, od_ref)):
        acc = acc + jnp.dot(ref[...], w_ref[m * GROUP_WIDTH:(m + 1) * GROUP_WIDTH, :],
                            preferred_element_type=F32)
    o_ref[...] = acc


def _outproj(x, oa, ob, oc, od, w):
    n, d = x.shape
    tm = min(OUT_TM, n)
    mix = pl.BlockSpec((tm, GROUP_WIDTH), lambda i: (i, 0))
    return pl.pallas_call(
        _outproj_kernel,
        out_shape=jax.ShapeDtypeStruct((n, d), F32),
        grid=(n // tm,),
        in_specs=[pl.BlockSpec((tm, d), lambda i: (i, 0)), mix, mix, mix, mix,
                  pl.BlockSpec(w.shape, lambda i: (0, 0))],
        out_specs=pl.BlockSpec((tm, d), lambda i: (i, 0)),
        compiler_params=_cparams(("parallel",)),
        name="out_proj",
    )(x, oa, ob, oc, od, w)


def _attn_init(q_ref, qm_ref):
    q = q_ref[...]
    unit = lax.broadcasted_iota(jnp.int32, q.shape, 1) // HEAD_DIM
    for u in range(qm_ref.shape[0]):
        qm_ref[u] = jnp.where(unit == u, q, jnp.zeros_like(q))


def _scores(k, qm_ref, u):
    return lax.dot_general(k, qm_ref[u], (((1,), (1,)), ((), ())), preferred_element_type=F32)


def _attn_units(k, qm_ref, update, st0_ref=None, k_next=None, from_st0=True):
    n_units = qm_ref.shape[0]
    st_next = st0_ref[...] if (st0_ref is not None and from_st0) else _scores(k, qm_ref, 0)
    for u in range(n_units):
        st = st_next
        if u + 1 < n_units:
            st_next = _scores(k, qm_ref, u + 1)
        elif st0_ref is not None:
            st0_ref[...] = _scores(k_next, qm_ref, 0)
        update(u, st)


def _tile_first(st, vt, u, m_ref, acc_ref, bias, shift):
    if bias is not None:
        st = st + bias
    m_tile = jnp.max(st, axis=0, keepdims=True)
    m_ref[u] = m_tile if shift is None else m_tile + shift
    acc_ref[u] = jnp.dot(vt, jnp.exp2(st - m_tile).astype(BF16), preferred_element_type=F32)


def _tile_fixed(st, vt, u, m_ref, acc_ref, bias, shift):
    if bias is not None:
        st = st + bias
    sub = m_ref[u] if shift is None else m_ref[u] - shift
    acc_ref[u] += jnp.dot(vt, jnp.exp2(st - sub).astype(BF16), preferred_element_type=F32)


def _tile_online(st, vt, u, m_ref, acc_ref, bias, shift):
    if bias is not None:
        st = st + bias
    smax = jnp.max(st, axis=0, keepdims=True)
    if shift is not None:
        smax = smax + shift
    m_old = m_ref[u]
    m_new = jnp.maximum(m_old, smax)
    sub = m_new if shift is None else m_new - shift
    p = jnp.exp2(st - sub).astype(BF16)
    acc_ref[u] = jnp.exp2(m_old - m_new) * acc_ref[u] + jnp.dot(vt, p, preferred_element_type=F32)
    m_ref[u] = m_new


def _attn_finish(o_ref, acc_ref, dv, diff_refs, lam_init):
    n_units = acc_ref.shape[0]
    outs = [acc_ref[u, :dv, :] / acc_ref[u, dv:dv + 1, :] for u in range(n_units)]
    if diff_refs:
        lam_ref, subg_ref = diff_refs
        lp = lam_ref[...]
        lam = (jnp.exp(jnp.sum(lp[0:1] * lp[1:2], keepdims=True))
               - jnp.exp(jnp.sum(lp[2:3] * lp[3:4], keepdims=True)) + lam_init)
        outs = [outs[2 * h] - lam * outs[2 * h + 1] for h in range(n_units // 2)]
    o = jnp.concatenate(outs, axis=0).T
    if diff_refs:
        o = jnp.concatenate([_rms(o[:, lo:lo + dv], subg_ref[...]) * (1.0 - lam_init)
                             for lo in range(0, o.shape[1], dv)], axis=1)
    o_ref[...] = o.astype(BF16)


def _value_rows(u, n_units, rows_in_block, dv):
    per_head = dv + ONES_ROWS
    heads_in_block = rows_in_block // per_head
    lo = (u * heads_in_block // n_units) * per_head
    return lo, lo + per_head


PLAIN, FAR_LEFT, FAR_RIGHT, NEAR = "plain", 0, 1, "near"


def _attn_kernel(*refs, dv, reach, far, bias_index, lam_init):
    it = iter(refs)
    q_ref, k_ref, vt_ref = next(it), next(it), next(it)
    bias_ref = next(it) if reach is not None else None
    far_ref = next(it) if far else None
    diff_refs = (next(it), next(it)) if lam_init is not None else None
    o_ref, qm_ref, m_ref, acc_ref, st0_ref = next(it), next(it), next(it), next(it), next(it)

    i = pl.program_id(2)
    n_units = qm_ref.shape[0]
    nk, rows, tile = vt_ref.shape
    bias_heads = bias_ref.shape[1] if bias_ref is not None else 0
    _attn_init(q_ref, qm_ref)
    lo, hi = (0, nk) if reach is None else (jnp.maximum(i - reach, 0), jnp.minimum(i + reach + 1, nk))
    p_first, p_end = (1, nk) if (reach is None or far) else (lo + 1, hi)

    def tile_at(p):
        p = jnp.minimum(p, p_end - 1)
        return jnp.clip(p - (p <= i).astype(jnp.int32), 0, nk - 1)

    def keys(j):
        return k_ref[pl.ds(pl.multiple_of(j * tile, tile), tile), :]

    def updater(tile_fn, j, side):
        def update(u, st):
            r0, r1 = _value_rows(u, n_units, rows, dv)
            h = u * bias_heads // n_units
            bias = bias_ref[bias_index(i, j - i, nk), h] if side == NEAR else None
            shift = far_ref[side * bias_heads + h:side * bias_heads + h + 1, :] if side in (FAR_LEFT, FAR_RIGHT) else None
            tile_fn(st, vt_ref[j, r0:r1, :], u, m_ref, acc_ref, bias, shift)
        return update

    def visit_rest(tile_fn, pipelined):
        def span(p_lo, p_hi, side):
            def body(p, carry):
                j = tile_at(p)
                if pipelined:
                    _attn_units(keys(j), qm_ref, updater(tile_fn, j, side), st0_ref, keys(tile_at(p + 1)))
                else:
                    _attn_units(keys(j), qm_ref, updater(tile_fn, j, side))
                return carry
            lax.fori_loop(p_lo, p_hi, body, 0)
        if reach is None:
            span(1, nk, PLAIN)
        elif far:
            span(1, lo + 1, FAR_LEFT)
            span(lo + 1, hi, NEAR)
            span(hi, nk, FAR_RIGHT)
        else:
            span(lo + 1, hi, NEAR)

    diag = PLAIN if reach is None else NEAR
    _attn_units(keys(i), qm_ref, updater(_tile_first, i, diag), st0_ref, keys(tile_at(p_first)), from_st0=False)
    visit_rest(_tile_fixed, pipelined=True)

    finite = jnp.min(jnp.where(jnp.isfinite(acc_ref[...]), 1.0, 0.0))

    @pl.when(finite < 0.5)
    def _():
        m_ref[...] = jnp.full(m_ref.shape, -jnp.inf, F32)
        acc_ref[...] = jnp.zeros(acc_ref.shape, F32)
        _attn_units(keys(i), qm_ref, updater(_tile_online, i, diag))
        visit_rest(_tile_online, pipelined=False)

    _attn_finish(o_ref, acc_ref, dv, diff_refs, lam_init)


def _attention(q, k, vt, *, t_len, tile, qcol, kcol, bias=None, reach=None, bias_index=None, far=None, diff=None):
    n = q.shape[0]
    batch, nt = n // t_len, t_len // tile
    dv = B_V_DIM if diff else HEAD_DIM
    rows = vt.shape[3]
    once = pl.Buffered(1)
    in_specs = [
        pl.BlockSpec((tile, MXU_DIM), lambda b, g, i: (b * nt + i, qcol + g)),
        pl.BlockSpec((t_len, MXU_DIM), lambda b, g, i: (b, kcol + g), pipeline_mode=once),
        pl.BlockSpec((None, None, nt, rows, tile), lambda b, g, i: (b, g, 0, 0, 0), pipeline_mode=once),
    ]
    args = [q, k, vt]
    if bias is not None:
        in_specs.append(pl.BlockSpec((bias.shape[0], None, bias.shape[2], tile, tile),
                                     lambda b, g, i: (0, g, 0, 0, 0), pipeline_mode=once))
        args.append(bias)
    if far is not None:
        in_specs.append(pl.BlockSpec((None,) + far.shape[1:], lambda b, g, i: (g, 0, 0)))
        args.append(far)
    lam_init = None
    if diff:
        lam_p, subg, lam_init = diff
        in_specs += [pl.BlockSpec(lam_p.shape, lambda b, g, i: (0, 0)),
                     pl.BlockSpec(subg.shape, lambda b, g, i: (0, 0))]
        args += [lam_p, subg]
    return pl.pallas_call(
        functools.partial(_attn_kernel, dv=dv, reach=reach, far=far is not None, bias_index=bias_index,
                          lam_init=lam_init),
        out_shape=jax.ShapeDtypeStruct((n, GROUP_WIDTH), BF16),
        grid=(batch, GROUPS, nt),
        in_specs=in_specs,
        out_specs=pl.BlockSpec((tile, MXU_DIM), lambda b, g, i: (b * nt + i, g)),
        scratch_shapes=[pltpu.VMEM((HEADS_PER_GROUP, tile, MXU_DIM), BF16),
                        pltpu.VMEM((HEADS_PER_GROUP, 1, tile), F32),
                        pltpu.VMEM((HEADS_PER_GROUP, dv + ONES_ROWS, tile), F32),
                        pltpu.VMEM((tile, tile), F32)],
        compiler_params=_cparams(("parallel", "parallel", "arbitrary")),
        name="attention",
    )(*args)


def _t5_bucket(rel):
    nb = T5_BUCKETS // 2
    max_exact = nb // 2
    side = (rel > 0).astype(jnp.int32) * nb
    n = jnp.abs(rel)
    large = max_exact + (jnp.log(jnp.maximum(n, 1).astype(F32) / max_exact)
                         / math.log(T5_MAX_DIST / max_exact) * (nb - max_exact)).astype(jnp.int32)
    large = jnp.minimum(large, nb - 1)
    return side + jnp.where(n < max_exact, n, large)


def _toeplitz(v, n):
    lead = v.shape[:-1]
    period = 2 * n
    vp = jnp.concatenate([v, jnp.zeros(lead + (1,), v.dtype)], axis=-1)
    flat = jnp.broadcast_to(vp[..., None, :], lead + (n + 1, period)).reshape(lead + ((n + 1) * period,))
    hankel = flat[..., :n * (period + 1)].reshape(lead + (n, period + 1))[..., :n]
    return hankel[..., ::-1]


def _rel_tiles(table, reach, n):
    return jnp.stack([_toeplitz(table[:, o * n:o * n + 2 * n - 1], n) for o in range(2 * reach + 1)])


def _group_heads(b):
    t, h = b.shape[:2]
    return b.reshape(t, GROUPS, h // GROUPS, *b.shape[2:])


def _t5_saturation():
    nb = T5_BUCKETS // 2
    max_exact = nb // 2
    return math.ceil(max_exact * (T5_MAX_DIST / max_exact) ** ((nb - 1 - max_exact) / (nb - max_exact))) + 1


def _bias_b(t5_b):
    n, reach = ATT_T, B_NEAR
    assert reach * n + 1 >= _t5_saturation()
    rel = jnp.arange(-(reach * n + n - 1), reach * n + n, dtype=jnp.int32)
    table = (t5_b[_t5_bucket(rel)].astype(F32) * LOG2E).T
    far = (t5_b[_t5_bucket(jnp.array([-T5_MAX_DIST * 2, T5_MAX_DIST * 2], jnp.int32))].astype(F32) * LOG2E)
    heads = t5_b.shape[1]
    far = far.reshape(2, GROUPS, heads // GROUPS).transpose(1, 0, 2).reshape(GROUPS, 2 * heads // GROUPS, 1)
    return _group_heads(_rel_tiles(table, reach, n)), jnp.broadcast_to(far, far.shape[:2] + (n,))


def _bias_d(t5_d):
    n = ATT_T
    reach = max(w // 2 for w, _ in D_BRANCHES) // n
    rel = jnp.arange(-(reach * n + n - 1), reach * n + n, dtype=jnp.int32)
    mult = sum(((jnp.abs(rel) <= w // 2) & (rel % d == 0)).astype(F32) for w, d in D_BRANCHES)
    b = t5_d[_t5_bucket(rel)].astype(F32).T
    table = jnp.where(mult[None] > 0, (b + jnp.log(jnp.maximum(mult, 1.0))[None]) * LOG2E, NEG_INF)
    return _group_heads(_rel_tiles(table, reach, n)), reach


def _bias_c(rpb, rows):
    heads = rpb.shape[0]
    kr = min(C_WIN_ROWS, rows)
    n_row_tiles = rows // C_ROWS_PER_TILE
    pad = GRID_W - C_WIN_COLS
    col_tiles = _toeplitz(jnp.pad(rpb.astype(F32) * LOG2E, ((0, 0), (0, 0), (pad, pad))), GRID_W)
    c = np.arange(GRID_W)[None, :]
    kc = np.arange(GRID_W)[:, None]
    cs = np.clip(c - C_WIN_COLS // 2, 0, GRID_W - C_WIN_COLS)
    col_ok = jnp.asarray((kc >= cs) & (kc < cs + C_WIN_COLS))
    col_tiles = jnp.where(col_ok, col_tiles, NEG_INF)
    masked = jnp.full((heads, GRID_W, GRID_W), NEG_INF, F32)
    tiles = []
    for it in (0, min(1, n_row_tiles - 1), n_row_tiles - 1):
        for off in (-1, 0, 1):
            key_rows = []
            for a in range(C_ROWS_PER_TILE):
                blocks = []
                for b in range(C_ROWS_PER_TILE):
                    r = it * C_ROWS_PER_TILE + b
                    key_r = (it + off) * C_ROWS_PER_TILE + a
                    rs = min(max(r - kr // 2, 0), rows - kr)
                    ok = 0 <= key_r < rows and rs <= key_r < rs + kr
                    blocks.append(col_tiles[:, key_r - r + C_WIN_ROWS - 1] if ok else masked)
                key_rows.append(jnp.concatenate(blocks, axis=2))
            tiles.append(jnp.concatenate(key_rows, axis=1))
    return _group_heads(jnp.stack(tiles))


def _c_tile_index(i, off, nt):
    variant = jnp.where(i == 0, 0, jnp.where(i == nt - 1, 2, 1))
    return variant * 3 + off + 1


def _rope_tables(t_len):
    t = jnp.arange(t_len, dtype=jnp.int32)
    n_freq = HEAD_DIM // 4
    inv_freq = ROPE_THETA ** (-jnp.arange(n_freq, dtype=F32) / n_freq)
    ang = jnp.concatenate([(t // GRID_W).astype(F32)[:, None] * inv_freq[None, :],
                           (t % GRID_W).astype(F32)[:, None] * inv_freq[None, :]], axis=-1)
    cos, sin = jnp.cos(ang), jnp.sin(ang)
    reps = LANES // HEAD_DIM
    return (jnp.tile(jnp.concatenate([cos, cos], axis=-1), (1, reps)),
            jnp.tile(jnp.concatenate([-sin, sin], axis=-1), (1, reps)))


def _deinterleave(w):
    return jnp.swapaxes(w.reshape(w.shape[:-1] + (HEAD_DIM // 2, 2)), -1, -2).reshape(w.shape)


def _swap_halves(w):
    return jnp.concatenate([w[..., HEAD_DIM // 2:], w[..., :HEAD_DIM // 2]], axis=-1)


def _prep_ffn(w_gate, w_up, w_down):
    d, dff = w_gate.shape
    nt = dff // FFN_TF
    wgu = jnp.stack([w_gate.reshape(d, nt, FFN_TF), w_up.reshape(d, nt, FFN_TF)], axis=2)
    return wgu.reshape(d, 2 * dff).astype(BF16), w_down.astype(BF16)


def _prep_mixer_a(w_in, g_q, g_k):
    d = w_in.shape[0]
    qw = A_HEADS * HEAD_DIM
    kw = A_KV_HEADS * HEAD_DIM
    rep = A_HEADS // A_KV_HEADS
    wq = _deinterleave(w_in[:, :qw].reshape(d, A_HEADS, HEAD_DIM))
    wk = _deinterleave(w_in[:, qw:qw + kw].reshape(d, A_KV_HEADS, 1, HEAD_DIM))
    wk = jnp.broadcast_to(wk, (d, A_KV_HEADS, rep, HEAD_DIM)).reshape(d, A_HEADS, HEAD_DIM)
    wqk = jnp.concatenate([wq, wk], axis=1)
    wv = w_in[:, qw + kw:qw + 2 * kw]
    w = jnp.concatenate([wqk.reshape(d, -1), wv], axis=1).astype(BF16)
    wsw = _swap_halves(wqk).reshape(d, -1).astype(BF16)
    gq, gk = _deinterleave(g_q) * (SCALE * LOG2E), _deinterleave(g_k)
    g = jnp.concatenate([jnp.tile(gq, A_HEADS), jnp.tile(gk, A_HEADS)])
    gsw = jnp.concatenate([jnp.tile(_swap_halves(gq), A_HEADS), jnp.tile(_swap_halves(gk), A_HEADS)])
    return w, wsw, g[None, :].astype(F32), gsw[None, :].astype(F32)


def _prep_mixers_bcd(w_in):
    a_cols = (A_HEADS + 2 * A_KV_HEADS) * HEAD_DIM
    w = w_in[:, a_cols:]
    is_q = (np.arange(w.shape[1]) // GROUP_WIDTH) % 3 == 0
    return (w * jnp.asarray(np.where(is_q, SCALE * LOG2E, 1.0), F32)[None, :]).astype(BF16)


def _head_sum_matrix():
    idx = np.arange(MXU_DIM) // HEAD_DIM
    return jnp.asarray(idx[:, None] == idx[None, :], dtype=BF16)


def _values_t(v, batch, t_len, dv):
    heads = v.shape[1] // dv
    v4 = v.reshape(batch, t_len, heads, dv)
    v4 = jnp.concatenate([v4, jnp.ones((batch, t_len, heads, ONES_ROWS), v.dtype)], axis=-1)
    return jnp.transpose(v4, (0, 2, 3, 1)).reshape(batch, heads * (dv + ONES_ROWS), t_len)


def _key_tiled(vt, tile):
    batch, r, t_len = vt.shape
    return jnp.transpose(vt.reshape(batch, GROUPS, r // GROUPS, t_len // tile, tile), (0, 1, 3, 2, 4))


def _trunk(x3, layers, bias_b, far_b, bias_d, d_reach, final_norm):
    batch, t_len, d = x3.shape
    x = x3.reshape(batch * t_len, d)
    cos, sin = _rope_tables(t_len)
    e = _head_sum_matrix()
    depth = len(layers)
    for l, p in enumerate(layers):
        x = _ffn(x, p["ffn1_norm"], *p["ffn1"], final_norm, final=False)
        qa, ka, va = _aproj(x, p["mix_norm"], p["wa"], p["wa_sw"], e, p["ga"], p["ga_sw"], cos, sin, t_len)
        qkv = _proj(x, p["mix_norm"], p["w_bcd"])

        def values(m, dv, tile):
            lo = (3 * m + 2) * GROUP_WIDTH
            return _key_tiled(_values_t(qkv[:, lo:lo + GROUP_WIDTH], batch, t_len, dv), tile)

        oa = _attention(qa, ka, _key_tiled(_values_t(va, batch, t_len, HEAD_DIM), ATT_T),
                        t_len=t_len, tile=ATT_T, qcol=0, kcol=0)
        ob = _attention(qkv, qkv, values(0, B_V_DIM, ATT_T), t_len=t_len, tile=ATT_T, qcol=0, kcol=GROUPS,
                        bias=bias_b, reach=B_NEAR, bias_index=lambda i, off, nt: off + B_NEAR, far=far_b,
                        diff=(p["b_lambda"], p["b_subln"], 0.8 - 0.6 * math.exp(-0.3 * l)))
        oc = _attention(qkv, qkv, values(1, HEAD_DIM, C_T), t_len=t_len, tile=C_T,
                        qcol=3 * GROUPS, kcol=4 * GROUPS, bias=_bias_c(p["c_rpb"], t_len // GRID_W),
                        reach=1, bias_index=_c_tile_index)
        od = _attention(qkv, qkv, values(2, HEAD_DIM, ATT_T), t_len=t_len, tile=ATT_T,
                        qcol=6 * GROUPS, kcol=7 * GROUPS, bias=bias_d, reach=d_reach,
                        bias_index=lambda i, off, nt: off + d_reach)
        x = _outproj(x, oa, ob, oc, od, p["w_out"])
        x = _ffn(x, p["ffn2_norm"], *p["ffn2"], final_norm, final=(l == depth - 1))
    return x.reshape(batch, t_len, d)


def kernel(x_prompt, x_sample, ffn1_norm, ffn1_w_gate, ffn1_w_up, ffn1_w_down, mix_norm, w_in, a_q_norm, a_k_norm, b_lambda, b_subln, c_rpb, w_out, ffn2_norm, ffn2_w_gate, ffn2_w_up, ffn2_w_down, t5_table, final_norm):
    depth = w_in.shape[0]
    layers = []
    for l in range(depth):
        wa, wa_sw, ga, ga_sw = _prep_mixer_a(w_in[l], a_q_norm[l], a_k_norm[l])
        layers.append(dict(
            ffn1_norm=ffn1_norm[l][None, :], ffn2_norm=ffn2_norm[l][None, :], mix_norm=mix_norm[l][None, :],
            ffn1=_prep_ffn(ffn1_w_gate[l], ffn1_w_up[l], ffn1_w_down[l]),
            ffn2=_prep_ffn(ffn2_w_gate[l], ffn2_w_up[l], ffn2_w_down[l]),
            wa=wa, wa_sw=wa_sw, ga=ga, ga_sw=ga_sw,
            w_bcd=_prep_mixers_bcd(w_in[l]),
            b_lambda=b_lambda[l].astype(F32), b_subln=b_subln[l][None, :].astype(F32),
            c_rpb=c_rpb[l], w_out=w_out[l].astype(BF16)))
    bias_b, far_b = _bias_b(t5_table[:, :B_HEADS])
    bias_d, d_reach = _bias_d(t5_table[:, B_HEADS:])
    fn = final_norm[None, :]
    return tuple(_trunk(x3, layers, bias_b, far_b, bias_d, d_reach, fn) for x3 in (x_prompt, x_sample))
```

```python
import functools
import math

import jax
import jax.numpy as jnp
import numpy as np
from jax import lax
from jax.experimental import pallas as pl
from jax.experimental.pallas import tpu as pltpu

F32 = jnp.float32
BF16 = jnp.bfloat16

HEAD_DIM = 64
GRID_W = 64
EPS = 1e-6
NEG_INF = -1e30
SCALE = HEAD_DIM ** -0.5
LOG2E = math.log2(math.e)
ROPE_THETA = 10000.0
A_HEADS, A_KV_HEADS = 8, 2
B_HEADS, B_V_DIM = 4, 128
C_HEADS, C_WIN_ROWS, C_WIN_COLS = 8, 8, 16
D_HEADS = 8
D_BRANCHES = ((128, 1), (512, 4), (2048, 16))
T5_BUCKETS, T5_MAX_DIST = 32, 1024
GROUP_WIDTH = 512

LANES = 128
BF16_SUBLANES = 16
MXU_DIM = 256
HEADS_PER_GROUP = MXU_DIM // HEAD_DIM
GROUPS = GROUP_WIDTH // MXU_DIM
VMEM_LIMIT = 56 * 1024 * 1024
FFN_TM, FFN_TF = 1024, 256
PROJ_TM, PROJ_TN = 1024, 1536
APROJ_TM = 512
OUT_TM = 512
ATT_T = 512
C_ROWS_PER_TILE = 4
C_T = C_ROWS_PER_TILE * GRID_W
ONES_ROWS = BF16_SUBLANES
B_NEAR = 2


def _cparams(sem, flags=None):
    return pltpu.CompilerParams(dimension_semantics=sem, vmem_limit_bytes=VMEM_LIMIT, flags=flags)


def _rms(x, g):
    return x * lax.rsqrt(jnp.mean(x * x, axis=-1, keepdims=True) + EPS) * g


def _ffn_kernel(x_ref, g_ref, wgu_ref, wd_ref, fg_ref, o_ref, xn_ref, *, final):
    j = pl.program_id(1)

    @pl.when(j == 0)
    def _():
        x = x_ref[...]
        xn_ref[...] = _rms(x, g_ref[...]).astype(BF16)
        o_ref[...] = x

    h = jnp.dot(xn_ref[...], wgu_ref[...], preferred_element_type=F32)
    tf = h.shape[1] // 2
    hg, hu = h[:, :tf], h[:, tf:]
    a = (hg * jax.nn.sigmoid(hg)) * hu
    o_ref[...] += jnp.dot((0.5 * a).astype(BF16), wd_ref[...], preferred_element_type=F32)

    if final:
        @pl.when(j == pl.num_programs(1) - 1)
        def _():
            o_ref[...] = _rms(o_ref[...], fg_ref[...])


def _ffn(x, g, wgu, wd, fg, *, final):
    n, d = x.shape
    dff = wd.shape[0]
    tm, tf = min(FFN_TM, n), FFN_TF
    return pl.pallas_call(
        functools.partial(_ffn_kernel, final=final),
        out_shape=jax.ShapeDtypeStruct((n, d), F32),
        grid=(n // tm, dff // tf),
        in_specs=[
            pl.BlockSpec((tm, d), lambda i, j: (i, 0)),
            pl.BlockSpec((1, d), lambda i, j: (0, 0)),
            pl.BlockSpec((d, 2 * tf), lambda i, j: (0, j)),
            pl.BlockSpec((tf, d), lambda i, j: (j, 0)),
            pl.BlockSpec((1, d), lambda i, j: (0, 0)),
        ],
        out_specs=pl.BlockSpec((tm, d), lambda i, j: (i, 0)),
        scratch_shapes=[pltpu.VMEM((tm, d), BF16)],
        compiler_params=_cparams(("parallel", "arbitrary")),
        name="ffn",
    )(x, g, wgu, wd, fg)


def _store_values_t(vt_ref, v, dv):
    groups, n_tiles, rows, tile = vt_ref.shape
    per_head = dv + ONES_ROWS
    ones = jnp.ones((ONES_ROWS, tile), BF16)
    heads_per_group = rows // per_head
    for t in range(n_tiles):
        vt = v[t * tile:(t + 1) * tile, :].T
        for g in range(groups):
            for h in range(heads_per_group):
                src = (g * heads_per_group + h) * dv
                vt_ref[g, t, h * per_head:h * per_head + dv, :] = vt[src:src + dv, :].astype(BF16)
                vt_ref[g, t, h * per_head + dv:(h + 1) * per_head, :] = ones


def _proj_kernel(x_ref, g_ref, w_ref, o_ref, vtb_ref, vtc_ref, vtd_ref, xn_ref):
    j = pl.program_id(1)

    @pl.when(j == 0)
    def _():
        xn_ref[...] = _rms(x_ref[...], g_ref[...]).astype(BF16)

    y = jnp.dot(xn_ref[...], w_ref[...], preferred_element_type=F32)
    nqk = o_ref.shape[1]
    o_ref[...] = y[:, :nqk].astype(BF16)
    for m, (vt_ref, dv) in enumerate(((vtb_ref, B_V_DIM), (vtc_ref, HEAD_DIM), (vtd_ref, HEAD_DIM))):
        @pl.when(j == m)
        def _():
            _store_values_t(vt_ref, y[:, nqk:], dv)


def _values_shape(batch, t_len, heads, dv, tile):
    return (batch, GROUPS, t_len // tile, heads // GROUPS * (dv + ONES_ROWS), tile)


def _proj(x, g, w, t_len):
    n, d = x.shape
    batch = n // t_len
    tm, tn = math.gcd(PROJ_TM, t_len), PROJ_TN
    npos = t_len // tm
    nqk = 2 * GROUP_WIDTH
    vt_shapes = [_values_shape(batch, t_len, B_HEADS, B_V_DIM, ATT_T),
                 _values_shape(batch, t_len, C_HEADS, HEAD_DIM, C_T),
                 _values_shape(batch, t_len, D_HEADS, HEAD_DIM, ATT_T)]
    vt_specs = [pl.BlockSpec((None, GROUPS, tm // s[4], s[3], s[4]), lambda i, j: (i // npos, 0, i % npos, 0, 0))
                for s in vt_shapes]
    return pl.pallas_call(
        _proj_kernel,
        out_shape=[jax.ShapeDtypeStruct((n, 3 * nqk), BF16)] + [jax.ShapeDtypeStruct(s, BF16) for s in vt_shapes],
        grid=(n // tm, w.shape[1] // tn),
        in_specs=[
            pl.BlockSpec((tm, d), lambda i, j: (i, 0)),
            pl.BlockSpec((1, d), lambda i, j: (0, 0)),
            pl.BlockSpec((d, tn), lambda i, j: (0, j)),
        ],
        out_specs=[pl.BlockSpec((tm, nqk), lambda i, j: (i, j))] + vt_specs,
        scratch_shapes=[pltpu.VMEM((tm, d), BF16)],
        compiler_params=_cparams(("parallel", "arbitrary")),
        name="proj_bcd",
    )(x, g, w)


def _aproj_kernel(x_ref, g_ref, w_ref, wsw_ref, e_ref, gc_ref, gs_ref, cos_ref, sin_ref,
                  q_ref, k_ref, vt_ref):
    xn = _rms(x_ref[...], g_ref[...]).astype(BF16)
    y = jnp.dot(xn, w_ref[...], preferred_element_type=F32)
    ysw = jnp.dot(xn, wsw_ref[...], preferred_element_type=F32)
    nqk = ysw.shape[1]
    _store_values_t(vt_ref, y[:, nqk:], HEAD_DIM)
    e = e_ref[...]
    cos = jnp.concatenate([cos_ref[...]] * (MXU_DIM // LANES), axis=1)
    sin = jnp.concatenate([sin_ref[...]] * (MXU_DIM // LANES), axis=1)
    half = nqk // 2
    for lo in range(0, nqk, MXU_DIM):
        yc = y[:, lo:lo + MXU_DIM]
        sq = yc * yc
        hi = sq.astype(BF16)
        rest = (sq - hi.astype(F32)).astype(BF16)
        ssum = (jnp.dot(hi, e, preferred_element_type=F32)
                + jnp.dot(rest, e, preferred_element_type=F32))
        r = lax.rsqrt(ssum * (1.0 / HEAD_DIM) + EPS)
        out = (yc * gc_ref[:, lo:lo + MXU_DIM] * cos
               + ysw[:, lo:lo + MXU_DIM] * gs_ref[:, lo:lo + MXU_DIM] * sin) * r
        dst, off = (q_ref, lo) if lo < half else (k_ref, lo - half)
        dst[:, off:off + MXU_DIM] = out.astype(BF16)


def _aproj(x, g, w, wsw, e, gc, gs, cos, sin, t_len):
    n, d = x.shape
    tm = math.gcd(APROJ_TM, t_len)
    npos = t_len // tm
    nqk = wsw.shape[1]
    const = lambda i: (0, 0)
    vt_shape = _values_shape(n // t_len, t_len, A_KV_HEADS, HEAD_DIM, ATT_T)
    return pl.pallas_call(
        _aproj_kernel,
        out_shape=(jax.ShapeDtypeStruct((n, nqk // 2), BF16),
                   jax.ShapeDtypeStruct((n, nqk // 2), BF16),
                   jax.ShapeDtypeStruct(vt_shape, BF16)),
        grid=(n // tm,),
        in_specs=[
            pl.BlockSpec((tm, d), lambda i: (i, 0)),
            pl.BlockSpec((1, d), const),
            pl.BlockSpec(w.shape, const),
            pl.BlockSpec(wsw.shape, const),
            pl.BlockSpec(e.shape, const),
            pl.BlockSpec(gc.shape, const),
            pl.BlockSpec(gs.shape, const),
            pl.BlockSpec((tm, LANES), lambda i: (i % npos, 0)),
            pl.BlockSpec((tm, LANES), lambda i: (i % npos, 0)),
        ],
        out_specs=(pl.BlockSpec((tm, nqk // 2), lambda i: (i, 0)),
                   pl.BlockSpec((tm, nqk // 2), lambda i: (i, 0)),
                   pl.BlockSpec((None, GROUPS, tm // ATT_T) + vt_shape[3:],
                                lambda i: (i // npos, 0, i % npos, 0, 0))),
        compiler_params=_cparams(("parallel",)),
        name="proj_a",
    )(x, g, w, wsw, e, gc, gs, cos, sin)


def _outproj_kernel(x_ref, oa_ref, ob_ref, oc_ref, od_ref, w_ref, o_ref):
    acc = x_ref[...]
    for m, ref in enumerate((oa_ref, ob_ref, oc_ref, od_ref)):
        acc = acc + jnp.dot(ref[...], w_ref[m * GROUP_WIDTH:(m + 1) * GROUP_WIDTH, :],
                            preferred_element_type=F32)
    o_ref[...] = acc


def _outproj(x, oa, ob, oc, od, w):
    n, d = x.shape
    tm = min(OUT_TM, n)
    mix = pl.BlockSpec((tm, GROUP_WIDTH), lambda i: (i, 0))
    return pl.pallas_call(
        _outproj_kernel,
        out_shape=jax.ShapeDtypeStruct((n, d), F32),
        grid=(n // tm,),
        in_specs=[pl.BlockSpec((tm, d), lambda i: (i, 0)), mix, mix, mix, mix,
                  pl.BlockSpec(w.shape, lambda i: (0, 0))],
        out_specs=pl.BlockSpec((tm, d), lambda i: (i, 0)),
        compiler_params=_cparams(("parallel",)),
        name="out_proj",
    )(x, oa, ob, oc, od, w)


def _attn_init(q_ref, qm_ref):
    qt = q_ref[...].astype(F32).T
    unit = lax.broadcasted_iota(jnp.int32, qt.shape, 0) // HEAD_DIM
    for u in range(qm_ref.shape[0]):
        qm_ref[u] = jnp.where(unit == u, qt, 0.0).astype(BF16)


def _scores(k, qm_ref, u):
    return jnp.dot(k, qm_ref[u], preferred_element_type=F32)


def _attn_units(k, qm_ref, update, st0_ref=None, k_next=None, from_st0=True):
    n_units = qm_ref.shape[0]
    st_next = st0_ref[...] if (st0_ref is not None and from_st0) else _scores(k, qm_ref, 0)
    for u in range(n_units):
        st = st_next
        if u + 1 < n_units:
            st_next = _scores(k, qm_ref, u + 1)
        elif st0_ref is not None:
            st0_ref[...] = _scores(k_next, qm_ref, 0)
        update(u, st)


def _tile_first(st, vt, u, m_ref, acc_ref, bias, shift):
    if bias is not None:
        st = st + bias
    m_tile = jnp.max(st, axis=0, keepdims=True)
    m_ref[u] = m_tile if shift is None else m_tile + shift
    acc_ref[u] = jnp.dot(vt, jnp.exp2(st - m_tile).astype(BF16), preferred_element_type=F32)


def _tile_fixed(st, vt, u, m_ref, acc_ref, bias, shift):
    if bias is not None:
        st = st + bias
    sub = m_ref[u] if shift is None else m_ref[u] - shift
    acc_ref[u] += jnp.dot(vt, jnp.exp2(st - sub).astype(BF16), preferred_element_type=F32)


def _tile_online(st, vt, u, m_ref, acc_ref, bias, shift):
    if bias is not None:
        st = st + bias
    smax = jnp.max(st, axis=0, keepdims=True)
    if shift is not None:
        smax = smax + shift
    m_old = m_ref[u]
    m_new = jnp.maximum(m_old, smax)
    sub = m_new if shift is None else m_new - shift
    p = jnp.exp2(st - sub).astype(BF16)
    acc_ref[u] = jnp.exp2(m_old - m_new) * acc_ref[u] + jnp.dot(vt, p, preferred_element_type=F32)
    m_ref[u] = m_new


def _attn_finish(o_ref, acc_ref, dv, diff_refs, lam_init):
    n_units = acc_ref.shape[0]
    outs = [acc_ref[u, :dv, :] / acc_ref[u, dv:dv + 1, :] for u in range(n_units)]
    if diff_refs:
        lam_ref, subg_ref = diff_refs
        lp = lam_ref[...]
        lam = (jnp.exp(jnp.sum(lp[0:1] * lp[1:2], keepdims=True))
               - jnp.exp(jnp.sum(lp[2:3] * lp[3:4], keepdims=True)) + lam_init)
        outs = [outs[2 * h] - lam * outs[2 * h + 1] for h in range(n_units // 2)]
    o = jnp.concatenate(outs, axis=0).T
    if diff_refs:
        o = jnp.concatenate([_rms(o[:, lo:lo + dv], subg_ref[...]) * (1.0 - lam_init)
                             for lo in range(0, o.shape[1], dv)], axis=1)
    o_ref[...] = o.astype(BF16)


def _value_rows(u, n_units, rows_in_block, dv):
    per_head = dv + ONES_ROWS
    heads_in_block = rows_in_block // per_head
    lo = (u * heads_in_block // n_units) * per_head
    return lo, lo + per_head


PLAIN, FAR_LEFT, FAR_RIGHT, NEAR = "plain", 0, 1, "near"


def _attn_kernel(*refs, dv, reach, far, bias_index, lam_init):
    it = iter(refs)
    q_ref, k_ref, vt_ref = next(it), next(it), next(it)
    bias_ref = next(it) if reach is not None else None
    far_ref = next(it) if far else None
    diff_refs = (next(it), next(it)) if lam_init is not None else None
    o_ref, qm_ref, m_ref, acc_ref, st0_ref = next(it), next(it), next(it), next(it), next(it)

    i = pl.program_id(2)
    n_units = qm_ref.shape[0]
    nk, rows, tile = vt_ref.shape
    bias_heads = bias_ref.shape[1] if bias_ref is not None else 0
    _attn_init(q_ref, qm_ref)
    lo, hi = (0, nk) if reach is None else (jnp.maximum(i - reach, 0), jnp.minimum(i + reach + 1, nk))
    p_first, p_end = (1, nk) if (reach is None or far) else (lo + 1, hi)

    def tile_at(p):
        p = jnp.minimum(p, p_end - 1)
        return jnp.clip(p - (p <= i).astype(jnp.int32), 0, nk - 1)

    def keys(j):
        return k_ref[pl.ds(pl.multiple_of(j * tile, tile), tile), :]

    def updater(tile_fn, j, side):
        def update(u, st):
            r0, r1 = _value_rows(u, n_units, rows, dv)
            h = u * bias_heads // n_units
            bias = bias_ref[bias_index(i, j - i, nk), h] if side == NEAR else None
            shift = far_ref[side * bias_heads + h:side * bias_heads + h + 1, :] if side in (FAR_LEFT, FAR_RIGHT) else None
            tile_fn(st, vt_ref[j, r0:r1, :], u, m_ref, acc_ref, bias, shift)
        return update

    def visit_rest(tile_fn, pipelined):
        def span(p_lo, p_hi, side):
            def body(p, carry):
                j = tile_at(p)
                if pipelined:
                    _attn_units(keys(j), qm_ref, updater(tile_fn, j, side), st0_ref, keys(tile_at(p + 1)))
                else:
                    _attn_units(keys(j), qm_ref, updater(tile_fn, j, side))
                return carry
            lax.fori_loop(p_lo, p_hi, body, 0)
        if reach is None:
            span(1, nk, PLAIN)
        elif far:
            span(1, lo + 1, FAR_LEFT)
            span(lo + 1, hi, NEAR)
            span(hi, nk, FAR_RIGHT)
        else:
            span(lo + 1, hi, NEAR)

    diag = PLAIN if reach is None else NEAR
    _attn_units(keys(i), qm_ref, updater(_tile_first, i, diag), st0_ref, keys(tile_at(p_first)), from_st0=False)
    visit_rest(_tile_fixed, pipelined=True)

    finite = jnp.min(jnp.where(jnp.isfinite(acc_ref[...]), 1.0, 0.0))

    @pl.when(finite < 0.5)
    def _():
        m_ref[...] = jnp.full(m_ref.shape, -jnp.inf, F32)
        acc_ref[...] = jnp.zeros(acc_ref.shape, F32)
        _attn_units(keys(i), qm_ref, updater(_tile_online, i, diag))
        visit_rest(_tile_online, pipelined=False)

    _attn_finish(o_ref, acc_ref, dv, diff_refs, lam_init)


def _attention(q, k, vt, *, t_len, tile, qcol, kcol, bias=None, reach=None, bias_index=None, far=None, diff=None,
               flags=None):
    n = q.shape[0]
    batch, nt = n // t_len, t_len // tile
    dv = B_V_DIM if diff else HEAD_DIM
    rows = vt.shape[3]
    once = pl.Buffered(1)
    in_specs = [
        pl.BlockSpec((tile, MXU_DIM), lambda b, g, i: (b * nt + i, qcol + g)),
        pl.BlockSpec((t_len, MXU_DIM), lambda b, g, i: (b, kcol + g), pipeline_mode=once),
        pl.BlockSpec((None, None, nt, rows, tile), lambda b, g, i: (b, g, 0, 0, 0), pipeline_mode=once),
    ]
    args = [q, k, vt]
    if bias is not None:
        in_specs.append(pl.BlockSpec((bias.shape[0], None, bias.shape[2], tile, tile),
                                     lambda b, g, i: (0, g, 0, 0, 0), pipeline_mode=once))
        args.append(bias)
    if far is not None:
        in_specs.append(pl.BlockSpec((None,) + far.shape[1:], lambda b, g, i: (g, 0, 0)))
        args.append(far)
    lam_init = None
    if diff:
        lam_p, subg, lam_init = diff
        in_specs += [pl.BlockSpec(lam_p.shape, lambda b, g, i: (0, 0)),
                     pl.BlockSpec(subg.shape, lambda b, g, i: (0, 0))]
        args += [lam_p, subg]
    return pl.pallas_call(
        functools.partial(_attn_kernel, dv=dv, reach=reach, far=far is not None, bias_index=bias_index,
                          lam_init=lam_init),
        out_shape=jax.ShapeDtypeStruct((n, GROUP_WIDTH), BF16),
        grid=(batch, GROUPS, nt),
        in_specs=in_specs,
        out_specs=pl.BlockSpec((tile, MXU_DIM), lambda b, g, i: (b * nt + i, g)),
        scratch_shapes=[pltpu.VMEM((HEADS_PER_GROUP, MXU_DIM, tile), BF16),
                        pltpu.VMEM((HEADS_PER_GROUP, 1, tile), F32),
                        pltpu.VMEM((HEADS_PER_GROUP, dv + ONES_ROWS, tile), F32),
                        pltpu.VMEM((tile, tile), F32)],
        compiler_params=_cparams(("parallel", "parallel", "arbitrary"), flags),
        name="attention",
    )(*args)


def _t5_bucket(rel):
    nb = T5_BUCKETS // 2
    max_exact = nb // 2
    side = (rel > 0).astype(jnp.int32) * nb
    n = jnp.abs(rel)
    large = max_exact + (jnp.log(jnp.maximum(n, 1).astype(F32) / max_exact)
                         / math.log(T5_MAX_DIST / max_exact) * (nb - max_exact)).astype(jnp.int32)
    large = jnp.minimum(large, nb - 1)
    return side + jnp.where(n < max_exact, n, large)


def _skew(v, n):
    lead = v.shape[:-1]
    r = jnp.concatenate([v[..., ::-1], jnp.zeros(lead + (1,), v.dtype)], axis=-1)
    x = jnp.broadcast_to(r[..., None, :], lead + (n, r.shape[-1]))
    k = np.arange(n)[:, None]
    for b in range(n.bit_length() - 1):
        x = jnp.where(jnp.asarray((k >> b) & 1 == 1), jnp.roll(x, 1 << b, axis=-1), x)
    return x


def _toeplitz(v, n):
    return _skew(v, n)[..., n - 1:2 * n - 1]


def _rel_tiles(table, reach, n):
    x = _skew(table, n)
    return jnp.stack([x[..., (2 * reach + 1 - o) * n - 1:(2 * reach + 2 - o) * n - 1]
                      for o in range(2 * reach + 1)])


def _group_heads(b):
    t, h = b.shape[:2]
    return b.reshape(t, GROUPS, h // GROUPS, *b.shape[2:])


def _t5_saturation():
    nb = T5_BUCKETS // 2
    max_exact = nb // 2
    return math.ceil(max_exact * (T5_MAX_DIST / max_exact) ** ((nb - 1 - max_exact) / (nb - max_exact))) + 1


def _bias_b(t5_b):
    n, reach = ATT_T, B_NEAR
    assert reach * n + 1 >= _t5_saturation()
    rel = jnp.arange(-(reach * n + n - 1), reach * n + n, dtype=jnp.int32)
    table = (t5_b[_t5_bucket(rel)].astype(F32) * LOG2E).T
    far = (t5_b[_t5_bucket(jnp.array([-T5_MAX_DIST * 2, T5_MAX_DIST * 2], jnp.int32))].astype(F32) * LOG2E)
    heads = t5_b.shape[1]
    far = far.reshape(2, GROUPS, heads // GROUPS).transpose(1, 0, 2).reshape(GROUPS, 2 * heads // GROUPS, 1)
    return _group_heads(_rel_tiles(table, reach, n)), jnp.broadcast_to(far, far.shape[:2] + (n,))


def _bias_d(t5_d):
    n = ATT_T
    reach = max(w // 2 for w, _ in D_BRANCHES) // n
    rel = jnp.arange(-(reach * n + n - 1), reach * n + n, dtype=jnp.int32)
    mult = sum(((jnp.abs(rel) <= w // 2) & (rel % d == 0)).astype(F32) for w, d in D_BRANCHES)
    b = t5_d[_t5_bucket(rel)].astype(F32).T
    table = jnp.where(mult[None] > 0, (b + jnp.log(jnp.maximum(mult, 1.0))[None]) * LOG2E, NEG_INF)
    return _group_heads(_rel_tiles(table, reach, n)), reach


def _bias_c(rpb, rows):
    heads = rpb.shape[0]
    kr = min(C_WIN_ROWS, rows)
    n_row_tiles = rows // C_ROWS_PER_TILE
    pad = GRID_W - C_WIN_COLS
    col_tiles = _toeplitz(jnp.pad(rpb.astype(F32) * LOG2E, ((0, 0), (0, 0), (pad, pad))), GRID_W)
    c = np.arange(GRID_W)[None, :]
    kc = np.arange(GRID_W)[:, None]
    cs = np.clip(c - C_WIN_COLS // 2, 0, GRID_W - C_WIN_COLS)
    col_ok = jnp.asarray((kc >= cs) & (kc < cs + C_WIN_COLS))
    col_tiles = jnp.where(col_ok, col_tiles, NEG_INF)
    masked = jnp.full((heads, GRID_W, GRID_W), NEG_INF, F32)
    tiles = []
    for it in (0, min(1, n_row_tiles - 1), n_row_tiles - 1):
        for off in (-1, 0, 1):
            key_rows = []
            for a in range(C_ROWS_PER_TILE):
                blocks = []
                for b in range(C_ROWS_PER_TILE):
                    r = it * C_ROWS_PER_TILE + b
                    key_r = (it + off) * C_ROWS_PER_TILE + a
                    rs = min(max(r - kr // 2, 0), rows - kr)
                    ok = 0 <= key_r < rows and rs <= key_r < rs + kr
                    blocks.append(col_tiles[:, key_r - r + C_WIN_ROWS - 1] if ok else masked)
                key_rows.append(jnp.concatenate(blocks, axis=2))
            tiles.append(jnp.concatenate(key_rows, axis=1))
    return _group_heads(jnp.stack(tiles))


def _c_tile_index(i, off, nt):
    variant = jnp.where(i == 0, 0, jnp.where(i == nt - 1, 2, 1))
    return variant * 3 + off + 1


def _rope_tables(t_len):
    t = jnp.arange(t_len, dtype=jnp.int32)
    n_freq = HEAD_DIM // 4
    inv_freq = ROPE_THETA ** (-jnp.arange(n_freq, dtype=F32) / n_freq)
    ang = jnp.concatenate([(t // GRID_W).astype(F32)[:, None] * inv_freq[None, :],
                           (t % GRID_W).astype(F32)[:, None] * inv_freq[None, :]], axis=-1)
    cos, sin = jnp.cos(ang), jnp.sin(ang)
    reps = LANES // HEAD_DIM
    return (jnp.tile(jnp.concatenate([cos, cos], axis=-1), (1, reps)),
            jnp.tile(jnp.concatenate([-sin, sin], axis=-1), (1, reps)))


def _deinterleave(w):
    return jnp.swapaxes(w.reshape(w.shape[:-1] + (HEAD_DIM // 2, 2)), -1, -2).reshape(w.shape)


def _swap_halves(w):
    return jnp.concatenate([w[..., HEAD_DIM // 2:], w[..., :HEAD_DIM // 2]], axis=-1)


def _prep_ffn(w_gate, w_up, w_down):
    d, dff = w_gate.shape
    nt = dff // FFN_TF
    wgu = jnp.stack([w_gate.reshape(d, nt, FFN_TF), w_up.reshape(d, nt, FFN_TF)], axis=2)
    return wgu.reshape(d, 2 * dff).astype(BF16), w_down.astype(BF16)


def _prep_mixer_a(w_in, g_q, g_k):
    d = w_in.shape[0]
    qw = A_HEADS * HEAD_DIM
    kw = A_KV_HEADS * HEAD_DIM
    rep = A_HEADS // A_KV_HEADS
    wq = _deinterleave(w_in[:, :qw].reshape(d, A_HEADS, HEAD_DIM))
    wk = _deinterleave(w_in[:, qw:qw + kw].reshape(d, A_KV_HEADS, 1, HEAD_DIM))
    wk = jnp.broadcast_to(wk, (d, A_KV_HEADS, rep, HEAD_DIM)).reshape(d, A_HEADS, HEAD_DIM)
    wqk = jnp.concatenate([wq, wk], axis=1)
    wv = w_in[:, qw + kw:qw + 2 * kw]
    w = jnp.concatenate([wqk.reshape(d, -1), wv], axis=1).astype(BF16)
    wsw = _swap_halves(wqk).reshape(d, -1).astype(BF16)
    gq, gk = _deinterleave(g_q) * (SCALE * LOG2E), _deinterleave(g_k)
    g = jnp.concatenate([jnp.tile(gq, A_HEADS), jnp.tile(gk, A_HEADS)])
    gsw = jnp.concatenate([jnp.tile(_swap_halves(gq), A_HEADS), jnp.tile(_swap_halves(gk), A_HEADS)])
    return w, wsw, g[None, :].astype(F32), gsw[None, :].astype(F32)


def _prep_mixers_bcd(w_in):
    a_cols = (A_HEADS + 2 * A_KV_HEADS) * HEAD_DIM
    w = w_in[:, a_cols:]
    is_q = (np.arange(w.shape[1]) // GROUP_WIDTH) % 3 == 0
    return (w * jnp.asarray(np.where(is_q, SCALE * LOG2E, 1.0), F32)[None, :]).astype(BF16)


def _head_sum_matrix():
    idx = np.arange(MXU_DIM) // HEAD_DIM
    return jnp.asarray(idx[:, None] == idx[None, :], dtype=BF16)


def _trunk(x3, layers, bias_b, far_b, bias_d, d_reach, final_norm):
    batch, t_len, d = x3.shape
    x = x3.reshape(batch * t_len, d)
    cos, sin = _rope_tables(t_len)
    e = _head_sum_matrix()
    depth = len(layers)
    for l, p in enumerate(layers):
        x = _ffn(x, p["ffn1_norm"], *p["ffn1"], final_norm, final=False)
        qa, ka, vta = _aproj(x, p["mix_norm"], p["wa"], p["wa_sw"], e, p["ga"], p["ga_sw"], cos, sin, t_len)
        qk, vtb, vtc, vtd = _proj(x, p["mix_norm"], p["w_bcd"], t_len)

        oa = _attention(qa, ka, vta, t_len=t_len, tile=ATT_T, qcol=0, kcol=0)
        ob = _attention(qk, qk, vtb, t_len=t_len, tile=ATT_T, qcol=0, kcol=GROUPS,
                        bias=bias_b, reach=B_NEAR, bias_index=lambda i, off, nt: off + B_NEAR, far=far_b,
                        diff=(p["b_lambda"], p["b_subln"], 0.8 - 0.6 * math.exp(-0.3 * l)))
        oc = _attention(qk, qk, vtc, t_len=t_len, tile=C_T, qcol=2 * GROUPS, kcol=3 * GROUPS,
                        bias=_bias_c(p["c_rpb"], t_len // GRID_W), reach=1, bias_index=_c_tile_index)
        od = _attention(qk, qk, vtd, t_len=t_len, tile=ATT_T, qcol=4 * GROUPS, kcol=5 * GROUPS,
                        bias=bias_d, reach=d_reach, bias_index=lambda i, off, nt: off + d_reach)
        x = _outproj(x, oa, ob, oc, od, p["w_out"])
        x = _ffn(x, p["ffn2_norm"], *p["ffn2"], final_norm, final=(l == depth - 1))
    return x.reshape(batch, t_len, d)


def kernel(x_prompt, x_sample, ffn1_norm, ffn1_w_gate, ffn1_w_up, ffn1_w_down, mix_norm, w_in, a_q_norm, a_k_norm, b_lambda, b_subln, c_rpb, w_out, ffn2_norm, ffn2_w_gate, ffn2_w_up, ffn2_w_down, t5_table, final_norm):
    depth = w_in.shape[0]
    layers = []
    for l in range(depth):
        wa, wa_sw, ga, ga_sw = _prep_mixer_a(w_in[l], a_q_norm[l], a_k_norm[l])
        layers.append(dict(
            ffn1_norm=ffn1_norm[l][None, :], ffn2_norm=ffn2_norm[l][None, :], mix_norm=mix_norm[l][None, :],
            ffn1=_prep_ffn(ffn1_w_gate[l], ffn1_w_up[l], ffn1_w_down[l]),
            ffn2=_prep_ffn(ffn2_w_gate[l], ffn2_w_up[l], ffn2_w_down[l]),
            wa=wa, wa_sw=wa_sw, ga=ga, ga_sw=ga_sw,
            w_bcd=_prep_mixers_bcd(w_in[l]),
            b_lambda=b_lambda[l].astype(F32), b_subln=b_subln[l][None, :].astype(F32),
            c_rpb=c_rpb[l], w_out=w_out[l].astype(BF16)))
    bias_b, far_b = _bias_b(t5_table[:, :B_HEADS])
    bias_d, d_reach = _bias_d(t5_table[:, B_HEADS:])
    fn = final_norm[None, :]
    return tuple(_trunk(x3, layers, bias_b, far_b, bias_d, d_reach, fn) for x3 in (x_prompt, x_sample))
```

```python
import functools
import math

import jax
import jax.numpy as jnp
import numpy as np
from jax import lax
from jax.experimental import pallas as pl
from jax.experimental.pallas import tpu as pltpu

F32 = jnp.float32
BF16 = jnp.bfloat16

HEAD_DIM = 64
GRID_W = 64
EPS = 1e-6
NEG_INF = -1e30
SCALE = HEAD_DIM ** -0.5
LOG2E = math.log2(math.e)
ROPE_THETA = 10000.0
A_HEADS, A_KV_HEADS = 8, 2
B_HEADS, B_V_DIM = 4, 128
C_HEADS, C_WIN_ROWS, C_WIN_COLS = 8, 8, 16
D_HEADS = 8
D_BRANCHES = ((128, 1), (512, 4), (2048, 16))
T5_BUCKETS, T5_MAX_DIST = 32, 1024
GROUP_WIDTH = 512

LANES = 128
BF16_SUBLANES = 16
MXU_DIM = 256
HEADS_PER_GROUP = MXU_DIM // HEAD_DIM
GROUPS = GROUP_WIDTH // MXU_DIM
VMEM_LIMIT = 56 * 1024 * 1024
FFN_TM, FFN_TF = 1024, 256
PROJ_TM, PROJ_TN = 1024, 1536
APROJ_TM = 512
OUT_TM = 512
ATT_T = 512
A_KEY_T = 1024
C_ROWS_PER_TILE = 4
C_T = C_ROWS_PER_TILE * GRID_W
ONES_ROWS = BF16_SUBLANES
B_NEAR = 2


def _cparams(sem, flags=None):
    return pltpu.CompilerParams(dimension_semantics=sem, vmem_limit_bytes=VMEM_LIMIT, flags=flags)


def _rms(x, g):
    return x * lax.rsqrt(jnp.mean(x * x, axis=-1, keepdims=True) + EPS) * g


def _ffn_kernel(x_ref, g_ref, wg_ref, wu_ref, wd_ref, fg_ref, o_ref, xn_ref, *, final):
    j = pl.program_id(1)

    @pl.when(j == 0)
    def _():
        x = x_ref[...]
        xn_ref[...] = _rms(x, g_ref[...]).astype(BF16)
        o_ref[...] = x

    h = jnp.dot(xn_ref[...], jnp.concatenate([wg_ref[...], wu_ref[...]], axis=1), preferred_element_type=F32)
    tf = h.shape[1] // 2
    hg, hu = h[:, :tf], h[:, tf:]
    a = (hg * jax.nn.sigmoid(hg)) * hu
    o_ref[...] += jnp.dot((0.5 * a).astype(BF16), wd_ref[...], preferred_element_type=F32)

    if final:
        @pl.when(j == pl.num_programs(1) - 1)
        def _():
            o_ref[...] = _rms(o_ref[...], fg_ref[...])


def _ffn(x, g, wg, wu, wd, fg, *, final):
    n, d = x.shape
    dff = wd.shape[0]
    tm, tf = min(FFN_TM, n), FFN_TF
    return pl.pallas_call(
        functools.partial(_ffn_kernel, final=final),
        out_shape=jax.ShapeDtypeStruct((n, d), F32),
        grid=(n // tm, dff // tf),
        in_specs=[
            pl.BlockSpec((tm, d), lambda i, j: (i, 0)),
            pl.BlockSpec((1, d), lambda i, j: (0, 0)),
            pl.BlockSpec((d, tf), lambda i, j: (0, j)),
            pl.BlockSpec((d, tf), lambda i, j: (0, j)),
            pl.BlockSpec((tf, d), lambda i, j: (j, 0)),
            pl.BlockSpec((1, d), lambda i, j: (0, 0)),
        ],
        out_specs=pl.BlockSpec((tm, d), lambda i, j: (i, 0)),
        scratch_shapes=[pltpu.VMEM((tm, d), BF16)],
        compiler_params=_cparams(("parallel", "arbitrary")),
        name="ffn",
    )(x, g, wg, wu, wd, fg)


def _store_values_t(vt_ref, v, dv):
    groups, n_tiles, rows, tile = vt_ref.shape
    per_head = dv + ONES_ROWS
    ones = jnp.ones((ONES_ROWS, tile), BF16)
    heads_per_group = rows // per_head
    for t in range(n_tiles):
        vt = v[t * tile:(t + 1) * tile, :].T
        for g in range(groups):
            for h in range(heads_per_group):
                src = (g * heads_per_group + h) * dv
                vt_ref[g, t, h * per_head:h * per_head + dv, :] = vt[src:src + dv, :].astype(BF16)
                vt_ref[g, t, h * per_head + dv:(h + 1) * per_head, :] = ones


def _proj_kernel(x_ref, g_ref, w_ref, o_ref, vtb_ref, vtc_ref, vtd_ref, xn_ref):
    j = pl.program_id(1)

    @pl.when(j == 0)
    def _():
        xn_ref[...] = _rms(x_ref[...], g_ref[...]).astype(BF16)

    y = jnp.dot(xn_ref[...], w_ref[...], preferred_element_type=F32)
    nqk = o_ref.shape[1]
    o_ref[...] = y[:, :nqk].astype(BF16)
    for m, (vt_ref, dv) in enumerate(((vtb_ref, B_V_DIM), (vtc_ref, HEAD_DIM), (vtd_ref, HEAD_DIM))):
        @pl.when(j == m)
        def _():
            _store_values_t(vt_ref, y[:, nqk:], dv)


def _values_shape(batch, t_len, heads, dv, tile):
    return (batch, GROUPS, t_len // tile, heads // GROUPS * (dv + ONES_ROWS), tile)


def _proj(x, g, w, t_len):
    n, d = x.shape
    batch = n // t_len
    tm, tn = math.gcd(PROJ_TM, t_len), PROJ_TN
    npos = t_len // tm
    nqk = 2 * GROUP_WIDTH
    vt_shapes = [_values_shape(batch, t_len, B_HEADS, B_V_DIM, ATT_T),
                 _values_shape(batch, t_len, C_HEADS, HEAD_DIM, C_T),
                 _values_shape(batch, t_len, D_HEADS, HEAD_DIM, ATT_T)]
    vt_specs = [pl.BlockSpec((None, GROUPS, tm // s[4], s[3], s[4]), lambda i, j: (i // npos, 0, i % npos, 0, 0))
                for s in vt_shapes]
    return pl.pallas_call(
        _proj_kernel,
        out_shape=[jax.ShapeDtypeStruct((n, 3 * nqk), BF16)] + [jax.ShapeDtypeStruct(s, BF16) for s in vt_shapes],
        grid=(n // tm, w.shape[1] // tn),
        in_specs=[
            pl.BlockSpec((tm, d), lambda i, j: (i, 0)),
            pl.BlockSpec((1, d), lambda i, j: (0, 0)),
            pl.BlockSpec((d, tn), lambda i, j: (0, j)),
        ],
        out_specs=[pl.BlockSpec((tm, nqk), lambda i, j: (i, j))] + vt_specs,
        scratch_shapes=[pltpu.VMEM((tm, d), BF16)],
        compiler_params=_cparams(("parallel", "arbitrary")),
        name="proj_bcd",
    )(x, g, w)


def _aproj_kernel(x_ref, g_ref, w_ref, wsw_ref, e_ref, gc_ref, gs_ref, cos_ref, sin_ref,
                  q_ref, k_ref, vt_ref):
    xn = _rms(x_ref[...], g_ref[...]).astype(BF16)
    y = jnp.dot(xn, w_ref[...], preferred_element_type=F32)
    ysw = jnp.dot(xn, wsw_ref[...], preferred_element_type=F32)
    nqk = ysw.shape[1]
    _store_values_t(vt_ref, y[:, nqk:], HEAD_DIM)
    e = e_ref[...]
    cos = jnp.concatenate([cos_ref[...]] * (MXU_DIM // LANES), axis=1)
    sin = jnp.concatenate([sin_ref[...]] * (MXU_DIM // LANES), axis=1)
    half = nqk // 2
    for lo in range(0, nqk, MXU_DIM):
        yc = y[:, lo:lo + MXU_DIM]
        sq = yc * yc
        hi = sq.astype(BF16)
        rest = (sq - hi.astype(F32)).astype(BF16)
        ssum = (jnp.dot(hi, e, preferred_element_type=F32)
                + jnp.dot(rest, e, preferred_element_type=F32))
        r = lax.rsqrt(ssum * (1.0 / HEAD_DIM) + EPS)
        out = (yc * gc_ref[:, lo:lo + MXU_DIM] * cos
               + ysw[:, lo:lo + MXU_DIM] * gs_ref[:, lo:lo + MXU_DIM] * sin) * r
        dst, off = (q_ref, lo) if lo < half else (k_ref, lo - half)
        dst[:, off:off + MXU_DIM] = out.astype(BF16)


def _aproj(x, g, w, wsw, e, gc, gs, cos, sin, t_len):
    n, d = x.shape
    tm = math.gcd(APROJ_TM, t_len)
    npos = t_len // tm
    nqk = wsw.shape[1]
    const = lambda i: (0, 0)
    ktile = A_KEY_T if t_len % A_KEY_T == 0 else ATT_T
    vt_shape = _values_shape(n // t_len, t_len, A_KV_HEADS, HEAD_DIM, ktile)
    per_ktile = ktile // tm
    return pl.pallas_call(
        _aproj_kernel,
        out_shape=(jax.ShapeDtypeStruct((n, nqk // 2), BF16),
                   jax.ShapeDtypeStruct((n, nqk // 2), BF16),
                   jax.ShapeDtypeStruct(vt_shape, BF16)),
        grid=(n // tm,),
        in_specs=[
            pl.BlockSpec((tm, d), lambda i: (i, 0)),
            pl.BlockSpec((1, d), const),
            pl.BlockSpec(w.shape, const),
            pl.BlockSpec(wsw.shape, const),
            pl.BlockSpec(e.shape, const),
            pl.BlockSpec(gc.shape, const),
            pl.BlockSpec(gs.shape, const),
            pl.BlockSpec((tm, LANES), lambda i: (i % npos, 0)),
            pl.BlockSpec((tm, LANES), lambda i: (i % npos, 0)),
        ],
        out_specs=(pl.BlockSpec((tm, nqk // 2), lambda i: (i, 0)),
                   pl.BlockSpec((tm, nqk // 2), lambda i: (i, 0)),
                   pl.BlockSpec((None, GROUPS, 1, vt_shape[3], tm),
                                lambda i: (i // npos, 0, (i % npos) // per_ktile, 0, (i % npos) % per_ktile))),
        compiler_params=_cparams(("parallel",)),
        name="proj_a",
    )(x, g, w, wsw, e, gc, gs, cos, sin)


def _outproj_kernel(x_ref, oa_ref, ob_ref, oc_ref, od_ref, w_ref, o_ref):
    acc = x_ref[...]
    for m, ref in enumerate((oa_ref, ob_ref, oc_ref, od_ref)):
        acc = acc + jnp.dot(ref[...], w_ref[m * GROUP_WIDTH:(m + 1) * GROUP_WIDTH, :],
                            preferred_element_type=F32)
    o_ref[...] = acc


def _outproj(x, oa, ob, oc, od, w):
    n, d = x.shape
    tm = min(OUT_TM, n)
    mix = pl.BlockSpec((tm, GROUP_WIDTH), lambda i: (i, 0))
    return pl.pallas_call(
        _outproj_kernel,
        out_shape=jax.ShapeDtypeStruct((n, d), F32),
        grid=(n // tm,),
        in_specs=[pl.BlockSpec((tm, d), lambda i: (i, 0)), mix, mix, mix, mix,
                  pl.BlockSpec(w.shape, lambda i: (0, 0))],
        out_specs=pl.BlockSpec((tm, d), lambda i: (i, 0)),
        compiler_params=_cparams(("parallel",)),
        name="out_proj",
    )(x, oa, ob, oc, od, w)


def _attn_init(q_ref, qm_ref):
    qt = q_ref[...].astype(F32).T
    unit = lax.broadcasted_iota(jnp.int32, qt.shape, 0) // HEAD_DIM
    for u in range(qm_ref.shape[0]):
        qm_ref[u] = jnp.where(unit == u, qt, 0.0).astype(BF16)


def _scores(k, qm_ref, u):
    return jnp.dot(k, qm_ref[u], preferred_element_type=F32)


def _attn_units(k, qm_ref, update, st0_ref=None, k_next=None, from_st0=True):
    n_units = qm_ref.shape[0]
    st_next = st0_ref[...] if (st0_ref is not None and from_st0) else _scores(k, qm_ref, 0)
    for u in range(n_units):
        st = st_next
        if u + 1 < n_units:
            st_next = _scores(k, qm_ref, u + 1)
        elif st0_ref is not None:
            st0_ref[...] = _scores(k_next, qm_ref, 0)
        update(u, st)


def _tile_first(st, vt, u, m_ref, acc_ref, bias, shift):
    if bias is not None:
        st = st + bias
    m_tile = jnp.max(st, axis=0, keepdims=True)
    m_ref[u] = m_tile if shift is None else m_tile + shift
    acc_ref[u] = jnp.dot(vt, jnp.exp2(st - m_tile).astype(BF16), preferred_element_type=F32)


def _tile_fixed(st, vt, u, m_ref, acc_ref, bias, shift):
    if bias is not None:
        st = st + bias
    sub = m_ref[u] if shift is None else m_ref[u] - shift
    acc_ref[u] += jnp.dot(vt, jnp.exp2(st - sub).astype(BF16), preferred_element_type=F32)


def _tile_online(st, vt, u, m_ref, acc_ref, bias, shift):
    if bias is not None:
        st = st + bias
    smax = jnp.max(st, axis=0, keepdims=True)
    if shift is not None:
        smax = smax + shift
    m_old = m_ref[u]
    m_new = jnp.maximum(m_old, smax)
    sub = m_new if shift is None else m_new - shift
    p = jnp.exp2(st - sub).astype(BF16)
    acc_ref[u] = jnp.exp2(m_old - m_new) * acc_ref[u] + jnp.dot(vt, p, preferred_element_type=F32)
    m_ref[u] = m_new


def _attn_finish(o_ref, acc_ref, dv, diff_refs, lam_init):
    n_units = acc_ref.shape[0]
    outs = [acc_ref[u, :dv, :] / acc_ref[u, dv:dv + 1, :] for u in range(n_units)]
    if diff_refs:
        lam_ref, subg_ref = diff_refs
        lp = lam_ref[...]
        lam = (jnp.exp(jnp.sum(lp[0:1] * lp[1:2], keepdims=True))
               - jnp.exp(jnp.sum(lp[2:3] * lp[3:4], keepdims=True)) + lam_init)
        outs = [outs[2 * h] - lam * outs[2 * h + 1] for h in range(n_units // 2)]
    o = jnp.concatenate(outs, axis=0).T
    if diff_refs:
        o = jnp.concatenate([_rms(o[:, lo:lo + dv], subg_ref[...]) * (1.0 - lam_init)
                             for lo in range(0, o.shape[1], dv)], axis=1)
    o_ref[...] = o.astype(BF16)


def _value_rows(u, n_units, rows_in_block, dv):
    per_head = dv + ONES_ROWS
    heads_in_block = rows_in_block // per_head
    lo = (u * heads_in_block // n_units) * per_head
    return lo, lo + per_head


PLAIN, FAR_LEFT, FAR_RIGHT, NEAR = "plain", 0, 1, "near"


def _attn_kernel(*refs, dv, reach, far, bias_index, lam_init):
    it = iter(refs)
    q_ref, k_ref, vt_ref = next(it), next(it), next(it)
    bias_ref = next(it) if reach is not None else None
    far_ref = next(it) if far else None
    diff_refs = (next(it), next(it)) if lam_init is not None else None
    o_ref, qm_ref, m_ref, acc_ref, st0_ref = next(it), next(it), next(it), next(it), next(it)

    n_units = qm_ref.shape[0]
    nk, rows, ktile = vt_ref.shape
    keys_per_query_tile = ktile // q_ref.shape[0]
    assert reach is None or keys_per_query_tile == 1
    i = pl.program_id(2) // keys_per_query_tile
    bias_heads = bias_ref.shape[1] if bias_ref is not None else 0
    _attn_init(q_ref, qm_ref)
    lo, hi = (0, nk) if reach is None else (jnp.maximum(i - reach, 0), jnp.minimum(i + reach + 1, nk))
    p_first, p_end = (1, nk) if (reach is None or far) else (lo + 1, hi)

    def tile_at(p):
        p = jnp.minimum(p, p_end - 1)
        return jnp.clip(p - (p <= i).astype(jnp.int32), 0, nk - 1)

    def keys(j):
        return k_ref[pl.ds(pl.multiple_of(j * ktile, ktile), ktile), :]

    def updater(tile_fn, j, side):
        def update(u, st):
            r0, r1 = _value_rows(u, n_units, rows, dv)
            h = u * bias_heads // n_units
            bias = bias_ref[bias_index(i, j - i, nk), h] if side == NEAR else None
            shift = far_ref[side * bias_heads + h:side * bias_heads + h + 1, :] if side in (FAR_LEFT, FAR_RIGHT) else None
            tile_fn(st, vt_ref[j, r0:r1, :], u, m_ref, acc_ref, bias, shift)
        return update

    def visit_rest(tile_fn, pipelined):
        def span(p_lo, p_hi, side):
            def body(p, carry):
                j = tile_at(p)
                if pipelined:
                    _attn_units(keys(j), qm_ref, updater(tile_fn, j, side), st0_ref, keys(tile_at(p + 1)))
                else:
                    _attn_units(keys(j), qm_ref, updater(tile_fn, j, side))
                return carry
            lax.fori_loop(p_lo, p_hi, body, 0)
        if reach is None:
            span(1, nk, PLAIN)
        elif far:
            span(1, lo + 1, FAR_LEFT)
            span(lo + 1, hi, NEAR)
            span(hi, nk, FAR_RIGHT)
        else:
            span(lo + 1, hi, NEAR)

    diag = PLAIN if reach is None else NEAR
    _attn_units(keys(i), qm_ref, updater(_tile_first, i, diag), st0_ref, keys(tile_at(p_first)), from_st0=False)
    visit_rest(_tile_fixed, pipelined=True)

    finite = jnp.min(jnp.where(jnp.isfinite(acc_ref[...]), 1.0, 0.0))

    @pl.when(finite < 0.5)
    def _():
        m_ref[...] = jnp.full(m_ref.shape, -jnp.inf, F32)
        acc_ref[...] = jnp.zeros(acc_ref.shape, F32)
        _attn_units(keys(i), qm_ref, updater(_tile_online, i, diag))
        visit_rest(_tile_online, pipelined=False)

    _attn_finish(o_ref, acc_ref, dv, diff_refs, lam_init)


def _attention(q, k, vt, *, t_len, tile, qcol, kcol, bias=None, reach=None, bias_index=None, far=None, diff=None,
               flags=None):
    n = q.shape[0]
    batch, nt = n // t_len, t_len // tile
    dv = B_V_DIM if diff else HEAD_DIM
    rows = vt.shape[3]
    once = pl.Buffered(1)
    in_specs = [
        pl.BlockSpec((tile, MXU_DIM), lambda b, g, i: (b * nt + i, qcol + g)),
        pl.BlockSpec((t_len, MXU_DIM), lambda b, g, i: (b, kcol + g), pipeline_mode=once),
        pl.BlockSpec((None, None) + vt.shape[2:], lambda b, g, i: (b, g, 0, 0, 0), pipeline_mode=once),
    ]
    args = [q, k, vt]
    if bias is not None:
        in_specs.append(pl.BlockSpec((bias.shape[0], None, bias.shape[2], tile, tile),
                                     lambda b, g, i: (0, g, 0, 0, 0), pipeline_mode=once))
        args.append(bias)
    if far is not None:
        in_specs.append(pl.BlockSpec((None,) + far.shape[1:], lambda b, g, i: (g, 0, 0)))
        args.append(far)
    lam_init = None
    if diff:
        lam_p, subg, lam_init = diff
        in_specs += [pl.BlockSpec(lam_p.shape, lambda b, g, i: (0, 0)),
                     pl.BlockSpec(subg.shape, lambda b, g, i: (0, 0))]
        args += [lam_p, subg]
    return pl.pallas_call(
        functools.partial(_attn_kernel, dv=dv, reach=reach, far=far is not None, bias_index=bias_index,
                          lam_init=lam_init),
        out_shape=jax.ShapeDtypeStruct((n, GROUP_WIDTH), BF16),
        grid=(batch, GROUPS, nt),
        in_specs=in_specs,
        out_specs=pl.BlockSpec((tile, MXU_DIM), lambda b, g, i: (b * nt + i, g)),
        scratch_shapes=[pltpu.VMEM((HEADS_PER_GROUP, MXU_DIM, tile), BF16),
                        pltpu.VMEM((HEADS_PER_GROUP, 1, tile), F32),
                        pltpu.VMEM((HEADS_PER_GROUP, dv + ONES_ROWS, tile), F32),
                        pltpu.VMEM((vt.shape[4], tile), F32)],
        compiler_params=_cparams(("parallel", "parallel", "arbitrary"), flags),
        name="attention",
    )(*args)


def _t5_bucket(rel):
    nb = T5_BUCKETS // 2
    max_exact = nb // 2
    side = (rel > 0).astype(jnp.int32) * nb
    n = jnp.abs(rel)
    large = max_exact + (jnp.log(jnp.maximum(n, 1).astype(F32) / max_exact)
                         / math.log(T5_MAX_DIST / max_exact) * (nb - max_exact)).astype(jnp.int32)
    large = jnp.minimum(large, nb - 1)
    return side + jnp.where(n < max_exact, n, large)


def _skew(v, n):
    lead = v.shape[:-1]
    r = jnp.concatenate([v[..., ::-1], jnp.zeros(lead + (1,), v.dtype)], axis=-1)
    x = jnp.broadcast_to(r[..., None, :], lead + (n, r.shape[-1]))
    k = np.arange(n)[:, None]
    for b in range(n.bit_length() - 1):
        x = jnp.where(jnp.asarray((k >> b) & 1 == 1), jnp.roll(x, 1 << b, axis=-1), x)
    return x


def _toeplitz(v, n):
    return _skew(v, n)[..., n - 1:2 * n - 1]


def _rel_tiles(table, reach, n):
    x = _skew(table, n)
    return jnp.stack([x[..., (2 * reach + 1 - o) * n - 1:(2 * reach + 2 - o) * n - 1]
                      for o in range(2 * reach + 1)])


def _group_heads(b):
    t, h = b.shape[:2]
    return b.reshape(t, GROUPS, h // GROUPS, *b.shape[2:])


def _t5_saturation():
    nb = T5_BUCKETS // 2
    max_exact = nb // 2
    return math.ceil(max_exact * (T5_MAX_DIST / max_exact) ** ((nb - 1 - max_exact) / (nb - max_exact))) + 1


def _bias_b(t5_b):
    n, reach = ATT_T, B_NEAR
    assert reach * n + 1 >= _t5_saturation()
    rel = jnp.arange(-(reach * n + n - 1), reach * n + n, dtype=jnp.int32)
    table = (t5_b[_t5_bucket(rel)].astype(F32) * LOG2E).T
    far = (t5_b[_t5_bucket(jnp.array([-T5_MAX_DIST * 2, T5_MAX_DIST * 2], jnp.int32))].astype(F32) * LOG2E)
    heads = t5_b.shape[1]
    far = far.reshape(2, GROUPS, heads // GROUPS).transpose(1, 0, 2).reshape(GROUPS, 2 * heads // GROUPS, 1)
    return _group_heads(_rel_tiles(table, reach, n)), jnp.broadcast_to(far, far.shape[:2] + (n,))


def _bias_d(t5_d):
    n = ATT_T
    reach = max(w // 2 for w, _ in D_BRANCHES) // n
    rel = jnp.arange(-(reach * n + n - 1), reach * n + n, dtype=jnp.int32)
    mult = sum(((jnp.abs(rel) <= w // 2) & (rel % d == 0)).astype(F32) for w, d in D_BRANCHES)
    b = t5_d[_t5_bucket(rel)].astype(F32).T
    table = jnp.where(mult[None] > 0, (b + jnp.log(jnp.maximum(mult, 1.0))[None]) * LOG2E, NEG_INF)
    return _group_heads(_rel_tiles(table, reach, n)), reach


def _bias_c(rpb, rows):
    heads = rpb.shape[0]
    kr = min(C_WIN_ROWS, rows)
    n_row_tiles = rows // C_ROWS_PER_TILE
    pad = GRID_W - C_WIN_COLS
    col_tiles = _toeplitz(jnp.pad(rpb.astype(F32) * LOG2E, ((0, 0), (0, 0), (pad, pad))), GRID_W)
    c = np.arange(GRID_W)[None, :]
    kc = np.arange(GRID_W)[:, None]
    cs = np.clip(c - C_WIN_COLS // 2, 0, GRID_W - C_WIN_COLS)
    col_ok = jnp.asarray((kc >= cs) & (kc < cs + C_WIN_COLS))
    col_tiles = jnp.where(col_ok, col_tiles, NEG_INF)
    masked = jnp.full((heads, GRID_W, GRID_W), NEG_INF, F32)
    tiles = []
    for it in (0, min(1, n_row_tiles - 1), n_row_tiles - 1):
        for off in (-1, 0, 1):
            key_rows = []
            for a in range(C_ROWS_PER_TILE):
                blocks = []
                for b in range(C_ROWS_PER_TILE):
                    r = it * C_ROWS_PER_TILE + b
                    key_r = (it + off) * C_ROWS_PER_TILE + a
                    rs = min(max(r - kr // 2, 0), rows - kr)
                    ok = 0 <= key_r < rows and rs <= key_r < rs + kr
                    blocks.append(col_tiles[:, key_r - r + C_WIN_ROWS - 1] if ok else masked)
                key_rows.append(jnp.concatenate(blocks, axis=2))
            tiles.append(jnp.concatenate(key_rows, axis=1))
    return _group_heads(jnp.stack(tiles))


def _c_tile_index(i, off, nt):
    variant = jnp.where(i == 0, 0, jnp.where(i == nt - 1, 2, 1))
    return variant * 3 + off + 1


def _rope_tables(t_len):
    t = jnp.arange(t_len, dtype=jnp.int32)
    n_freq = HEAD_DIM // 4
    inv_freq = ROPE_THETA ** (-jnp.arange(n_freq, dtype=F32) / n_freq)
    ang = jnp.concatenate([(t // GRID_W).astype(F32)[:, None] * inv_freq[None, :],
                           (t % GRID_W).astype(F32)[:, None] * inv_freq[None, :]], axis=-1)
    cos, sin = jnp.cos(ang), jnp.sin(ang)
    reps = LANES // HEAD_DIM
    return (jnp.tile(jnp.concatenate([cos, cos], axis=-1), (1, reps)),
            jnp.tile(jnp.concatenate([-sin, sin], axis=-1), (1, reps)))


def _deinterleave(w):
    return jnp.swapaxes(w.reshape(w.shape[:-1] + (HEAD_DIM // 2, 2)), -1, -2).reshape(w.shape)


def _swap_halves(w):
    return jnp.concatenate([w[..., HEAD_DIM // 2:], w[..., :HEAD_DIM // 2]], axis=-1)


def _prep_ffn(w_gate, w_up, w_down):
    return w_gate.astype(BF16), w_up.astype(BF16), w_down.astype(BF16)


def _prep_mixer_a(w_in, g_q, g_k):
    d = w_in.shape[0]
    qw = A_HEADS * HEAD_DIM
    kw = A_KV_HEADS * HEAD_DIM
    rep = A_HEADS // A_KV_HEADS
    wq = _deinterleave(w_in[:, :qw].reshape(d, A_HEADS, HEAD_DIM))
    wk = _deinterleave(w_in[:, qw:qw + kw].reshape(d, A_KV_HEADS, 1, HEAD_DIM))
    wk = jnp.broadcast_to(wk, (d, A_KV_HEADS, rep, HEAD_DIM)).reshape(d, A_HEADS, HEAD_DIM)
    wqk = jnp.concatenate([wq, wk], axis=1)
    wv = w_in[:, qw + kw:qw + 2 * kw]
    w = jnp.concatenate([wqk.reshape(d, -1), wv], axis=1).astype(BF16)
    wsw = _swap_halves(wqk).reshape(d, -1).astype(BF16)
    gq, gk = _deinterleave(g_q) * (SCALE * LOG2E), _deinterleave(g_k)
    g = jnp.concatenate([jnp.tile(gq, A_HEADS), jnp.tile(gk, A_HEADS)])
    gsw = jnp.concatenate([jnp.tile(_swap_halves(gq), A_HEADS), jnp.tile(_swap_halves(gk), A_HEADS)])
    return w, wsw, g[None, :].astype(F32), gsw[None, :].astype(F32)


def _prep_mixers_bcd(w_in):
    a_cols = (A_HEADS + 2 * A_KV_HEADS) * HEAD_DIM
    w = w_in[:, a_cols:]
    is_q = (np.arange(w.shape[1]) // GROUP_WIDTH) % 3 == 0
    return (w * jnp.asarray(np.where(is_q, SCALE * LOG2E, 1.0), F32)[None, :]).astype(BF16)


def _head_sum_matrix():
    idx = np.arange(MXU_DIM) // HEAD_DIM
    return jnp.asarray(idx[:, None] == idx[None, :], dtype=BF16)


def _trunk(x3, layers, bias_b, far_b, bias_d, d_reach, final_norm):
    batch, t_len, d = x3.shape
    x = x3.reshape(batch * t_len, d)
    cos, sin = _rope_tables(t_len)
    e = _head_sum_matrix()
    depth = len(layers)
    for l, p in enumerate(layers):
        x = _ffn(x, p["ffn1_norm"], *p["ffn1"], final_norm, final=False)
        qa, ka, vta = _aproj(x, p["mix_norm"], p["wa"], p["wa_sw"], e, p["ga"], p["ga_sw"], cos, sin, t_len)
        qk, vtb, vtc, vtd = _proj(x, p["mix_norm"], p["w_bcd"], t_len)

        oa = _attention(qa, ka, vta, t_len=t_len, tile=ATT_T, qcol=0, kcol=0)
        ob = _attention(qk, qk, vtb, t_len=t_len, tile=ATT_T, qcol=0, kcol=GROUPS,
                        bias=bias_b, reach=B_NEAR, bias_index=lambda i, off, nt: off + B_NEAR, far=far_b,
                        diff=(p["b_lambda"], p["b_subln"], 0.8 - 0.6 * math.exp(-0.3 * l)))
        oc = _attention(qk, qk, vtc, t_len=t_len, tile=C_T, qcol=2 * GROUPS, kcol=3 * GROUPS,
                        bias=_bias_c(p["c_rpb"], t_len // GRID_W), reach=1, bias_index=_c_tile_index)
        od = _attention(qk, qk, vtd, t_len=t_len, tile=ATT_T, qcol=4 * GROUPS, kcol=5 * GROUPS,
                        bias=bias_d, reach=d_reach, bias_index=lambda i, off, nt: off + d_reach)
        x = _outproj(x, oa, ob, oc, od, p["w_out"])
        x = _ffn(x, p["ffn2_norm"], *p["ffn2"], final_norm, final=(l == depth - 1))
    return x.reshape(batch, t_len, d)


def kernel(x_prompt, x_sample, ffn1_norm, ffn1_w_gate, ffn1_w_up, ffn1_w_down, mix_norm, w_in, a_q_norm, a_k_norm, b_lambda, b_subln, c_rpb, w_out, ffn2_norm, ffn2_w_gate, ffn2_w_up, ffn2_w_down, t5_table, final_norm):
    depth = w_in.shape[0]
    layers = []
    for l in range(depth):
        wa, wa_sw, ga, ga_sw = _prep_mixer_a(w_in[l], a_q_norm[l], a_k_norm[l])
        layers.append(dict(
            ffn1_norm=ffn1_norm[l][None, :], ffn2_norm=ffn2_norm[l][None, :], mix_norm=mix_norm[l][None, :],
            ffn1=_prep_ffn(ffn1_w_gate[l], ffn1_w_up[l], ffn1_w_down[l]),
            ffn2=_prep_ffn(ffn2_w_gate[l], ffn2_w_up[l], ffn2_w_down[l]),
            wa=wa, wa_sw=wa_sw, ga=ga, ga_sw=ga_sw,
            w_bcd=_prep_mixers_bcd(w_in[l]),
            b_lambda=b_lambda[l].astype(F32), b_subln=b_subln[l][None, :].astype(F32),
            c_rpb=c_rpb[l], w_out=w_out[l].astype(BF16)))
    bias_b, far_b = _bias_b(t5_table[:, :B_HEADS])
    bias_d, d_reach = _bias_d(t5_table[:, B_HEADS:])
    fn = final_norm[None, :]
    return tuple(_trunk(x3, layers, bias_b, far_b, bias_d, d_reach, fn) for x3 in (x_prompt, x_sample))
```

```python
import functools
import math

import jax
import jax.numpy as jnp
import numpy as np
from jax import lax
from jax.experimental import pallas as pl
from jax.experimental.pallas import tpu as pltpu

F32 = jnp.float32
BF16 = jnp.bfloat16

HEAD_DIM = 64
GRID_W = 64
EPS = 1e-6
NEG_INF = -1e30
SCALE = HEAD_DIM ** -0.5
LOG2E = math.log2(math.e)
ROPE_THETA = 10000.0
A_HEADS, A_KV_HEADS = 8, 2
B_HEADS, B_V_DIM = 4, 128
C_HEADS, C_WIN_ROWS, C_WIN_COLS = 8, 8, 16
D_HEADS = 8
D_BRANCHES = ((128, 1), (512, 4), (2048, 16))
T5_BUCKETS, T5_MAX_DIST = 32, 1024
GROUP_WIDTH = 512

LANES = 128
BF16_SUBLANES = 16
MXU_DIM = 256
HEADS_PER_GROUP = MXU_DIM // HEAD_DIM
GROUPS = GROUP_WIDTH // MXU_DIM
VMEM_LIMIT = 56 * 1024 * 1024
FFN_TM, FFN_TF = 1024, 256
PROJ_TM, PROJ_TN = 1024, 1536
APROJ_TM = 512
OUT_TM = 512
ATT_T = 512
C_ROWS_PER_TILE = 4
C_T = C_ROWS_PER_TILE * GRID_W
ONES_ROWS = BF16_SUBLANES
B_NEAR = 2


def _cparams(sem, flags=None):
    return pltpu.CompilerParams(dimension_semantics=sem, vmem_limit_bytes=VMEM_LIMIT, flags=flags)


def _rms(x, g):
    return x * lax.rsqrt(jnp.mean(x * x, axis=-1, keepdims=True) + EPS) * g


def _ffn_kernel(x_ref, g_ref, wg_ref, wu_ref, wd_ref, fg_ref, o_ref, xn_ref, *, final):
    j = pl.program_id(1)

    @pl.when(j == 0)
    def _():
        x = x_ref[...]
        xn_ref[...] = _rms(x, g_ref[...]).astype(BF16)
        o_ref[...] = x

    h = jnp.dot(xn_ref[...], jnp.concatenate([wg_ref[...], wu_ref[...]], axis=1), preferred_element_type=F32)
    tf = h.shape[1] // 2
    hg, hu = h[:, :tf], h[:, tf:]
    a = (hg * jax.nn.sigmoid(hg)) * hu
    o_ref[...] += jnp.dot((0.5 * a).astype(BF16), wd_ref[...], preferred_element_type=F32)

    if final:
        @pl.when(j == pl.num_programs(1) - 1)
        def _():
            o_ref[...] = _rms(o_ref[...], fg_ref[...])


def _ffn(x, g, wg, wu, wd, fg, *, final):
    n, d = x.shape
    dff = wd.shape[0]
    tm, tf = min(FFN_TM, n), FFN_TF
    return pl.pallas_call(
        functools.partial(_ffn_kernel, final=final),
        out_shape=jax.ShapeDtypeStruct((n, d), F32),
        grid=(n // tm, dff // tf),
        in_specs=[
            pl.BlockSpec((tm, d), lambda i, j: (i, 0)),
            pl.BlockSpec((1, d), lambda i, j: (0, 0)),
            pl.BlockSpec((d, tf), lambda i, j: (0, j)),
            pl.BlockSpec((d, tf), lambda i, j: (0, j)),
            pl.BlockSpec((tf, d), lambda i, j: (j, 0)),
            pl.BlockSpec((1, d), lambda i, j: (0, 0)),
        ],
        out_specs=pl.BlockSpec((tm, d), lambda i, j: (i, 0)),
        scratch_shapes=[pltpu.VMEM((tm, d), BF16)],
        compiler_params=_cparams(("parallel", "arbitrary")),
        name="ffn",
    )(x, g, wg, wu, wd, fg)


def _store_values_t(vt_ref, v, dv):
    groups, n_tiles, rows, tile = vt_ref.shape
    per_head = dv + ONES_ROWS
    ones = jnp.ones((ONES_ROWS, tile), BF16)
    heads_per_group = rows // per_head
    for t in range(n_tiles):
        vt = v[t * tile:(t + 1) * tile, :].T
        for g in range(groups):
            for h in range(heads_per_group):
                src = (g * heads_per_group + h) * dv
                vt_ref[g, t, h * per_head:h * per_head + dv, :] = vt[src:src + dv, :].astype(BF16)
                vt_ref[g, t, h * per_head + dv:(h + 1) * per_head, :] = ones


def _proj_kernel(x_ref, g_ref, w_ref, o_ref, vtb_ref, vtc_ref, vtd_ref, xn_ref):
    j = pl.program_id(1)

    @pl.when(j == 0)
    def _():
        xn_ref[...] = _rms(x_ref[...], g_ref[...]).astype(BF16)

    y = jnp.dot(xn_ref[...], w_ref[...], preferred_element_type=F32)
    nqk = o_ref.shape[1]
    o_ref[...] = y[:, :nqk].astype(BF16)
    for m, (vt_ref, dv) in enumerate(((vtb_ref, B_V_DIM), (vtc_ref, HEAD_DIM), (vtd_ref, HEAD_DIM))):
        @pl.when(j == m)
        def _():
            _store_values_t(vt_ref, y[:, nqk:], dv)


def _values_shape(batch, t_len, heads, dv, tile):
    return (batch, GROUPS, t_len // tile, heads // GROUPS * (dv + ONES_ROWS), tile)


def _proj(x, g, w, t_len):
    n, d = x.shape
    batch = n // t_len
    tm, tn = math.gcd(PROJ_TM, t_len), PROJ_TN
    npos = t_len // tm
    nqk = 2 * GROUP_WIDTH
    vt_shapes = [_values_shape(batch, t_len, B_HEADS, B_V_DIM, ATT_T),
                 _values_shape(batch, t_len, C_HEADS, HEAD_DIM, C_T),
                 _values_shape(batch, t_len, D_HEADS, HEAD_DIM, ATT_T)]
    vt_specs = [pl.BlockSpec((None, GROUPS, tm // s[4], s[3], s[4]), lambda i, j: (i // npos, 0, i % npos, 0, 0))
                for s in vt_shapes]
    return pl.pallas_call(
        _proj_kernel,
        out_shape=[jax.ShapeDtypeStruct((n, 3 * nqk), BF16)] + [jax.ShapeDtypeStruct(s, BF16) for s in vt_shapes],
        grid=(n // tm, w.shape[1] // tn),
        in_specs=[
            pl.BlockSpec((tm, d), lambda i, j: (i, 0)),
            pl.BlockSpec((1, d), lambda i, j: (0, 0)),
            pl.BlockSpec((d, tn), lambda i, j: (0, j)),
        ],
        out_specs=[pl.BlockSpec((tm, nqk), lambda i, j: (i, j))] + vt_specs,
        scratch_shapes=[pltpu.VMEM((tm, d), BF16)],
        compiler_params=_cparams(("parallel", "arbitrary")),
        name="proj_bcd",
    )(x, g, w)


def _aproj_kernel(x_ref, g_ref, w_ref, e_ref, gc_ref, gs_ref, cos_ref, sin_ref, q_ref, k_ref, vt_ref):
    xn = _rms(x_ref[...], g_ref[...]).astype(BF16)
    y = jnp.dot(xn, w_ref[...], preferred_element_type=F32)
    nqk = gc_ref.shape[1]
    _store_values_t(vt_ref, y[:, nqk:], HEAD_DIM)
    e = e_ref[...]
    cos = jnp.concatenate([cos_ref[...]] * (MXU_DIM // LANES), axis=1)
    sin = jnp.concatenate([sin_ref[...]] * (MXU_DIM // LANES), axis=1)
    first_half = lax.broadcasted_iota(jnp.int32, (y.shape[0], MXU_DIM), 1) % HEAD_DIM < HEAD_DIM // 2
    half = nqk // 2
    for lo in range(0, nqk, MXU_DIM):
        yc = y[:, lo:lo + MXU_DIM]
        partner = jnp.where(first_half, pltpu.roll(yc, MXU_DIM - HEAD_DIM // 2, axis=1),
                            pltpu.roll(yc, HEAD_DIM // 2, axis=1))
        sq = yc * yc
        hi = sq.astype(BF16)
        rest = (sq - hi.astype(F32)).astype(BF16)
        ssum = (jnp.dot(hi, e, preferred_element_type=F32)
                + jnp.dot(rest, e, preferred_element_type=F32))
        r = lax.rsqrt(ssum * (1.0 / HEAD_DIM) + EPS)
        out = (yc * gc_ref[:, lo:lo + MXU_DIM] * cos
               + partner * gs_ref[:, lo:lo + MXU_DIM] * sin) * r
        dst, off = (q_ref, lo) if lo < half else (k_ref, lo - half)
        dst[:, off:off + MXU_DIM] = out.astype(BF16)


def _aproj(x, g, w, e, gc, gs, cos, sin, t_len):
    n, d = x.shape
    tm = math.gcd(APROJ_TM, t_len)
    npos = t_len // tm
    nqk = gc.shape[1]
    const = lambda i: (0, 0)
    vt_shape = _values_shape(n // t_len, t_len, A_KV_HEADS, HEAD_DIM, ATT_T)
    return pl.pallas_call(
        _aproj_kernel,
        out_shape=(jax.ShapeDtypeStruct((n, nqk // 2), BF16),
                   jax.ShapeDtypeStruct((n, nqk // 2), BF16),
                   jax.ShapeDtypeStruct(vt_shape, BF16)),
        grid=(n // tm,),
        in_specs=[
            pl.BlockSpec((tm, d), lambda i: (i, 0)),
            pl.BlockSpec((1, d), const),
            pl.BlockSpec(w.shape, const),
            pl.BlockSpec(e.shape, const),
            pl.BlockSpec(gc.shape, const),
            pl.BlockSpec(gs.shape, const),
            pl.BlockSpec((tm, LANES), lambda i: (i % npos, 0)),
            pl.BlockSpec((tm, LANES), lambda i: (i % npos, 0)),
        ],
        out_specs=(pl.BlockSpec((tm, nqk // 2), lambda i: (i, 0)),
                   pl.BlockSpec((tm, nqk // 2), lambda i: (i, 0)),
                   pl.BlockSpec((None, GROUPS, tm // ATT_T) + vt_shape[3:],
                                lambda i: (i // npos, 0, i % npos, 0, 0))),
        compiler_params=_cparams(("parallel",)),
        name="proj_a",
    )(x, g, w, e, gc, gs, cos, sin)


def _outproj_kernel(x_ref, oa_ref, ob_ref, oc_ref, od_ref, w_ref, o_ref):
    acc = x_ref[...]
    for m, ref in enumerate((oa_ref, ob_ref, oc_ref, od_ref)):
        acc = acc + jnp.dot(ref[...], w_ref[m * GROUP_WIDTH:(m + 1) * GROUP_WIDTH, :],
                            preferred_element_type=F32)
    o_ref[...] = acc


def _outproj(x, oa, ob, oc, od, w):
    n, d = x.shape
    tm = min(OUT_TM, n)
    mix = pl.BlockSpec((tm, GROUP_WIDTH), lambda i: (i, 0))
    return pl.pallas_call(
        _outproj_kernel,
        out_shape=jax.ShapeDtypeStruct((n, d), F32),
        grid=(n // tm,),
        in_specs=[pl.BlockSpec((tm, d), lambda i: (i, 0)), mix, mix, mix, mix,
                  pl.BlockSpec(w.shape, lambda i: (0, 0))],
        out_specs=pl.BlockSpec((tm, d), lambda i: (i, 0)),
        compiler_params=_cparams(("parallel",)),
        name="out_proj",
    )(x, oa, ob, oc, od, w)


def _attn_init(q_ref, qm_ref):
    qt = q_ref[...].astype(F32).T
    unit = lax.broadcasted_iota(jnp.int32, qt.shape, 0) // HEAD_DIM
    for u in range(qm_ref.shape[0]):
        qm_ref[u] = jnp.where(unit == u, qt, 0.0).astype(BF16)


def _scores(k, qm_ref, u):
    return jnp.dot(k, qm_ref[u], preferred_element_type=F32)


def _attn_units(k, qm_ref, update, st0_ref=None, k_next=None, from_st0=True):
    n_units = qm_ref.shape[0]
    st_next = st0_ref[...] if (st0_ref is not None and from_st0) else _scores(k, qm_ref, 0)
    for u in range(n_units):
        st = st_next
        if u + 1 < n_units:
            st_next = _scores(k, qm_ref, u + 1)
        elif st0_ref is not None:
            st0_ref[...] = _scores(k_next, qm_ref, 0)
        update(u, st)


def _tile_first(st, vt, u, m_ref, acc_ref, bias, shift):
    if bias is not None:
        st = st + bias
    m_tile = jnp.max(st, axis=0, keepdims=True)
    m_ref[u] = m_tile if shift is None else m_tile + shift
    acc_ref[u] = jnp.dot(vt, jnp.exp2(st - m_tile).astype(BF16), preferred_element_type=F32)


def _tile_fixed(st, vt, u, m_ref, acc_ref, bias, shift):
    if bias is not None:
        st = st + bias
    sub = m_ref[u] if shift is None else m_ref[u] - shift
    acc_ref[u] += jnp.dot(vt, jnp.exp2(st - sub).astype(BF16), preferred_element_type=F32)


def _tile_online(st, vt, u, m_ref, acc_ref, bias, shift):
    if bias is not None:
        st = st + bias
    smax = jnp.max(st, axis=0, keepdims=True)
    if shift is not None:
        smax = smax + shift
    m_old = m_ref[u]
    m_new = jnp.maximum(m_old, smax)
    sub = m_new if shift is None else m_new - shift
    p = jnp.exp2(st - sub).astype(BF16)
    acc_ref[u] = jnp.exp2(m_old - m_new) * acc_ref[u] + jnp.dot(vt, p, preferred_element_type=F32)
    m_ref[u] = m_new


def _attn_finish(o_ref, acc_ref, dv, diff_refs, lam_init):
    n_units = acc_ref.shape[0]
    outs = [acc_ref[u, :dv, :] / acc_ref[u, dv:dv + 1, :] for u in range(n_units)]
    if diff_refs:
        lam_ref, subg_ref = diff_refs
        lp = lam_ref[...]
        lam = (jnp.exp(jnp.sum(lp[0:1] * lp[1:2], keepdims=True))
               - jnp.exp(jnp.sum(lp[2:3] * lp[3:4], keepdims=True)) + lam_init)
        outs = [outs[2 * h] - lam * outs[2 * h + 1] for h in range(n_units // 2)]
    o = jnp.concatenate(outs, axis=0).T
    if diff_refs:
        o = jnp.concatenate([_rms(o[:, lo:lo + dv], subg_ref[...]) * (1.0 - lam_init)
                             for lo in range(0, o.shape[1], dv)], axis=1)
    o_ref[...] = o.astype(BF16)


def _value_rows(u, n_units, rows_in_block, dv):
    per_head = dv + ONES_ROWS
    heads_in_block = rows_in_block // per_head
    lo = (u * heads_in_block // n_units) * per_head
    return lo, lo + per_head


PLAIN, FAR_LEFT, FAR_RIGHT, NEAR = "plain", 0, 1, "near"


def _attn_kernel(*refs, dv, reach, far, bias_index, lam_init):
    it = iter(refs)
    q_ref, k_ref, vt_ref = next(it), next(it), next(it)
    bias_ref = next(it) if reach is not None else None
    far_ref = next(it) if far else None
    diff_refs = (next(it), next(it)) if lam_init is not None else None
    o_ref, qm_ref, m_ref, acc_ref, st0_ref = next(it), next(it), next(it), next(it), next(it)

    i = pl.program_id(2)
    n_units = qm_ref.shape[0]
    nk, rows, tile = vt_ref.shape
    bias_heads = bias_ref.shape[1] if bias_ref is not None else 0
    _attn_init(q_ref, qm_ref)
    lo, hi = (0, nk) if reach is None else (jnp.maximum(i - reach, 0), jnp.minimum(i + reach + 1, nk))
    p_first, p_end = (1, nk) if (reach is None or far) else (lo + 1, hi)

    def tile_at(p):
        p = jnp.minimum(p, p_end - 1)
        return jnp.clip(p - (p <= i).astype(jnp.int32), 0, nk - 1)

    def keys(j):
        return k_ref[pl.ds(pl.multiple_of(j * tile, tile), tile), :]

    def updater(tile_fn, j, side):
        def update(u, st):
            r0, r1 = _value_rows(u, n_units, rows, dv)
            h = u * bias_heads // n_units
            bias = bias_ref[bias_index(i, j - i, nk), h] if side == NEAR else None
            shift = far_ref[side * bias_heads + h:side * bias_heads + h + 1, :] if side in (FAR_LEFT, FAR_RIGHT) else None
            tile_fn(st, vt_ref[j, r0:r1, :], u, m_ref, acc_ref, bias, shift)
        return update

    def visit_rest(tile_fn, pipelined):
        def span(p_lo, p_hi, side):
            def body(p, carry):
                j = tile_at(p)
                if pipelined:
                    _attn_units(keys(j), qm_ref, updater(tile_fn, j, side), st0_ref, keys(tile_at(p + 1)))
                else:
                    _attn_units(keys(j), qm_ref, updater(tile_fn, j, side))
                return carry
            lax.fori_loop(p_lo, p_hi, body, 0)
        if reach is None:
            span(1, nk, PLAIN)
        elif far:
            span(1, lo + 1, FAR_LEFT)
            span(lo + 1, hi, NEAR)
            span(hi, nk, FAR_RIGHT)
        else:
            span(lo + 1, hi, NEAR)

    diag = PLAIN if reach is None else NEAR
    _attn_units(keys(i), qm_ref, updater(_tile_first, i, diag), st0_ref, keys(tile_at(p_first)), from_st0=False)
    visit_rest(_tile_fixed, pipelined=True)

    finite = jnp.min(jnp.where(jnp.isfinite(acc_ref[...]), 1.0, 0.0))

    @pl.when(finite < 0.5)
    def _():
        m_ref[...] = jnp.full(m_ref.shape, -jnp.inf, F32)
        acc_ref[...] = jnp.zeros(acc_ref.shape, F32)
        _attn_units(keys(i), qm_ref, updater(_tile_online, i, diag))
        visit_rest(_tile_online, pipelined=False)

    _attn_finish(o_ref, acc_ref, dv, diff_refs, lam_init)


def _attention(q, k, vt, *, t_len, tile, qcol, kcol, bias=None, reach=None, bias_index=None, far=None, diff=None,
               flags=None):
    n = q.shape[0]
    batch, nt = n // t_len, t_len // tile
    dv = B_V_DIM if diff else HEAD_DIM
    rows = vt.shape[3]
    once = pl.Buffered(1)
    in_specs = [
        pl.BlockSpec((tile, MXU_DIM), lambda b, g, i: (b * nt + i, qcol + g)),
        pl.BlockSpec((t_len, MXU_DIM), lambda b, g, i: (b, kcol + g), pipeline_mode=once),
        pl.BlockSpec((None, None) + vt.shape[2:], lambda b, g, i: (b, g, 0, 0, 0), pipeline_mode=once),
    ]
    args = [q, k, vt]
    if bias is not None:
        in_specs.append(pl.BlockSpec((bias.shape[0], None, bias.shape[2], tile, tile),
                                     lambda b, g, i: (0, g, 0, 0, 0), pipeline_mode=once))
        args.append(bias)
    if far is not None:
        in_specs.append(pl.BlockSpec((None,) + far.shape[1:], lambda b, g, i: (g, 0, 0)))
        args.append(far)
    lam_init = None
    if diff:
        lam_p, subg, lam_init = diff
        in_specs += [pl.BlockSpec(lam_p.shape, lambda b, g, i: (0, 0)),
                     pl.BlockSpec(subg.shape, lambda b, g, i: (0, 0))]
        args += [lam_p, subg]
    return pl.pallas_call(
        functools.partial(_attn_kernel, dv=dv, reach=reach, far=far is not None, bias_index=bias_index,
                          lam_init=lam_init),
        out_shape=jax.ShapeDtypeStruct((n, GROUP_WIDTH), BF16),
        grid=(batch, GROUPS, nt),
        in_specs=in_specs,
        out_specs=pl.BlockSpec((tile, MXU_DIM), lambda b, g, i: (b * nt + i, g)),
        scratch_shapes=[pltpu.VMEM((HEADS_PER_GROUP, MXU_DIM, tile), BF16),
                        pltpu.VMEM((HEADS_PER_GROUP, 1, tile), F32),
                        pltpu.VMEM((HEADS_PER_GROUP, dv + ONES_ROWS, tile), F32),
                        pltpu.VMEM((vt.shape[4], tile), F32)],
        compiler_params=_cparams(("parallel", "parallel", "arbitrary"), flags),
        name="attention",
    )(*args)


def _t5_bucket(rel):
    nb = T5_BUCKETS // 2
    max_exact = nb // 2
    side = (rel > 0).astype(jnp.int32) * nb
    n = jnp.abs(rel)
    large = max_exact + (jnp.log(jnp.maximum(n, 1).astype(F32) / max_exact)
                         / math.log(T5_MAX_DIST / max_exact) * (nb - max_exact)).astype(jnp.int32)
    large = jnp.minimum(large, nb - 1)
    return side + jnp.where(n < max_exact, n, large)


def _skew(v, n):
    lead = v.shape[:-1]
    r = jnp.concatenate([v[..., ::-1], jnp.zeros(lead + (1,), v.dtype)], axis=-1)
    x = jnp.broadcast_to(r[..., None, :], lead + (n, r.shape[-1]))
    k = np.arange(n)[:, None]
    for b in range(n.bit_length() - 1):
        x = jnp.where(jnp.asarray((k >> b) & 1 == 1), jnp.roll(x, 1 << b, axis=-1), x)
    return x


def _toeplitz(v, n):
    return _skew(v, n)[..., n - 1:2 * n - 1]


def _rel_tiles(table, reach, n):
    x = _skew(table, n)
    return jnp.stack([x[..., (2 * reach + 1 - o) * n - 1:(2 * reach + 2 - o) * n - 1]
                      for o in range(2 * reach + 1)])


def _group_heads(b):
    t, h = b.shape[:2]
    return b.reshape(t, GROUPS, h // GROUPS, *b.shape[2:])


def _t5_saturation():
    nb = T5_BUCKETS // 2
    max_exact = nb // 2
    return math.ceil(max_exact * (T5_MAX_DIST / max_exact) ** ((nb - 1 - max_exact) / (nb - max_exact))) + 1


def _bias_b(t5_b):
    n, reach = ATT_T, B_NEAR
    assert reach * n + 1 >= _t5_saturation()
    rel = jnp.arange(-(reach * n + n - 1), reach * n + n, dtype=jnp.int32)
    table = (t5_b[_t5_bucket(rel)].astype(F32) * LOG2E).T
    far = (t5_b[_t5_bucket(jnp.array([-T5_MAX_DIST * 2, T5_MAX_DIST * 2], jnp.int32))].astype(F32) * LOG2E)
    heads = t5_b.shape[1]
    far = far.reshape(2, GROUPS, heads // GROUPS).transpose(1, 0, 2).reshape(GROUPS, 2 * heads // GROUPS, 1)
    return _group_heads(_rel_tiles(table, reach, n)), jnp.broadcast_to(far, far.shape[:2] + (n,))


def _bias_d(t5_d):
    n = ATT_T
    reach = max(w // 2 for w, _ in D_BRANCHES) // n
    rel = jnp.arange(-(reach * n + n - 1), reach * n + n, dtype=jnp.int32)
    mult = sum(((jnp.abs(rel) <= w // 2) & (rel % d == 0)).astype(F32) for w, d in D_BRANCHES)
    b = t5_d[_t5_bucket(rel)].astype(F32).T
    table = jnp.where(mult[None] > 0, (b + jnp.log(jnp.maximum(mult, 1.0))[None]) * LOG2E, NEG_INF)
    return _group_heads(_rel_tiles(table, reach, n)), reach


def _bias_c(rpb, rows):
    heads = rpb.shape[0]
    kr = min(C_WIN_ROWS, rows)
    n_row_tiles = rows // C_ROWS_PER_TILE
    pad = GRID_W - C_WIN_COLS
    col_tiles = _toeplitz(jnp.pad(rpb.astype(F32) * LOG2E, ((0, 0), (0, 0), (pad, pad))), GRID_W)
    c = np.arange(GRID_W)[None, :]
    kc = np.arange(GRID_W)[:, None]
    cs = np.clip(c - C_WIN_COLS // 2, 0, GRID_W - C_WIN_COLS)
    col_ok = jnp.asarray((kc >= cs) & (kc < cs + C_WIN_COLS))
    col_tiles = jnp.where(col_ok, col_tiles, NEG_INF)
    masked = jnp.full((heads, GRID_W, GRID_W), NEG_INF, F32)
    tiles = []
    for it in (0, min(1, n_row_tiles - 1), n_row_tiles - 1):
        for off in (-1, 0, 1):
            key_rows = []
            for a in range(C_ROWS_PER_TILE):
                blocks = []
                for b in range(C_ROWS_PER_TILE):
                    r = it * C_ROWS_PER_TILE + b
                    key_r = (it + off) * C_ROWS_PER_TILE + a
                    rs = min(max(r - kr // 2, 0), rows - kr)
                    ok = 0 <= key_r < rows and rs <= key_r < rs + kr
                    blocks.append(col_tiles[:, key_r - r + C_WIN_ROWS - 1] if ok else masked)
                key_rows.append(jnp.concatenate(blocks, axis=2))
            tiles.append(jnp.concatenate(key_rows, axis=1))
    return _group_heads(jnp.stack(tiles))


def _c_tile_index(i, off, nt):
    variant = jnp.where(i == 0, 0, jnp.where(i == nt - 1, 2, 1))
    return variant * 3 + off + 1


def _rope_tables(t_len):
    t = jnp.arange(t_len, dtype=jnp.int32)
    n_freq = HEAD_DIM // 4
    inv_freq = ROPE_THETA ** (-jnp.arange(n_freq, dtype=F32) / n_freq)
    ang = jnp.concatenate([(t // GRID_W).astype(F32)[:, None] * inv_freq[None, :],
                           (t % GRID_W).astype(F32)[:, None] * inv_freq[None, :]], axis=-1)
    cos, sin = jnp.cos(ang), jnp.sin(ang)
    reps = LANES // HEAD_DIM
    return (jnp.tile(jnp.concatenate([cos, cos], axis=-1), (1, reps)),
            jnp.tile(jnp.concatenate([-sin, sin], axis=-1), (1, reps)))


def _deinterleave(w):
    return jnp.swapaxes(w.reshape(w.shape[:-1] + (HEAD_DIM // 2, 2)), -1, -2).reshape(w.shape)


def _swap_halves(w):
    return jnp.concatenate([w[..., HEAD_DIM // 2:], w[..., :HEAD_DIM // 2]], axis=-1)


def _prep_ffn(w_gate, w_up, w_down):
    return w_gate.astype(BF16), w_up.astype(BF16), w_down.astype(BF16)


def _prep_mixer_a(w_in, g_q, g_k):
    d = w_in.shape[0]
    qw = A_HEADS * HEAD_DIM
    kw = A_KV_HEADS * HEAD_DIM
    rep = A_HEADS // A_KV_HEADS
    wq = _deinterleave(w_in[:, :qw].reshape(d, A_HEADS, HEAD_DIM))
    wk = _deinterleave(w_in[:, qw:qw + kw].reshape(d, A_KV_HEADS, 1, HEAD_DIM))
    wk = jnp.broadcast_to(wk, (d, A_KV_HEADS, rep, HEAD_DIM)).reshape(d, A_HEADS, HEAD_DIM)
    wqk = jnp.concatenate([wq, wk], axis=1)
    wv = w_in[:, qw + kw:qw + 2 * kw]
    w = jnp.concatenate([wqk.reshape(d, -1), wv], axis=1).astype(BF16)
    gq, gk = _deinterleave(g_q) * (SCALE * LOG2E), _deinterleave(g_k)
    g = jnp.concatenate([jnp.tile(gq, A_HEADS), jnp.tile(gk, A_HEADS)])
    gsw = jnp.concatenate([jnp.tile(_swap_halves(gq), A_HEADS), jnp.tile(_swap_halves(gk), A_HEADS)])
    return w, g[None, :].astype(F32), gsw[None, :].astype(F32)


def _prep_mixers_bcd(w_in):
    a_cols = (A_HEADS + 2 * A_KV_HEADS) * HEAD_DIM
    w = w_in[:, a_cols:]
    is_q = (np.arange(w.shape[1]) // GROUP_WIDTH) % 3 == 0
    return (w * jnp.asarray(np.where(is_q, SCALE * LOG2E, 1.0), F32)[None, :]).astype(BF16)


def _head_sum_matrix():
    idx = np.arange(MXU_DIM) // HEAD_DIM
    return jnp.asarray(idx[:, None] == idx[None, :], dtype=BF16)


def _trunk(x3, layers, bias_b, far_b, bias_d, d_reach, final_norm):
    batch, t_len, d = x3.shape
    x = x3.reshape(batch * t_len, d)
    cos, sin = _rope_tables(t_len)
    e = _head_sum_matrix()
    depth = len(layers)
    for l, p in enumerate(layers):
        x = _ffn(x, p["ffn1_norm"], *p["ffn1"], final_norm, final=False)
        qa, ka, vta = _aproj(x, p["mix_norm"], p["wa"], e, p["ga"], p["ga_sw"], cos, sin, t_len)
        qk, vtb, vtc, vtd = _proj(x, p["mix_norm"], p["w_bcd"], t_len)

        oa = _attention(qa, ka, vta, t_len=t_len, tile=ATT_T, qcol=0, kcol=0)
        ob = _attention(qk, qk, vtb, t_len=t_len, tile=ATT_T, qcol=0, kcol=GROUPS,
                        bias=bias_b, reach=B_NEAR, bias_index=lambda i, off, nt: off + B_NEAR, far=far_b,
                        diff=(p["b_lambda"], p["b_subln"], 0.8 - 0.6 * math.exp(-0.3 * l)))
        oc = _attention(qk, qk, vtc, t_len=t_len, tile=C_T, qcol=2 * GROUPS, kcol=3 * GROUPS,
                        bias=_bias_c(p["c_rpb"], t_len // GRID_W), reach=1, bias_index=_c_tile_index)
        od = _attention(qk, qk, vtd, t_len=t_len, tile=ATT_T, qcol=4 * GROUPS, kcol=5 * GROUPS,
                        bias=bias_d, reach=d_reach, bias_index=lambda i, off, nt: off + d_reach)
        x = _outproj(x, oa, ob, oc, od, p["w_out"])
        x = _ffn(x, p["ffn2_norm"], *p["ffn2"], final_norm, final=(l == depth - 1))
    return x.reshape(batch, t_len, d)


def kernel(x_prompt, x_sample, ffn1_norm, ffn1_w_gate, ffn1_w_up, ffn1_w_down, mix_norm, w_in, a_q_norm, a_k_norm, b_lambda, b_subln, c_rpb, w_out, ffn2_norm, ffn2_w_gate, ffn2_w_up, ffn2_w_down, t5_table, final_norm):
    depth = w_in.shape[0]
    layers = []
    for l in range(depth):
        wa, ga, ga_sw = _prep_mixer_a(w_in[l], a_q_norm[l], a_k_norm[l])
        layers.append(dict(
            ffn1_norm=ffn1_norm[l][None, :], ffn2_norm=ffn2_norm[l][None, :], mix_norm=mix_norm[l][None, :],
            ffn1=_prep_ffn(ffn1_w_gate[l], ffn1_w_up[l], ffn1_w_down[l]),
            ffn2=_prep_ffn(ffn2_w_gate[l], ffn2_w_up[l], ffn2_w_down[l]),
            wa=wa, ga=ga, ga_sw=ga_sw,
            w_bcd=_prep_mixers_bcd(w_in[l]),
            b_lambda=b_lambda[l].astype(F32), b_subln=b_subln[l][None, :].astype(F32),
            c_rpb=c_rpb[l], w_out=w_out[l].astype(BF16)))
    bias_b, far_b = _bias_b(t5_table[:, :B_HEADS])
    bias_d, d_reach = _bias_d(t5_table[:, B_HEADS:])
    fn = final_norm[None, :]
    return tuple(_trunk(x3, layers, bias_b, far_b, bias_d, d_reach, fn) for x3 in (x_prompt, x_sample))
```

```python
import functools
import math

import jax
import jax.numpy as jnp
import numpy as np
from jax import lax
from jax.experimental import pallas as pl
from jax.experimental.pallas import tpu as pltpu

F32 = jnp.float32
BF16 = jnp.bfloat16

HEAD_DIM = 64
GRID_W = 64
EPS = 1e-6
NEG_INF = -1e30
SCALE = HEAD_DIM ** -0.5
LOG2E = math.log2(math.e)
ROPE_THETA = 10000.0
A_HEADS, A_KV_HEADS = 8, 2
B_HEADS, B_V_DIM = 4, 128
C_HEADS, C_WIN_ROWS, C_WIN_COLS = 8, 8, 16
D_HEADS = 8
D_BRANCHES = ((128, 1), (512, 4), (2048, 16))
T5_BUCKETS, T5_MAX_DIST = 32, 1024
GROUP_WIDTH = 512

LANES = 128
BF16_SUBLANES = 16
MXU_DIM = 256
HEADS_PER_GROUP = MXU_DIM // HEAD_DIM
GROUPS = GROUP_WIDTH // MXU_DIM
VMEM_LIMIT = 56 * 1024 * 1024
FFN_TM, FFN_TF = 1024, 256
PROJ_TM, PROJ_TN = 1024, 1536
APROJ_TM = 512
OUT_TM = 512
ATT_T = 512
C_ROWS_PER_TILE = 4
C_T = C_ROWS_PER_TILE * GRID_W
ONES_ROWS = BF16_SUBLANES
DENSE_UNROLL, BIASED_UNROLL = 4, 2
B_NEAR = 2


def _cparams(sem):
    return pltpu.CompilerParams(dimension_semantics=sem, vmem_limit_bytes=VMEM_LIMIT)


def _rms(x, g):
    return x * lax.rsqrt(jnp.mean(x * x, axis=-1, keepdims=True) + EPS) * g


def _ffn_kernel(x_ref, g_ref, wg_ref, wu_ref, wd_ref, fg_ref, o_ref, xn_ref, *, final):
    j = pl.program_id(1)

    @pl.when(j == 0)
    def _():
        x = x_ref[...]
        xn_ref[...] = _rms(x, g_ref[...]).astype(BF16)
        o_ref[...] = x

    h = jnp.dot(xn_ref[...], jnp.concatenate([wg_ref[...], wu_ref[...]], axis=1), preferred_element_type=F32)
    tf = h.shape[1] // 2
    hg, hu = h[:, :tf], h[:, tf:]
    a = (hg * jax.nn.sigmoid(hg)) * hu
    o_ref[...] += jnp.dot((0.5 * a).astype(BF16), wd_ref[...], preferred_element_type=F32)

    if final:
        @pl.when(j == pl.num_programs(1) - 1)
        def _():
            o_ref[...] = _rms(o_ref[...], fg_ref[...])


def _ffn(x, g, wg, wu, wd, fg, *, final):
    n, d = x.shape
    dff = wd.shape[0]
    tm, tf = min(FFN_TM, n), FFN_TF
    return pl.pallas_call(
        functools.partial(_ffn_kernel, final=final),
        out_shape=jax.ShapeDtypeStruct((n, d), F32),
        grid=(n // tm, dff // tf),
        in_specs=[
            pl.BlockSpec((tm, d), lambda i, j: (i, 0)),
            pl.BlockSpec((1, d), lambda i, j: (0, 0)),
            pl.BlockSpec((d, tf), lambda i, j: (0, j)),
            pl.BlockSpec((d, tf), lambda i, j: (0, j)),
            pl.BlockSpec((tf, d), lambda i, j: (j, 0)),
            pl.BlockSpec((1, d), lambda i, j: (0, 0)),
        ],
        out_specs=pl.BlockSpec((tm, d), lambda i, j: (i, 0)),
        scratch_shapes=[pltpu.VMEM((tm, d), BF16)],
        compiler_params=_cparams(("parallel", "arbitrary")),
        name="ffn",
    )(x, g, wg, wu, wd, fg)


def _store_values_t(vt_ref, v, dv):
    groups, n_tiles, rows, tile = vt_ref.shape
    per_head = dv + ONES_ROWS
    ones = jnp.ones((ONES_ROWS, tile), BF16)
    heads_per_group = rows // per_head
    for t in range(n_tiles):
        vt = v[t * tile:(t + 1) * tile, :].T
        for g in range(groups):
            for h in range(heads_per_group):
                src = (g * heads_per_group + h) * dv
                vt_ref[g, t, h * per_head:h * per_head + dv, :] = vt[src:src + dv, :].astype(BF16)
                vt_ref[g, t, h * per_head + dv:(h + 1) * per_head, :] = ones


def _proj_kernel(x_ref, g_ref, w_ref, o_ref, vtb_ref, vtc_ref, vtd_ref, xn_ref):
    j = pl.program_id(1)

    @pl.when(j == 0)
    def _():
        xn_ref[...] = _rms(x_ref[...], g_ref[...]).astype(BF16)

    y = jnp.dot(xn_ref[...], w_ref[...], preferred_element_type=F32)
    nqk = o_ref.shape[1]
    o_ref[...] = y[:, :nqk].astype(BF16)
    for m, (vt_ref, dv) in enumerate(((vtb_ref, B_V_DIM), (vtc_ref, HEAD_DIM), (vtd_ref, HEAD_DIM))):
        @pl.when(j == m)
        def _():
            _store_values_t(vt_ref, y[:, nqk:], dv)


def _values_shape(batch, t_len, heads, dv, tile):
    return (batch, GROUPS, t_len // tile, heads // GROUPS * (dv + ONES_ROWS), tile)


def _proj(x, g, w, t_len):
    n, d = x.shape
    batch = n // t_len
    tm, tn = math.gcd(PROJ_TM, t_len), PROJ_TN
    npos = t_len // tm
    nqk = 2 * GROUP_WIDTH
    vt_shapes = [_values_shape(batch, t_len, B_HEADS, B_V_DIM, ATT_T),
                 _values_shape(batch, t_len, C_HEADS, HEAD_DIM, C_T),
                 _values_shape(batch, t_len, D_HEADS, HEAD_DIM, ATT_T)]
    vt_specs = [pl.BlockSpec((None, GROUPS, tm // s[4], s[3], s[4]), lambda i, j: (i // npos, 0, i % npos, 0, 0))
                for s in vt_shapes]
    return pl.pallas_call(
        _proj_kernel,
        out_shape=[jax.ShapeDtypeStruct((n, 3 * nqk), BF16)] + [jax.ShapeDtypeStruct(s, BF16) for s in vt_shapes],
        grid=(n // tm, w.shape[1] // tn),
        in_specs=[
            pl.BlockSpec((tm, d), lambda i, j: (i, 0)),
            pl.BlockSpec((1, d), lambda i, j: (0, 0)),
            pl.BlockSpec((d, tn), lambda i, j: (0, j)),
        ],
        out_specs=[pl.BlockSpec((tm, nqk), lambda i, j: (i, j))] + vt_specs,
        scratch_shapes=[pltpu.VMEM((tm, d), BF16)],
        compiler_params=_cparams(("parallel", "arbitrary")),
        name="proj_bcd",
    )(x, g, w)


def _aproj_kernel(x_ref, g_ref, w_ref, e_ref, gc_ref, gs_ref, cos_ref, sin_ref, q_ref, k_ref, vt_ref):
    xn = _rms(x_ref[...], g_ref[...]).astype(BF16)
    y = jnp.dot(xn, w_ref[...], preferred_element_type=F32)
    nqk = gc_ref.shape[1]
    _store_values_t(vt_ref, y[:, nqk:], HEAD_DIM)
    e = e_ref[...]
    cos = jnp.concatenate([cos_ref[...]] * (MXU_DIM // LANES), axis=1)
    sin = jnp.concatenate([sin_ref[...]] * (MXU_DIM // LANES), axis=1)
    first_half = lax.broadcasted_iota(jnp.int32, (y.shape[0], MXU_DIM), 1) % HEAD_DIM < HEAD_DIM // 2
    half = nqk // 2
    for lo in range(0, nqk, MXU_DIM):
        yc = y[:, lo:lo + MXU_DIM]
        partner = jnp.where(first_half, pltpu.roll(yc, MXU_DIM - HEAD_DIM // 2, axis=1),
                            pltpu.roll(yc, HEAD_DIM // 2, axis=1))
        sq = yc * yc
        hi = sq.astype(BF16)
        rest = (sq - hi.astype(F32)).astype(BF16)
        ssum = (jnp.dot(hi, e, preferred_element_type=F32)
                + jnp.dot(rest, e, preferred_element_type=F32))
        r = lax.rsqrt(ssum * (1.0 / HEAD_DIM) + EPS)
        out = (yc * gc_ref[:, lo:lo + MXU_DIM] * cos
               + partner * gs_ref[:, lo:lo + MXU_DIM] * sin) * r
        dst, off = (q_ref, lo) if lo < half else (k_ref, lo - half)
        dst[:, off:off + MXU_DIM] = out.astype(BF16)


def _aproj(x, g, w, e, gc, gs, cos, sin, t_len):
    n, d = x.shape
    tm = math.gcd(APROJ_TM, t_len)
    npos = t_len // tm
    nqk = gc.shape[1]
    const = lambda i: (0, 0)
    vt_shape = _values_shape(n // t_len, t_len, A_KV_HEADS, HEAD_DIM, ATT_T)
    return pl.pallas_call(
        _aproj_kernel,
        out_shape=(jax.ShapeDtypeStruct((n, nqk // 2), BF16),
                   jax.ShapeDtypeStruct((n, nqk // 2), BF16),
                   jax.ShapeDtypeStruct(vt_shape, BF16)),
        grid=(n // tm,),
        in_specs=[
            pl.BlockSpec((tm, d), lambda i: (i, 0)),
            pl.BlockSpec((1, d), const),
            pl.BlockSpec(w.shape, const),
            pl.BlockSpec(e.shape, const),
            pl.BlockSpec(gc.shape, const),
            pl.BlockSpec(gs.shape, const),
            pl.BlockSpec((tm, LANES), lambda i: (i % npos, 0)),
            pl.BlockSpec((tm, LANES), lambda i: (i % npos, 0)),
        ],
        out_specs=(pl.BlockSpec((tm, nqk // 2), lambda i: (i, 0)),
                   pl.BlockSpec((tm, nqk // 2), lambda i: (i, 0)),
                   pl.BlockSpec((None, GROUPS, tm // ATT_T) + vt_shape[3:],
                                lambda i: (i // npos, 0, i % npos, 0, 0))),
        compiler_params=_cparams(("parallel",)),
        name="proj_a",
    )(x, g, w, e, gc, gs, cos, sin)


def _outproj_kernel(x_ref, oa_ref, ob_ref, oc_ref, od_ref, w_ref, o_ref):
    acc = x_ref[...]
    for m, ref in enumerate((oa_ref, ob_ref, oc_ref, od_ref)):
        acc = acc + jnp.dot(ref[...], w_ref[m * GROUP_WIDTH:(m + 1) * GROUP_WIDTH, :],
                            preferred_element_type=F32)
    o_ref[...] = acc


def _outproj(x, oa, ob, oc, od, w):
    n, d = x.shape
    tm = min(OUT_TM, n)
    mix = pl.BlockSpec((tm, GROUP_WIDTH), lambda i: (i, 0))
    return pl.pallas_call(
        _outproj_kernel,
        out_shape=jax.ShapeDtypeStruct((n, d), F32),
        grid=(n // tm,),
        in_specs=[pl.BlockSpec((tm, d), lambda i: (i, 0)), mix, mix, mix, mix,
                  pl.BlockSpec(w.shape, lambda i: (0, 0))],
        out_specs=pl.BlockSpec((tm, d), lambda i: (i, 0)),
        compiler_params=_cparams(("parallel",)),
        name="out_proj",
    )(x, oa, ob, oc, od, w)


def _attn_init(q_ref, qm_ref):
    qt = q_ref[...].astype(F32).T
    unit = lax.broadcasted_iota(jnp.int32, qt.shape, 0) // HEAD_DIM
    for u in range(qm_ref.shape[0]):
        qm_ref[u] = jnp.where(unit == u, qt, 0.0).astype(BF16)


def _scores(k, qm_ref, u):
    return jnp.dot(k, qm_ref[u], preferred_element_type=F32)


def _attn_units(k, qm_ref, update, st0_ref=None, k_next=None, from_st0=True):
    n_units = qm_ref.shape[0]
    st_next = st0_ref[...] if (st0_ref is not None and from_st0) else _scores(k, qm_ref, 0)
    for u in range(n_units):
        st = st_next
        if u + 1 < n_units:
            st_next = _scores(k, qm_ref, u + 1)
        elif st0_ref is not None:
            st0_ref[...] = _scores(k_next, qm_ref, 0)
        update(u, st)


def _tile_first(st, vt, u, m_ref, acc_ref, bias, shift):
    if bias is not None:
        st = st + bias
    m_tile = jnp.max(st, axis=0, keepdims=True)
    m_ref[u] = m_tile if shift is None else m_tile + shift
    acc_ref[u] = jnp.dot(vt, jnp.exp2(st - m_tile).astype(BF16), preferred_element_type=F32)


def _tile_fixed(st, vt, u, m_ref, acc_ref, bias, shift):
    if bias is not None:
        st = st + bias
    sub = m_ref[u] if shift is None else m_ref[u] - shift
    acc_ref[u] += jnp.dot(vt, jnp.exp2(st - sub).astype(BF16), preferred_element_type=F32)


def _tile_online(st, vt, u, m_ref, acc_ref, bias, shift):
    if bias is not None:
        st = st + bias
    smax = jnp.max(st, axis=0, keepdims=True)
    if shift is not None:
        smax = smax + shift
    m_old = m_ref[u]
    m_new = jnp.maximum(m_old, smax)
    sub = m_new if shift is None else m_new - shift
    p = jnp.exp2(st - sub).astype(BF16)
    acc_ref[u] = jnp.exp2(m_old - m_new) * acc_ref[u] + jnp.dot(vt, p, preferred_element_type=F32)
    m_ref[u] = m_new


def _attn_finish(o_ref, acc_ref, dv, diff_refs, lam_init):
    n_units = acc_ref.shape[0]
    outs = [acc_ref[u, :dv, :] / acc_ref[u, dv:dv + 1, :] for u in range(n_units)]
    if diff_refs:
        lam_ref, subg_ref = diff_refs
        lp = lam_ref[...]
        lam = (jnp.exp(jnp.sum(lp[0:1] * lp[1:2], keepdims=True))
               - jnp.exp(jnp.sum(lp[2:3] * lp[3:4], keepdims=True)) + lam_init)
        outs = [outs[2 * h] - lam * outs[2 * h + 1] for h in range(n_units // 2)]
    o = jnp.concatenate(outs, axis=0).T
    if diff_refs:
        o = jnp.concatenate([_rms(o[:, lo:lo + dv], subg_ref[...]) * (1.0 - lam_init)
                             for lo in range(0, o.shape[1], dv)], axis=1)
    o_ref[...] = o.astype(BF16)


def _value_rows(u, n_units, rows_in_block, dv):
    per_head = dv + ONES_ROWS
    heads_in_block = rows_in_block // per_head
    lo = (u * heads_in_block // n_units) * per_head
    return lo, lo + per_head


PLAIN, FAR_LEFT, FAR_RIGHT, NEAR = "plain", 0, 1, "near"


def _attn_kernel(*refs, dv, reach, far, bias_index, lam_init, unroll):
    it = iter(refs)
    q_ref, k_ref, vt_ref = next(it), next(it), next(it)
    bias_ref = next(it) if reach is not None else None
    far_ref = next(it) if far else None
    diff_refs = (next(it), next(it)) if lam_init is not None else None
    o_ref, qm_ref, m_ref, acc_ref, st0_ref = next(it), next(it), next(it), next(it), next(it)

    i = pl.program_id(2)
    n_units = qm_ref.shape[0]
    nk, rows, tile = vt_ref.shape
    bias_heads = bias_ref.shape[1] if bias_ref is not None else 0
    _attn_init(q_ref, qm_ref)
    lo, hi = (0, nk) if reach is None else (jnp.maximum(i - reach, 0), jnp.minimum(i + reach + 1, nk))
    p_first, p_end = (1, nk) if (reach is None or far) else (lo + 1, hi)

    def tile_at(p):
        p = jnp.minimum(p, p_end - 1)
        return jnp.clip(p - (p <= i).astype(jnp.int32), 0, nk - 1)

    def keys(j):
        return k_ref[pl.ds(pl.multiple_of(j * tile, tile), tile), :]

    def updater(tile_fn, j, side):
        def update(u, st):
            r0, r1 = _value_rows(u, n_units, rows, dv)
            h = u * bias_heads // n_units
            bias = bias_ref[bias_index(i, j - i, nk), h] if side == NEAR else None
            shift = far_ref[side * bias_heads + h:side * bias_heads + h + 1, :] if side in (FAR_LEFT, FAR_RIGHT) else None
            tile_fn(st, vt_ref[j, r0:r1, :], u, m_ref, acc_ref, bias, shift)
        return update

    def visit_rest(tile_fn, pipelined):
        def span(p_lo, p_hi, side):
            def visit(p):
                j = tile_at(p)
                if pipelined:
                    _attn_units(keys(j), qm_ref, updater(tile_fn, j, side), st0_ref, keys(tile_at(p + 1)))
                else:
                    _attn_units(keys(j), qm_ref, updater(tile_fn, j, side))

            def single(p, carry):
                visit(p)
                return carry

            if not pipelined:
                lax.fori_loop(p_lo, p_hi, single, 0)
                return

            def group(t, carry):
                for s in range(unroll):
                    visit(p_lo + t * unroll + s)
                return carry
            n_groups = (p_hi - p_lo) // unroll
            lax.fori_loop(0, n_groups, group, 0)
            lax.fori_loop(p_lo + n_groups * unroll, p_hi, single, 0)
        if reach is None:
            span(1, nk, PLAIN)
        elif far:
            span(1, lo + 1, FAR_LEFT)
            span(lo + 1, hi, NEAR)
            span(hi, nk, FAR_RIGHT)
        else:
            span(lo + 1, hi, NEAR)

    diag = PLAIN if reach is None else NEAR
    _attn_units(keys(i), qm_ref, updater(_tile_first, i, diag), st0_ref, keys(tile_at(p_first)), from_st0=False)
    visit_rest(_tile_fixed, pipelined=True)

    finite = jnp.min(jnp.where(jnp.isfinite(acc_ref[...]), 1.0, 0.0))

    @pl.when(finite < 0.5)
    def _():
        m_ref[...] = jnp.full(m_ref.shape, -jnp.inf, F32)
        acc_ref[...] = jnp.zeros(acc_ref.shape, F32)
        _attn_units(keys(i), qm_ref, updater(_tile_online, i, diag))
        visit_rest(_tile_online, pipelined=False)

    _attn_finish(o_ref, acc_ref, dv, diff_refs, lam_init)


def _attention(q, k, vt, *, t_len, tile, qcol, kcol, bias=None, reach=None, bias_index=None, far=None, diff=None,
               unroll=1):
    n = q.shape[0]
    batch, nt = n // t_len, t_len // tile
    dv = B_V_DIM if diff else HEAD_DIM
    rows = vt.shape[3]
    once = pl.Buffered(1)
    in_specs = [
        pl.BlockSpec((tile, MXU_DIM), lambda b, g, i: (b * nt + i, qcol + g)),
        pl.BlockSpec((t_len, MXU_DIM), lambda b, g, i: (b, kcol + g), pipeline_mode=once),
        pl.BlockSpec((None, None) + vt.shape[2:], lambda b, g, i: (b, g, 0, 0, 0), pipeline_mode=once),
    ]
    args = [q, k, vt]
    if bias is not None:
        in_specs.append(pl.BlockSpec((bias.shape[0], None, bias.shape[2], tile, tile),
                                     lambda b, g, i: (0, g, 0, 0, 0), pipeline_mode=once))
        args.append(bias)
    if far is not None:
        in_specs.append(pl.BlockSpec((None,) + far.shape[1:], lambda b, g, i: (g, 0, 0)))
        args.append(far)
    lam_init = None
    if diff:
        lam_p, subg, lam_init = diff
        in_specs += [pl.BlockSpec(lam_p.shape, lambda b, g, i: (0, 0)),
                     pl.BlockSpec(subg.shape, lambda b, g, i: (0, 0))]
        args += [lam_p, subg]
    return pl.pallas_call(
        functools.partial(_attn_kernel, dv=dv, reach=reach, far=far is not None, bias_index=bias_index,
                          lam_init=lam_init, unroll=unroll),
        out_shape=jax.ShapeDtypeStruct((n, GROUP_WIDTH), BF16),
        grid=(batch, GROUPS, nt),
        in_specs=in_specs,
        out_specs=pl.BlockSpec((tile, MXU_DIM), lambda b, g, i: (b * nt + i, g)),
        scratch_shapes=[pltpu.VMEM((HEADS_PER_GROUP, MXU_DIM, tile), BF16),
                        pltpu.VMEM((HEADS_PER_GROUP, 1, tile), F32),
                        pltpu.VMEM((HEADS_PER_GROUP, dv + ONES_ROWS, tile), F32),
                        pltpu.VMEM((vt.shape[4], tile), F32)],
        compiler_params=_cparams(("parallel", "parallel", "arbitrary")),
        name="attention",
    )(*args)


def _t5_bucket(rel):
    nb = T5_BUCKETS // 2
    max_exact = nb // 2
    side = (rel > 0).astype(jnp.int32) * nb
    n = jnp.abs(rel)
    large = max_exact + (jnp.log(jnp.maximum(n, 1).astype(F32) / max_exact)
                         / math.log(T5_MAX_DIST / max_exact) * (nb - max_exact)).astype(jnp.int32)
    large = jnp.minimum(large, nb - 1)
    return side + jnp.where(n < max_exact, n, large)


def _skew(v, n):
    lead = v.shape[:-1]
    r = jnp.concatenate([v[..., ::-1], jnp.zeros(lead + (1,), v.dtype)], axis=-1)
    x = jnp.broadcast_to(r[..., None, :], lead + (n, r.shape[-1]))
    k = np.arange(n)[:, None]
    for b in range(n.bit_length() - 1):
        x = jnp.where(jnp.asarray((k >> b) & 1 == 1), jnp.roll(x, 1 << b, axis=-1), x)
    return x


def _toeplitz(v, n):
    return _skew(v, n)[..., n - 1:2 * n - 1]


def _rel_tiles(table, reach, n):
    x = _skew(table, n)
    return jnp.stack([x[..., (2 * reach + 1 - o) * n - 1:(2 * reach + 2 - o) * n - 1]
                      for o in range(2 * reach + 1)])


def _group_heads(b):
    t, h = b.shape[:2]
    return b.reshape(t, GROUPS, h // GROUPS, *b.shape[2:])


def _t5_saturation():
    nb = T5_BUCKETS // 2
    max_exact = nb // 2
    return math.ceil(max_exact * (T5_MAX_DIST / max_exact) ** ((nb - 1 - max_exact) / (nb - max_exact))) + 1


def _bias_b(t5_b):
    n, reach = ATT_T, B_NEAR
    assert reach * n + 1 >= _t5_saturation()
    rel = jnp.arange(-(reach * n + n - 1), reach * n + n, dtype=jnp.int32)
    table = (t5_b[_t5_bucket(rel)].astype(F32) * LOG2E).T
    far = (t5_b[_t5_bucket(jnp.array([-T5_MAX_DIST * 2, T5_MAX_DIST * 2], jnp.int32))].astype(F32) * LOG2E)
    heads = t5_b.shape[1]
    far = far.reshape(2, GROUPS, heads // GROUPS).transpose(1, 0, 2).reshape(GROUPS, 2 * heads // GROUPS, 1)
    return _group_heads(_rel_tiles(table, reach, n)), jnp.broadcast_to(far, far.shape[:2] + (n,))


def _bias_d(t5_d):
    n = ATT_T
    reach = max(w // 2 for w, _ in D_BRANCHES) // n
    rel = jnp.arange(-(reach * n + n - 1), reach * n + n, dtype=jnp.int32)
    mult = sum(((jnp.abs(rel) <= w // 2) & (rel % d == 0)).astype(F32) for w, d in D_BRANCHES)
    b = t5_d[_t5_bucket(rel)].astype(F32).T
    table = jnp.where(mult[None] > 0, (b + jnp.log(jnp.maximum(mult, 1.0))[None]) * LOG2E, NEG_INF)
    return _group_heads(_rel_tiles(table, reach, n)), reach


def _bias_c(rpb, rows):
    heads = rpb.shape[0]
    kr = min(C_WIN_ROWS, rows)
    n_row_tiles = rows // C_ROWS_PER_TILE
    pad = GRID_W - C_WIN_COLS
    col_tiles = _toeplitz(jnp.pad(rpb.astype(F32) * LOG2E, ((0, 0), (0, 0), (pad, pad))), GRID_W)
    c = np.arange(GRID_W)[None, :]
    kc = np.arange(GRID_W)[:, None]
    cs = np.clip(c - C_WIN_COLS // 2, 0, GRID_W - C_WIN_COLS)
    col_ok = jnp.asarray((kc >= cs) & (kc < cs + C_WIN_COLS))
    col_tiles = jnp.where(col_ok, col_tiles, NEG_INF)
    masked = jnp.full((heads, GRID_W, GRID_W), NEG_INF, F32)
    tiles = []
    for it in (0, min(1, n_row_tiles - 1), n_row_tiles - 1):
        for off in (-1, 0, 1):
            key_rows = []
            for a in range(C_ROWS_PER_TILE):
                blocks = []
                for b in range(C_ROWS_PER_TILE):
                    r = it * C_ROWS_PER_TILE + b
                    key_r = (it + off) * C_ROWS_PER_TILE + a
                    rs = min(max(r - kr // 2, 0), rows - kr)
                    ok = 0 <= key_r < rows and rs <= key_r < rs + kr
                    blocks.append(col_tiles[:, key_r - r + C_WIN_ROWS - 1] if ok else masked)
                key_rows.append(jnp.concatenate(blocks, axis=2))
            tiles.append(jnp.concatenate(key_rows, axis=1))
    return _group_heads(jnp.stack(tiles))


def _c_tile_index(i, off, nt):
    variant = jnp.where(i == 0, 0, jnp.where(i == nt - 1, 2, 1))
    return variant * 3 + off + 1


def _rope_tables(t_len):
    t = jnp.arange(t_len, dtype=jnp.int32)
    n_freq = HEAD_DIM // 4
    inv_freq = ROPE_THETA ** (-jnp.arange(n_freq, dtype=F32) / n_freq)
    ang = jnp.concatenate([(t // GRID_W).astype(F32)[:, None] * inv_freq[None, :],
                           (t % GRID_W).astype(F32)[:, None] * inv_freq[None, :]], axis=-1)
    cos, sin = jnp.cos(ang), jnp.sin(ang)
    reps = LANES // HEAD_DIM
    return (jnp.tile(jnp.concatenate([cos, cos], axis=-1), (1, reps)),
            jnp.tile(jnp.concatenate([-sin, sin], axis=-1), (1, reps)))


def _deinterleave(w):
    return jnp.swapaxes(w.reshape(w.shape[:-1] + (HEAD_DIM // 2, 2)), -1, -2).reshape(w.shape)


def _swap_halves(w):
    return jnp.concatenate([w[..., HEAD_DIM // 2:], w[..., :HEAD_DIM // 2]], axis=-1)


def _prep_ffn(w_gate, w_up, w_down):
    return w_gate.astype(BF16), w_up.astype(BF16), w_down.astype(BF16)


def _prep_mixer_a(w_in, g_q, g_k):
    d = w_in.shape[0]
    qw = A_HEADS * HEAD_DIM
    kw = A_KV_HEADS * HEAD_DIM
    rep = A_HEADS // A_KV_HEADS
    wq = _deinterleave(w_in[:, :qw].reshape(d, A_HEADS, HEAD_DIM))
    wk = _deinterleave(w_in[:, qw:qw + kw].reshape(d, A_KV_HEADS, 1, HEAD_DIM))
    wk = jnp.broadcast_to(wk, (d, A_KV_HEADS, rep, HEAD_DIM)).reshape(d, A_HEADS, HEAD_DIM)
    wqk = jnp.concatenate([wq, wk], axis=1)
    wv = w_in[:, qw + kw:qw + 2 * kw]
    w = jnp.concatenate([wqk.reshape(d, -1), wv], axis=1).astype(BF16)
    gq, gk = _deinterleave(g_q) * (SCALE * LOG2E), _deinterleave(g_k)
    g = jnp.concatenate([jnp.tile(gq, A_HEADS), jnp.tile(gk, A_HEADS)])
    gsw = jnp.concatenate([jnp.tile(_swap_halves(gq), A_HEADS), jnp.tile(_swap_halves(gk), A_HEADS)])
    return w, g[None, :].astype(F32), gsw[None, :].astype(F32)


def _prep_mixers_bcd(w_in):
    a_cols = (A_HEADS + 2 * A_KV_HEADS) * HEAD_DIM
    w = w_in[:, a_cols:]
    is_q = (np.arange(w.shape[1]) // GROUP_WIDTH) % 3 == 0
    return (w * jnp.asarray(np.where(is_q, SCALE * LOG2E, 1.0), F32)[None, :]).astype(BF16)


def _head_sum_matrix():
    idx = np.arange(MXU_DIM) // HEAD_DIM
    return jnp.asarray(idx[:, None] == idx[None, :], dtype=BF16)


def _trunk(x3, layers, bias_b, far_b, bias_d, d_reach, final_norm):
    batch, t_len, d = x3.shape
    x = x3.reshape(batch * t_len, d)
    cos, sin = _rope_tables(t_len)
    e = _head_sum_matrix()
    depth = len(layers)
    for l, p in enumerate(layers):
        x = _ffn(x, p["ffn1_norm"], *p["ffn1"], final_norm, final=False)
        qa, ka, vta = _aproj(x, p["mix_norm"], p["wa"], e, p["ga"], p["ga_sw"], cos, sin, t_len)
        qk, vtb, vtc, vtd = _proj(x, p["mix_norm"], p["w_bcd"], t_len)

        oa = _attention(qa, ka, vta, t_len=t_len, tile=ATT_T, qcol=0, kcol=0, unroll=DENSE_UNROLL)
        ob = _attention(qk, qk, vtb, t_len=t_len, tile=ATT_T, qcol=0, kcol=GROUPS,
                        bias=bias_b, reach=B_NEAR, bias_index=lambda i, off, nt: off + B_NEAR, far=far_b,
                        diff=(p["b_lambda"], p["b_subln"], 0.8 - 0.6 * math.exp(-0.3 * l)), unroll=BIASED_UNROLL)
        oc = _attention(qk, qk, vtc, t_len=t_len, tile=C_T, qcol=2 * GROUPS, kcol=3 * GROUPS,
                        bias=_bias_c(p["c_rpb"], t_len // GRID_W), reach=1, bias_index=_c_tile_index, unroll=BIASED_UNROLL)
        od = _attention(qk, qk, vtd, t_len=t_len, tile=ATT_T, qcol=4 * GROUPS, kcol=5 * GROUPS,
                        bias=bias_d, reach=d_reach, bias_index=lambda i, off, nt: off + d_reach, unroll=BIASED_UNROLL)
        x = _outproj(x, oa, ob, oc, od, p["w_out"])
        x = _ffn(x, p["ffn2_norm"], *p["ffn2"], final_norm, final=(l == depth - 1))
    return x.reshape(batch, t_len, d)


def kernel(x_prompt, x_sample, ffn1_norm, ffn1_w_gate, ffn1_w_up, ffn1_w_down, mix_norm, w_in, a_q_norm, a_k_norm, b_lambda, b_subln, c_rpb, w_out, ffn2_norm, ffn2_w_gate, ffn2_w_up, ffn2_w_down, t5_table, final_norm):
    depth = w_in.shape[0]
    layers = []
    for l in range(depth):
        wa, ga, ga_sw = _prep_mixer_a(w_in[l], a_q_norm[l], a_k_norm[l])
        layers.append(dict(
            ffn1_norm=ffn1_norm[l][None, :], ffn2_norm=ffn2_norm[l][None, :], mix_norm=mix_norm[l][None, :],
            ffn1=_prep_ffn(ffn1_w_gate[l], ffn1_w_up[l], ffn1_w_down[l]),
            ffn2=_prep_ffn(ffn2_w_gate[l], ffn2_w_up[l], ffn2_w_down[l]),
            wa=wa, ga=ga, ga_sw=ga_sw,
            w_bcd=_prep_mixers_bcd(w_in[l]),
            b_lambda=b_lambda[l].astype(F32), b_subln=b_subln[l][None, :].astype(F32),
            c_rpb=c_rpb[l], w_out=w_out[l].astype(BF16)))
    bias_b, far_b = _bias_b(t5_table[:, :B_HEADS])
    bias_d, d_reach = _bias_d(t5_table[:, B_HEADS:])
    fn = final_norm[None, :]
    return tuple(_trunk(x3, layers, bias_b, far_b, bias_d, d_reach, fn) for x3 in (x_prompt, x_sample))
```

```python
import functools
import math

import jax
import jax.numpy as jnp
import numpy as np
from jax import lax
from jax.experimental import pallas as pl
from jax.experimental.pallas import tpu as pltpu

F32 = jnp.float32
BF16 = jnp.bfloat16

HEAD_DIM = 64
GRID_W = 64
EPS = 1e-6
NEG_INF = -1e30
SCALE = HEAD_DIM ** -0.5
LOG2E = math.log2(math.e)
ROPE_THETA = 10000.0
A_HEADS, A_KV_HEADS = 8, 2
B_HEADS, B_V_DIM = 4, 128
C_HEADS, C_WIN_ROWS, C_WIN_COLS = 8, 8, 16
D_HEADS = 8
D_BRANCHES = ((128, 1), (512, 4), (2048, 16))
T5_BUCKETS, T5_MAX_DIST = 32, 1024
GROUP_WIDTH = 512

LANES = 128
BF16_SUBLANES = 16
MXU_DIM = 256
HEADS_PER_GROUP = MXU_DIM // HEAD_DIM
GROUPS = GROUP_WIDTH // MXU_DIM
VMEM_LIMIT = 56 * 1024 * 1024
FFN_TM, FFN_TF = 1024, 256
PROJ_TM, PROJ_TN = 1024, 1536
APROJ_TM = 512
OUT_TM = 512
ATT_T = 512
C_ROWS_PER_TILE = 4
C_T = C_ROWS_PER_TILE * GRID_W
ONES_ROWS = BF16_SUBLANES
DENSE_UNROLL, BIASED_UNROLL = 4, 4
B_NEAR = 2


def _cparams(sem):
    return pltpu.CompilerParams(dimension_semantics=sem, vmem_limit_bytes=VMEM_LIMIT)


def _rms(x, g):
    return x * lax.rsqrt(jnp.mean(x * x, axis=-1, keepdims=True) + EPS) * g


def _ffn_kernel(x_ref, g_ref, wg_ref, wu_ref, wd_ref, fg_ref, o_ref, xn_ref, *, final):
    j = pl.program_id(1)

    @pl.when(j == 0)
    def _():
        x = x_ref[...]
        xn_ref[...] = _rms(x, g_ref[...]).astype(BF16)
        o_ref[...] = x

    h = jnp.dot(xn_ref[...], jnp.concatenate([wg_ref[...], wu_ref[...]], axis=1), preferred_element_type=F32)
    tf = h.shape[1] // 2
    hg, hu = h[:, :tf], h[:, tf:]
    a = (hg * jax.nn.sigmoid(hg)) * hu
    o_ref[...] += jnp.dot((0.5 * a).astype(BF16), wd_ref[...], preferred_element_type=F32)

    if final:
        @pl.when(j == pl.num_programs(1) - 1)
        def _():
            o_ref[...] = _rms(o_ref[...], fg_ref[...])


def _ffn(x, g, wg, wu, wd, fg, *, final):
    n, d = x.shape
    dff = wd.shape[0]
    tm, tf = min(FFN_TM, n), FFN_TF
    return pl.pallas_call(
        functools.partial(_ffn_kernel, final=final),
        out_shape=jax.ShapeDtypeStruct((n, d), F32),
        grid=(n // tm, dff // tf),
        in_specs=[
            pl.BlockSpec((tm, d), lambda i, j: (i, 0)),
            pl.BlockSpec((1, d), lambda i, j: (0, 0)),
            pl.BlockSpec((d, tf), lambda i, j: (0, j)),
            pl.BlockSpec((d, tf), lambda i, j: (0, j)),
            pl.BlockSpec((tf, d), lambda i, j: (j, 0)),
            pl.BlockSpec((1, d), lambda i, j: (0, 0)),
        ],
        out_specs=pl.BlockSpec((tm, d), lambda i, j: (i, 0)),
        scratch_shapes=[pltpu.VMEM((tm, d), BF16)],
        compiler_params=_cparams(("parallel", "arbitrary")),
        name="ffn",
    )(x, g, wg, wu, wd, fg)


def _store_values_t(vt_ref, v, dv):
    groups, n_tiles, rows, tile = vt_ref.shape
    per_head = dv + ONES_ROWS
    ones = jnp.ones((ONES_ROWS, tile), BF16)
    heads_per_group = rows // per_head
    for t in range(n_tiles):
        vt = v[t * tile:(t + 1) * tile, :].T
        for g in range(groups):
            for h in range(heads_per_group):
                src = (g * heads_per_group + h) * dv
                vt_ref[g, t, h * per_head:h * per_head + dv, :] = vt[src:src + dv, :].astype(BF16)
                vt_ref[g, t, h * per_head + dv:(h + 1) * per_head, :] = ones


def _proj_kernel(x_ref, g_ref, w_ref, o_ref, vtb_ref, vtc_ref, vtd_ref, xn_ref):
    j = pl.program_id(1)

    @pl.when(j == 0)
    def _():
        xn_ref[...] = _rms(x_ref[...], g_ref[...]).astype(BF16)

    y = jnp.dot(xn_ref[...], w_ref[...], preferred_element_type=F32)
    nqk = o_ref.shape[1]
    o_ref[...] = y[:, :nqk].astype(BF16)
    for m, (vt_ref, dv) in enumerate(((vtb_ref, B_V_DIM), (vtc_ref, HEAD_DIM), (vtd_ref, HEAD_DIM))):
        @pl.when(j == m)
        def _():
            _store_values_t(vt_ref, y[:, nqk:], dv)


def _values_shape(batch, t_len, heads, dv, tile):
    return (batch, GROUPS, t_len // tile, heads // GROUPS * (dv + ONES_ROWS), tile)


def _proj(x, g, w, t_len):
    n, d = x.shape
    batch = n // t_len
    tm, tn = math.gcd(PROJ_TM, t_len), PROJ_TN
    npos = t_len // tm
    nqk = 2 * GROUP_WIDTH
    vt_shapes = [_values_shape(batch, t_len, B_HEADS, B_V_DIM, ATT_T),
                 _values_shape(batch, t_len, C_HEADS, HEAD_DIM, C_T),
                 _values_shape(batch, t_len, D_HEADS, HEAD_DIM, ATT_T)]
    vt_specs = [pl.BlockSpec((None, GROUPS, tm // s[4], s[3], s[4]), lambda i, j: (i // npos, 0, i % npos, 0, 0))
                for s in vt_shapes]
    return pl.pallas_call(
        _proj_kernel,
        out_shape=[jax.ShapeDtypeStruct((n, 3 * nqk), BF16)] + [jax.ShapeDtypeStruct(s, BF16) for s in vt_shapes],
        grid=(n // tm, w.shape[1] // tn),
        in_specs=[
            pl.BlockSpec((tm, d), lambda i, j: (i, 0)),
            pl.BlockSpec((1, d), lambda i, j: (0, 0)),
            pl.BlockSpec((d, tn), lambda i, j: (0, j)),
        ],
        out_specs=[pl.BlockSpec((tm, nqk), lambda i, j: (i, j))] + vt_specs,
        scratch_shapes=[pltpu.VMEM((tm, d), BF16)],
        compiler_params=_cparams(("parallel", "arbitrary")),
        name="proj_bcd",
    )(x, g, w)


def _aproj_kernel(x_ref, g_ref, w_ref, e_ref, gc_ref, gs_ref, cos_ref, sin_ref, q_ref, k_ref, vt_ref):
    xn = _rms(x_ref[...], g_ref[...]).astype(BF16)
    y = jnp.dot(xn, w_ref[...], preferred_element_type=F32)
    nqk = gc_ref.shape[1]
    _store_values_t(vt_ref, y[:, nqk:], HEAD_DIM)
    e = e_ref[...]
    cos = jnp.concatenate([cos_ref[...]] * (MXU_DIM // LANES), axis=1)
    sin = jnp.concatenate([sin_ref[...]] * (MXU_DIM // LANES), axis=1)
    first_half = lax.broadcasted_iota(jnp.int32, (y.shape[0], MXU_DIM), 1) % HEAD_DIM < HEAD_DIM // 2
    half = nqk // 2
    for lo in range(0, nqk, MXU_DIM):
        yc = y[:, lo:lo + MXU_DIM]
        partner = jnp.where(first_half, pltpu.roll(yc, MXU_DIM - HEAD_DIM // 2, axis=1),
                            pltpu.roll(yc, HEAD_DIM // 2, axis=1))
        sq = yc * yc
        hi = sq.astype(BF16)
        rest = (sq - hi.astype(F32)).astype(BF16)
        ssum = (jnp.dot(hi, e, preferred_element_type=F32)
                + jnp.dot(rest, e, preferred_element_type=F32))
        r = lax.rsqrt(ssum * (1.0 / HEAD_DIM) + EPS)
        out = (yc * gc_ref[:, lo:lo + MXU_DIM] * cos
               + partner * gs_ref[:, lo:lo + MXU_DIM] * sin) * r
        dst, off = (q_ref, lo) if lo < half else (k_ref, lo - half)
        dst[:, off:off + MXU_DIM] = out.astype(BF16)


def _aproj(x, g, w, e, gc, gs, cos, sin, t_len):
    n, d = x.shape
    tm = math.gcd(APROJ_TM, t_len)
    npos = t_len // tm
    nqk = gc.shape[1]
    const = lambda i: (0, 0)
    vt_shape = _values_shape(n // t_len, t_len, A_KV_HEADS, HEAD_DIM, ATT_T)
    return pl.pallas_call(
        _aproj_kernel,
        out_shape=(jax.ShapeDtypeStruct((n, nqk // 2), BF16),
                   jax.ShapeDtypeStruct((n, nqk // 2), BF16),
                   jax.ShapeDtypeStruct(vt_shape, BF16)),
        grid=(n // tm,),
        in_specs=[
            pl.BlockSpec((tm, d), lambda i: (i, 0)),
            pl.BlockSpec((1, d), const),
            pl.BlockSpec(w.shape, const),
            pl.BlockSpec(e.shape, const),
            pl.BlockSpec(gc.shape, const),
            pl.BlockSpec(gs.shape, const),
            pl.BlockSpec((tm, LANES), lambda i: (i % npos, 0)),
            pl.BlockSpec((tm, LANES), lambda i: (i % npos, 0)),
        ],
        out_specs=(pl.BlockSpec((tm, nqk // 2), lambda i: (i, 0)),
                   pl.BlockSpec((tm, nqk // 2), lambda i: (i, 0)),
                   pl.BlockSpec((None, GROUPS, tm // ATT_T) + vt_shape[3:],
                                lambda i: (i // npos, 0, i % npos, 0, 0))),
        compiler_params=_cparams(("parallel",)),
        name="proj_a",
    )(x, g, w, e, gc, gs, cos, sin)


def _outproj_kernel(x_ref, oa_ref, ob_ref, oc_ref, od_ref, w_ref, o_ref):
    acc = x_ref[...]
    for m, ref in enumerate((oa_ref, ob_ref, oc_ref, od_ref)):
        acc = acc + jnp.dot(ref[...], w_ref[m * GROUP_WIDTH:(m + 1) * GROUP_WIDTH, :],
                            preferred_element_type=F32)
    o_ref[...] = acc


def _outproj(x, oa, ob, oc, od, w):
    n, d = x.shape
    tm = min(OUT_TM, n)
    mix = pl.BlockSpec((tm, GROUP_WIDTH), lambda i: (i, 0))
    return pl.pallas_call(
        _outproj_kernel,
        out_shape=jax.ShapeDtypeStruct((n, d), F32),
        grid=(n // tm,),
        in_specs=[pl.BlockSpec((tm, d), lambda i: (i, 0)), mix, mix, mix, mix,
                  pl.BlockSpec(w.shape, lambda i: (0, 0))],
        out_specs=pl.BlockSpec((tm, d), lambda i: (i, 0)),
        compiler_params=_cparams(("parallel",)),
        name="out_proj",
    )(x, oa, ob, oc, od, w)


def _attn_init(q_ref, qm_ref):
    qt = q_ref[...].astype(F32).T
    unit = lax.broadcasted_iota(jnp.int32, qt.shape, 0) // HEAD_DIM
    for u in range(qm_ref.shape[0]):
        qm_ref[u] = jnp.where(unit == u, qt, 0.0).astype(BF16)


def _scores(k, qm_ref, u):
    return jnp.dot(k, qm_ref[u], preferred_element_type=F32)


def _attn_units(k, qm_ref, update, st0_ref=None, k_next=None, from_st0=True):
    n_units = qm_ref.shape[0]
    st_next = st0_ref[...] if (st0_ref is not None and from_st0) else _scores(k, qm_ref, 0)
    for u in range(n_units):
        st = st_next
        if u + 1 < n_units:
            st_next = _scores(k, qm_ref, u + 1)
        elif st0_ref is not None:
            st0_ref[...] = _scores(k_next, qm_ref, 0)
        update(u, st)


def _tile_first(st, vt, u, m_ref, acc_ref, bias, shift):
    if bias is not None:
        st = st + bias
    m_tile = jnp.max(st, axis=0, keepdims=True)
    m_ref[u] = m_tile if shift is None else m_tile + shift
    acc_ref[u] = jnp.dot(vt, jnp.exp2(st - m_tile).astype(BF16), preferred_element_type=F32)


def _tile_fixed(st, vt, u, m_ref, acc_ref, bias, shift):
    if bias is not None:
        st = st + bias
    sub = m_ref[u] if shift is None else m_ref[u] - shift
    acc_ref[u] += jnp.dot(vt, jnp.exp2(st - sub).astype(BF16), preferred_element_type=F32)


def _tile_online(st, vt, u, m_ref, acc_ref, bias, shift):
    if bias is not None:
        st = st + bias
    smax = jnp.max(st, axis=0, keepdims=True)
    if shift is not None:
        smax = smax + shift
    m_old = m_ref[u]
    m_new = jnp.maximum(m_old, smax)
    sub = m_new if shift is None else m_new - shift
    p = jnp.exp2(st - sub).astype(BF16)
    acc_ref[u] = jnp.exp2(m_old - m_new) * acc_ref[u] + jnp.dot(vt, p, preferred_element_type=F32)
    m_ref[u] = m_new


def _attn_finish(o_ref, acc_ref, dv, diff_refs, lam_init):
    n_units = acc_ref.shape[0]
    outs = [acc_ref[u, :dv, :] / acc_ref[u, dv:dv + 1, :] for u in range(n_units)]
    if diff_refs:
        lam_ref, subg_ref = diff_refs
        lp = lam_ref[...]
        lam = (jnp.exp(jnp.sum(lp[0:1] * lp[1:2], keepdims=True))
               - jnp.exp(jnp.sum(lp[2:3] * lp[3:4], keepdims=True)) + lam_init)
        outs = [outs[2 * h] - lam * outs[2 * h + 1] for h in range(n_units // 2)]
    o = jnp.concatenate(outs, axis=0).T
    if diff_refs:
        o = jnp.concatenate([_rms(o[:, lo:lo + dv], subg_ref[...]) * (1.0 - lam_init)
                             for lo in range(0, o.shape[1], dv)], axis=1)
    o_ref[...] = o.astype(BF16)


def _value_rows(u, n_units, rows_in_block, dv):
    per_head = dv + ONES_ROWS
    heads_in_block = rows_in_block // per_head
    lo = (u * heads_in_block // n_units) * per_head
    return lo, lo + per_head


PLAIN, FAR_LEFT, FAR_RIGHT, NEAR = "plain", 0, 1, "near"


def _attn_kernel(*refs, dv, reach, far, bias_index, lam_init, unroll):
    it = iter(refs)
    q_ref, k_ref, vt_ref = next(it), next(it), next(it)
    bias_ref = next(it) if reach is not None else None
    far_ref = next(it) if far else None
    diff_refs = (next(it), next(it)) if lam_init is not None else None
    o_ref, qm_ref, m_ref, acc_ref, st0_ref = next(it), next(it), next(it), next(it), next(it)

    i = pl.program_id(2)
    n_units = qm_ref.shape[0]
    nk, rows, tile = vt_ref.shape
    bias_heads = bias_ref.shape[1] if bias_ref is not None else 0
    _attn_init(q_ref, qm_ref)
    lo, hi = (0, nk) if reach is None else (jnp.maximum(i - reach, 0), jnp.minimum(i + reach + 1, nk))
    p_first, p_end = (1, nk) if (reach is None or far) else (lo + 1, hi)

    def tile_at(p):
        p = jnp.minimum(p, p_end - 1)
        return jnp.clip(p - (p <= i).astype(jnp.int32), 0, nk - 1)

    def keys(j):
        return k_ref[pl.ds(pl.multiple_of(j * tile, tile), tile), :]

    def updater(tile_fn, j, side):
        def update(u, st):
            r0, r1 = _value_rows(u, n_units, rows, dv)
            h = u * bias_heads // n_units
            bias = bias_ref[bias_index(i, j - i, nk), h] if side == NEAR else None
            shift = far_ref[side * bias_heads + h:side * bias_heads + h + 1, :] if side in (FAR_LEFT, FAR_RIGHT) else None
            tile_fn(st, vt_ref[j, r0:r1, :], u, m_ref, acc_ref, bias, shift)
        return update

    def visit_rest(tile_fn, pipelined):
        def span(p_lo, p_hi, side):
            def visit(p):
                j = tile_at(p)
                if pipelined:
                    _attn_units(keys(j), qm_ref, updater(tile_fn, j, side), st0_ref, keys(tile_at(p + 1)))
                else:
                    _attn_units(keys(j), qm_ref, updater(tile_fn, j, side))

            def single(p, carry):
                visit(p)
                return carry

            if not pipelined:
                lax.fori_loop(p_lo, p_hi, single, 0)
                return

            def group(t, carry):
                for s in range(unroll):
                    visit(p_lo + t * unroll + s)
                return carry
            n_groups = (p_hi - p_lo) // unroll
            lax.fori_loop(0, n_groups, group, 0)
            lax.fori_loop(p_lo + n_groups * unroll, p_hi, single, 0)
        if reach is None:
            span(1, nk, PLAIN)
        elif far:
            span(1, lo + 1, FAR_LEFT)
            span(lo + 1, hi, NEAR)
            span(hi, nk, FAR_RIGHT)
        else:
            span(lo + 1, hi, NEAR)

    diag = PLAIN if reach is None else NEAR
    _attn_units(keys(i), qm_ref, updater(_tile_first, i, diag), st0_ref, keys(tile_at(p_first)), from_st0=False)
    visit_rest(_tile_fixed, pipelined=True)

    finite = jnp.min(jnp.where(jnp.isfinite(acc_ref[...]), 1.0, 0.0))

    @pl.when(finite < 0.5)
    def _():
        m_ref[...] = jnp.full(m_ref.shape, -jnp.inf, F32)
        acc_ref[...] = jnp.zeros(acc_ref.shape, F32)
        _attn_units(keys(i), qm_ref, updater(_tile_online, i, diag))
        visit_rest(_tile_online, pipelined=False)

    _attn_finish(o_ref, acc_ref, dv, diff_refs, lam_init)


def _attention(q, k, vt, *, t_len, tile, qcol, kcol, bias=None, reach=None, bias_index=None, far=None, diff=None,
               unroll=1):
    n = q.shape[0]
    batch, nt = n // t_len, t_len // tile
    dv = B_V_DIM if diff else HEAD_DIM
    rows = vt.shape[3]
    once = pl.Buffered(1)
    in_specs = [
        pl.BlockSpec((tile, MXU_DIM), lambda b, g, i: (b * nt + i, qcol + g)),
        pl.BlockSpec((t_len, MXU_DIM), lambda b, g, i: (b, kcol + g), pipeline_mode=once),
        pl.BlockSpec((None, None) + vt.shape[2:], lambda b, g, i: (b, g, 0, 0, 0), pipeline_mode=once),
    ]
    args = [q, k, vt]
    if bias is not None:
        in_specs.append(pl.BlockSpec((bias.shape[0], None, bias.shape[2], tile, tile),
                                     lambda b, g, i: (0, g, 0, 0, 0), pipeline_mode=once))
        args.append(bias)
    if far is not None:
        in_specs.append(pl.BlockSpec((None,) + far.shape[1:], lambda b, g, i: (g, 0, 0)))
        args.append(far)
    lam_init = None
    if diff:
        lam_p, subg, lam_init = diff
        in_specs += [pl.BlockSpec(lam_p.shape, lambda b, g, i: (0, 0)),
                     pl.BlockSpec(subg.shape, lambda b, g, i: (0, 0))]
        args += [lam_p, subg]
    return pl.pallas_call(
        functools.partial(_attn_kernel, dv=dv, reach=reach, far=far is not None, bias_index=bias_index,
                          lam_init=lam_init, unroll=unroll),
        out_shape=jax.ShapeDtypeStruct((n, GROUP_WIDTH), BF16),
        grid=(batch, GROUPS, nt),
        in_specs=in_specs,
        out_specs=pl.BlockSpec((tile, MXU_DIM), lambda b, g, i: (b * nt + i, g)),
        scratch_shapes=[pltpu.VMEM((HEADS_PER_GROUP, MXU_DIM, tile), BF16),
                        pltpu.VMEM((HEADS_PER_GROUP, 1, tile), F32),
                        pltpu.VMEM((HEADS_PER_GROUP, dv + ONES_ROWS, tile), F32),
                        pltpu.VMEM((vt.shape[4], tile), F32)],
        compiler_params=_cparams(("parallel", "parallel", "arbitrary")),
        name="attention",
    )(*args)


def _t5_bucket(rel):
    nb = T5_BUCKETS // 2
    max_exact = nb // 2
    side = (rel > 0).astype(jnp.int32) * nb
    n = jnp.abs(rel)
    large = max_exact + (jnp.log(jnp.maximum(n, 1).astype(F32) / max_exact)
                         / math.log(T5_MAX_DIST / max_exact) * (nb - max_exact)).astype(jnp.int32)
    large = jnp.minimum(large, nb - 1)
    return side + jnp.where(n < max_exact, n, large)


def _skew(v, n):
    lead = v.shape[:-1]
    r = jnp.concatenate([v[..., ::-1], jnp.zeros(lead + (1,), v.dtype)], axis=-1)
    x = jnp.broadcast_to(r[..., None, :], lead + (n, r.shape[-1]))
    k = np.arange(n)[:, None]
    for b in range(n.bit_length() - 1):
        x = jnp.where(jnp.asarray((k >> b) & 1 == 1), jnp.roll(x, 1 << b, axis=-1), x)
    return x


def _toeplitz(v, n):
    return _skew(v, n)[..., n - 1:2 * n - 1]


def _rel_tiles(table, reach, n):
    x = _skew(table, n)
    return jnp.stack([x[..., (2 * reach + 1 - o) * n - 1:(2 * reach + 2 - o) * n - 1]
                      for o in range(2 * reach + 1)])


def _group_heads(b):
    t, h = b.shape[:2]
    return b.reshape(t, GROUPS, h // GROUPS, *b.shape[2:])


def _t5_saturation():
    nb = T5_BUCKETS // 2
    max_exact = nb // 2
    return math.ceil(max_exact * (T5_MAX_DIST / max_exact) ** ((nb - 1 - max_exact) / (nb - max_exact))) + 1


def _bias_b(t5_b):
    n, reach = ATT_T, B_NEAR
    assert reach * n + 1 >= _t5_saturation()
    rel = jnp.arange(-(reach * n + n - 1), reach * n + n, dtype=jnp.int32)
    table = (t5_b[_t5_bucket(rel)].astype(F32) * LOG2E).T
    far = (t5_b[_t5_bucket(jnp.array([-T5_MAX_DIST * 2, T5_MAX_DIST * 2], jnp.int32))].astype(F32) * LOG2E)
    heads = t5_b.shape[1]
    far = far.reshape(2, GROUPS, heads // GROUPS).transpose(1, 0, 2).reshape(GROUPS, 2 * heads // GROUPS, 1)
    return _group_heads(_rel_tiles(table, reach, n)), jnp.broadcast_to(far, far.shape[:2] + (n,))


def _bias_d(t5_d):
    n = ATT_T
    reach = max(w // 2 for w, _ in D_BRANCHES) // n
    rel = jnp.arange(-(reach * n + n - 1), reach * n + n, dtype=jnp.int32)
    mult = sum(((jnp.abs(rel) <= w // 2) & (rel % d == 0)).astype(F32) for w, d in D_BRANCHES)
    b = t5_d[_t5_bucket(rel)].astype(F32).T
    table = jnp.where(mult[None] > 0, (b + jnp.log(jnp.maximum(mult, 1.0))[None]) * LOG2E, NEG_INF)
    return _group_heads(_rel_tiles(table, reach, n)), reach


def _bias_c(rpb, rows):
    heads = rpb.shape[0]
    kr = min(C_WIN_ROWS, rows)
    n_row_tiles = rows // C_ROWS_PER_TILE
    pad = GRID_W - C_WIN_COLS
    col_tiles = _toeplitz(jnp.pad(rpb.astype(F32) * LOG2E, ((0, 0), (0, 0), (pad, pad))), GRID_W)
    c = np.arange(GRID_W)[None, :]
    kc = np.arange(GRID_W)[:, None]
    cs = np.clip(c - C_WIN_COLS // 2, 0, GRID_W - C_WIN_COLS)
    col_ok = jnp.asarray((kc >= cs) & (kc < cs + C_WIN_COLS))
    col_tiles = jnp.where(col_ok, col_tiles, NEG_INF)
    masked = jnp.full((heads, GRID_W, GRID_W), NEG_INF, F32)
    tiles = []
    for it in (0, min(1, n_row_tiles - 1), n_row_tiles - 1):
        for off in (-1, 0, 1):
            key_rows = []
            for a in range(C_ROWS_PER_TILE):
                blocks = []
                for b in range(C_ROWS_PER_TILE):
                    r = it * C_ROWS_PER_TILE + b
                    key_r = (it + off) * C_ROWS_PER_TILE + a
                    rs = min(max(r - kr // 2, 0), rows - kr)
                    ok = 0 <= key_r < rows and rs <= key_r < rs + kr
                    blocks.append(col_tiles[:, key_r - r + C_WIN_ROWS - 1] if ok else masked)
                key_rows.append(jnp.concatenate(blocks, axis=2))
            tiles.append(jnp.concatenate(key_rows, axis=1))
    return _group_heads(jnp.stack(tiles))


def _c_tile_index(i, off, nt):
    variant = jnp.where(i == 0, 0, jnp.where(i == nt - 1, 2, 1))
    return variant * 3 + off + 1


def _rope_tables(t_len):
    t = jnp.arange(t_len, dtype=jnp.int32)
    n_freq = HEAD_DIM // 4
    inv_freq = ROPE_THETA ** (-jnp.arange(n_freq, dtype=F32) / n_freq)
    ang = jnp.concatenate([(t // GRID_W).astype(F32)[:, None] * inv_freq[None, :],
                           (t % GRID_W).astype(F32)[:, None] * inv_freq[None, :]], axis=-1)
    cos, sin = jnp.cos(ang), jnp.sin(ang)
    reps = LANES // HEAD_DIM
    return (jnp.tile(jnp.concatenate([cos, cos], axis=-1), (1, reps)),
            jnp.tile(jnp.concatenate([-sin, sin], axis=-1), (1, reps)))


def _deinterleave(w):
    return jnp.swapaxes(w.reshape(w.shape[:-1] + (HEAD_DIM // 2, 2)), -1, -2).reshape(w.shape)


def _swap_halves(w):
    return jnp.concatenate([w[..., HEAD_DIM // 2:], w[..., :HEAD_DIM // 2]], axis=-1)


def _prep_ffn(w_gate, w_up, w_down):
    return w_gate.astype(BF16), w_up.astype(BF16), w_down.astype(BF16)


def _prep_mixer_a(w_in, g_q, g_k):
    d = w_in.shape[0]
    qw = A_HEADS * HEAD_DIM
    kw = A_KV_HEADS * HEAD_DIM
    rep = A_HEADS // A_KV_HEADS
    wq = _deinterleave(w_in[:, :qw].reshape(d, A_HEADS, HEAD_DIM))
    wk = _deinterleave(w_in[:, qw:qw + kw].reshape(d, A_KV_HEADS, 1, HEAD_DIM))
    wk = jnp.broadcast_to(wk, (d, A_KV_HEADS, rep, HEAD_DIM)).reshape(d, A_HEADS, HEAD_DIM)
    wqk = jnp.concatenate([wq, wk], axis=1)
    wv = w_in[:, qw + kw:qw + 2 * kw]
    w = jnp.concatenate([wqk.reshape(d, -1), wv], axis=1).astype(BF16)
    gq, gk = _deinterleave(g_q) * (SCALE * LOG2E), _deinterleave(g_k)
    g = jnp.concatenate([jnp.tile(gq, A_HEADS), jnp.tile(gk, A_HEADS)])
    gsw = jnp.concatenate([jnp.tile(_swap_halves(gq), A_HEADS), jnp.tile(_swap_halves(gk), A_HEADS)])
    return w, g[None, :].astype(F32), gsw[None, :].astype(F32)


def _prep_mixers_bcd(w_in):
    a_cols = (A_HEADS + 2 * A_KV_HEADS) * HEAD_DIM
    w = w_in[:, a_cols:]
    is_q = (np.arange(w.shape[1]) // GROUP_WIDTH) % 3 == 0
    return (w * jnp.asarray(np.where(is_q, SCALE * LOG2E, 1.0), F32)[None, :]).astype(BF16)


def _head_sum_matrix():
    idx = np.arange(MXU_DIM) // HEAD_DIM
    return jnp.asarray(idx[:, None] == idx[None, :], dtype=BF16)


def _trunk(x3, layers, bias_b, far_b, bias_d, d_reach, final_norm):
    batch, t_len, d = x3.shape
    x = x3.reshape(batch * t_len, d)
    cos, sin = _rope_tables(t_len)
    e = _head_sum_matrix()
    depth = len(layers)
    for l, p in enumerate(layers):
        x = _ffn(x, p["ffn1_norm"], *p["ffn1"], final_norm, final=False)
        qa, ka, vta = _aproj(x, p["mix_norm"], p["wa"], e, p["ga"], p["ga_sw"], cos, sin, t_len)
        qk, vtb, vtc, vtd = _proj(x, p["mix_norm"], p["w_bcd"], t_len)

        oa = _attention(qa, ka, vta, t_len=t_len, tile=ATT_T, qcol=0, kcol=0, unroll=DENSE_UNROLL)
        ob = _attention(qk, qk, vtb, t_len=t_len, tile=ATT_T, qcol=0, kcol=GROUPS,
                        bias=bias_b, reach=B_NEAR, bias_index=lambda i, off, nt: off + B_NEAR, far=far_b,
                        diff=(p["b_lambda"], p["b_subln"], 0.8 - 0.6 * math.exp(-0.3 * l)), unroll=BIASED_UNROLL)
        oc = _attention(qk, qk, vtc, t_len=t_len, tile=C_T, qcol=2 * GROUPS, kcol=3 * GROUPS,
                        bias=_bias_c(p["c_rpb"], t_len // GRID_W), reach=1, bias_index=_c_tile_index, unroll=BIASED_UNROLL)
        od = _attention(qk, qk, vtd, t_len=t_len, tile=ATT_T, qcol=4 * GROUPS, kcol=5 * GROUPS,
                        bias=bias_d, reach=d_reach, bias_index=lambda i, off, nt: off + d_reach, unroll=BIASED_UNROLL)
        x = _outproj(x, oa, ob, oc, od, p["w_out"])
        x = _ffn(x, p["ffn2_norm"], *p["ffn2"], final_norm, final=(l == depth - 1))
    return x.reshape(batch, t_len, d)


def kernel(x_prompt, x_sample, ffn1_norm, ffn1_w_gate, ffn1_w_up, ffn1_w_down, mix_norm, w_in, a_q_norm, a_k_norm, b_lambda, b_subln, c_rpb, w_out, ffn2_norm, ffn2_w_gate, ffn2_w_up, ffn2_w_down, t5_table, final_norm):
    depth = w_in.shape[0]
    layers = []
    for l in range(depth):
        wa, ga, ga_sw = _prep_mixer_a(w_in[l], a_q_norm[l], a_k_norm[l])
        layers.append(dict(
            ffn1_norm=ffn1_norm[l][None, :], ffn2_norm=ffn2_norm[l][None, :], mix_norm=mix_norm[l][None, :],
            ffn1=_prep_ffn(ffn1_w_gate[l], ffn1_w_up[l], ffn1_w_down[l]),
            ffn2=_prep_ffn(ffn2_w_gate[l], ffn2_w_up[l], ffn2_w_down[l]),
            wa=wa, ga=ga, ga_sw=ga_sw,
            w_bcd=_prep_mixers_bcd(w_in[l]),
            b_lambda=b_lambda[l].astype(F32), b_subln=b_subln[l][None, :].astype(F32),
            c_rpb=c_rpb[l], w_out=w_out[l].astype(BF16)))
    bias_b, far_b = _bias_b(t5_table[:, :B_HEADS])
    bias_d, d_reach = _bias_d(t5_table[:, B_HEADS:])
    fn = final_norm[None, :]
    return tuple(_trunk(x3, layers, bias_b, far_b, bias_d, d_reach, fn) for x3 in (x_prompt, x_sample))
```

```python
import functools
import math

import jax
import jax.numpy as jnp
import numpy as np
from jax import lax
from jax.experimental import pallas as pl
from jax.experimental.pallas import tpu as pltpu

F32 = jnp.float32
BF16 = jnp.bfloat16

HEAD_DIM = 64
GRID_W = 64
EPS = 1e-6
NEG_INF = -1e30
SCALE = HEAD_DIM ** -0.5
LOG2E = math.log2(math.e)
ROPE_THETA = 10000.0
A_HEADS, A_KV_HEADS = 8, 2
B_HEADS, B_V_DIM = 4, 128
C_HEADS, C_WIN_ROWS, C_WIN_COLS = 8, 8, 16
D_HEADS = 8
D_BRANCHES = ((128, 1), (512, 4), (2048, 16))
T5_BUCKETS, T5_MAX_DIST = 32, 1024
GROUP_WIDTH = 512

LANES = 128
BF16_SUBLANES = 16
MXU_DIM = 256
HEADS_PER_GROUP = MXU_DIM // HEAD_DIM
GROUPS = GROUP_WIDTH // MXU_DIM
VMEM_LIMIT = 56 * 1024 * 1024
FFN_TM, FFN_TF = 1024, 512
PROJ_TM, PROJ_TN = 1024, 1536
APROJ_TM = 512
OUT_TM = 512
ATT_T = 512
C_ROWS_PER_TILE = 4
C_T = C_ROWS_PER_TILE * GRID_W
ONES_ROWS = BF16_SUBLANES
ATT_UNROLL = 4
C_UNROLL = 2
B_NEAR = 2


def _cparams(sem):
    return pltpu.CompilerParams(dimension_semantics=sem, vmem_limit_bytes=VMEM_LIMIT)


def _rms(x, g):
    return x * lax.rsqrt(jnp.mean(x * x, axis=-1, keepdims=True) + EPS) * g


def _ffn_kernel(x_ref, g_ref, wg_ref, wu_ref, wd_ref, fg_ref, o_ref, xn_ref, *, final):
    j = pl.program_id(1)

    @pl.when(j == 0)
    def _():
        x = x_ref[...]
        xn_ref[...] = _rms(x, g_ref[...]).astype(BF16)
        o_ref[...] = x

    h = jnp.dot(xn_ref[...], jnp.concatenate([wg_ref[...], wu_ref[...]], axis=1), preferred_element_type=F32)
    tf = h.shape[1] // 2
    hg, hu = h[:, :tf], h[:, tf:]
    a = (hg * jax.nn.sigmoid(hg)) * hu
    o_ref[...] += jnp.dot((0.5 * a).astype(BF16), wd_ref[...], preferred_element_type=F32)

    if final:
        @pl.when(j == pl.num_programs(1) - 1)
        def _():
            o_ref[...] = _rms(o_ref[...], fg_ref[...])


def _ffn(x, g, wg, wu, wd, fg, *, final):
    n, d = x.shape
    dff = wd.shape[0]
    tm, tf = min(FFN_TM, n), FFN_TF
    return pl.pallas_call(
        functools.partial(_ffn_kernel, final=final),
        out_shape=jax.ShapeDtypeStruct((n, d), F32),
        grid=(n // tm, dff // tf),
        in_specs=[
            pl.BlockSpec((tm, d), lambda i, j: (i, 0)),
            pl.BlockSpec((1, d), lambda i, j: (0, 0)),
            pl.BlockSpec((d, tf), lambda i, j: (0, j)),
            pl.BlockSpec((d, tf), lambda i, j: (0, j)),
            pl.BlockSpec((tf, d), lambda i, j: (j, 0)),
            pl.BlockSpec((1, d), lambda i, j: (0, 0)),
        ],
        out_specs=pl.BlockSpec((tm, d), lambda i, j: (i, 0)),
        scratch_shapes=[pltpu.VMEM((tm, d), BF16)],
        compiler_params=_cparams(("parallel", "arbitrary")),
        name="ffn",
    )(x, g, wg, wu, wd, fg)


def _store_values_t(vt_ref, v, dv):
    groups, n_tiles, rows, tile = vt_ref.shape
    per_head = dv + ONES_ROWS
    ones = jnp.ones((ONES_ROWS, tile), BF16)
    heads_per_group = rows // per_head
    for t in range(n_tiles):
        vt = v[t * tile:(t + 1) * tile, :].T
        for g in range(groups):
            for h in range(heads_per_group):
                src = (g * heads_per_group + h) * dv
                vt_ref[g, t, h * per_head:h * per_head + dv, :] = vt[src:src + dv, :].astype(BF16)
                vt_ref[g, t, h * per_head + dv:(h + 1) * per_head, :] = ones


def _proj_kernel(x_ref, g_ref, w_ref, o_ref, vtb_ref, vtc_ref, vtd_ref, xn_ref):
    j = pl.program_id(1)

    @pl.when(j == 0)
    def _():
        xn_ref[...] = _rms(x_ref[...], g_ref[...]).astype(BF16)

    y = jnp.dot(xn_ref[...], w_ref[...], preferred_element_type=F32)
    nqk = o_ref.shape[1]
    o_ref[...] = y[:, :nqk].astype(BF16)
    for m, (vt_ref, dv) in enumerate(((vtb_ref, B_V_DIM), (vtc_ref, HEAD_DIM), (vtd_ref, HEAD_DIM))):
        @pl.when(j == m)
        def _():
            _store_values_t(vt_ref, y[:, nqk:], dv)


def _values_shape(batch, t_len, heads, dv, tile):
    return (batch, GROUPS, t_len // tile, heads // GROUPS * (dv + ONES_ROWS), tile)


def _proj(x, g, w, t_len):
    n, d = x.shape
    batch = n // t_len
    tm, tn = math.gcd(PROJ_TM, t_len), PROJ_TN
    npos = t_len // tm
    nqk = 2 * GROUP_WIDTH
    vt_shapes = [_values_shape(batch, t_len, B_HEADS, B_V_DIM, ATT_T),
                 _values_shape(batch, t_len, C_HEADS, HEAD_DIM, C_T),
                 _values_shape(batch, t_len, D_HEADS, HEAD_DIM, ATT_T)]
    vt_specs = [pl.BlockSpec((None, GROUPS, tm // s[4], s[3], s[4]), lambda i, j: (i // npos, 0, i % npos, 0, 0))
                for s in vt_shapes]
    return pl.pallas_call(
        _proj_kernel,
        out_shape=[jax.ShapeDtypeStruct((n, 3 * nqk), BF16)] + [jax.ShapeDtypeStruct(s, BF16) for s in vt_shapes],
        grid=(n // tm, w.shape[1] // tn),
        in_specs=[
            pl.BlockSpec((tm, d), lambda i, j: (i, 0)),
            pl.BlockSpec((1, d), lambda i, j: (0, 0)),
            pl.BlockSpec((d, tn), lambda i, j: (0, j)),
        ],
        out_specs=[pl.BlockSpec((tm, nqk), lambda i, j: (i, j))] + vt_specs,
        scratch_shapes=[pltpu.VMEM((tm, d), BF16)],
        compiler_params=_cparams(("parallel", "arbitrary")),
        name="proj_bcd",
    )(x, g, w)


def _aproj_kernel(x_ref, g_ref, w_ref, e_ref, gc_ref, gs_ref, cos_ref, sin_ref, q_ref, k_ref, vt_ref):
    xn = _rms(x_ref[...], g_ref[...]).astype(BF16)
    y = jnp.dot(xn, w_ref[...], preferred_element_type=F32)
    nqk = gc_ref.shape[1]
    _store_values_t(vt_ref, y[:, nqk:], HEAD_DIM)
    e = e_ref[...]
    cos = jnp.concatenate([cos_ref[...]] * (MXU_DIM // LANES), axis=1)
    sin = jnp.concatenate([sin_ref[...]] * (MXU_DIM // LANES), axis=1)
    first_half = lax.broadcasted_iota(jnp.int32, (y.shape[0], MXU_DIM), 1) % HEAD_DIM < HEAD_DIM // 2
    half = nqk // 2
    for lo in range(0, nqk, MXU_DIM):
        yc = y[:, lo:lo + MXU_DIM]
        partner = jnp.where(first_half, pltpu.roll(yc, MXU_DIM - HEAD_DIM // 2, axis=1),
                            pltpu.roll(yc, HEAD_DIM // 2, axis=1))
        sq = yc * yc
        hi = sq.astype(BF16)
        rest = (sq - hi.astype(F32)).astype(BF16)
        ssum = (jnp.dot(hi, e, preferred_element_type=F32)
                + jnp.dot(rest, e, preferred_element_type=F32))
        r = lax.rsqrt(ssum * (1.0 / HEAD_DIM) + EPS)
        out = (yc * gc_ref[:, lo:lo + MXU_DIM] * cos
               + partner * gs_ref[:, lo:lo + MXU_DIM] * sin) * r
        dst, off = (q_ref, lo) if lo < half else (k_ref, lo - half)
        dst[:, off:off + MXU_DIM] = out.astype(BF16)


def _aproj(x, g, w, e, gc, gs, cos, sin, t_len):
    n, d = x.shape
    tm = math.gcd(APROJ_TM, t_len)
    npos = t_len // tm
    nqk = gc.shape[1]
    const = lambda i: (0, 0)
    vt_shape = _values_shape(n // t_len, t_len, A_KV_HEADS, HEAD_DIM, ATT_T)
    return pl.pallas_call(
        _aproj_kernel,
        out_shape=(jax.ShapeDtypeStruct((n, nqk // 2), BF16),
                   jax.ShapeDtypeStruct((n, nqk // 2), BF16),
                   jax.ShapeDtypeStruct(vt_shape, BF16)),
        grid=(n // tm,),
        in_specs=[
            pl.BlockSpec((tm, d), lambda i: (i, 0)),
            pl.BlockSpec((1, d), const),
            pl.BlockSpec(w.shape, const),
            pl.BlockSpec(e.shape, const),
            pl.BlockSpec(gc.shape, const),
            pl.BlockSpec(gs.shape, const),
            pl.BlockSpec((tm, LANES), lambda i: (i % npos, 0)),
            pl.BlockSpec((tm, LANES), lambda i: (i % npos, 0)),
        ],
        out_specs=(pl.BlockSpec((tm, nqk // 2), lambda i: (i, 0)),
                   pl.BlockSpec((tm, nqk // 2), lambda i: (i, 0)),
                   pl.BlockSpec((None, GROUPS, tm // ATT_T) + vt_shape[3:],
                                lambda i: (i // npos, 0, i % npos, 0, 0))),
        compiler_params=_cparams(("parallel",)),
        name="proj_a",
    )(x, g, w, e, gc, gs, cos, sin)


def _outproj_kernel(x_ref, oa_ref, ob_ref, oc_ref, od_ref, w_ref, o_ref):
    acc = x_ref[...]
    for m, ref in enumerate((oa_ref, ob_ref, oc_ref, od_ref)):
        acc = acc + jnp.dot(ref[...], w_ref[m * GROUP_WIDTH:(m + 1) * GROUP_WIDTH, :],
                            preferred_element_type=F32)
    o_ref[...] = acc


def _outproj(x, oa, ob, oc, od, w):
    n, d = x.shape
    tm = min(OUT_TM, n)
    mix = pl.BlockSpec((tm, GROUP_WIDTH), lambda i: (i, 0))
    return pl.pallas_call(
        _outproj_kernel,
        out_shape=jax.ShapeDtypeStruct((n, d), F32),
        grid=(n // tm,),
        in_specs=[pl.BlockSpec((tm, d), lambda i: (i, 0)), mix, mix, mix, mix,
                  pl.BlockSpec(w.shape, lambda i: (0, 0))],
        out_specs=pl.BlockSpec((tm, d), lambda i: (i, 0)),
        compiler_params=_cparams(("parallel",)),
        name="out_proj",
    )(x, oa, ob, oc, od, w)


def _attn_init(q_ref, qm_ref):
    qt = q_ref[...].astype(F32).T
    unit = lax.broadcasted_iota(jnp.int32, qt.shape, 0) // HEAD_DIM
    for u in range(qm_ref.shape[0]):
        qm_ref[u] = jnp.where(unit == u, qt, 0.0).astype(BF16)


def _scores(k, qm_ref, u):
    return jnp.dot(k, qm_ref[u], preferred_element_type=F32)


def _attn_units(k, qm_ref, update, st0_ref=None, k_next=None, from_st0=True):
    n_units = qm_ref.shape[0]
    st_next = st0_ref[...] if (st0_ref is not None and from_st0) else _scores(k, qm_ref, 0)
    for u in range(n_units):
        st = st_next
        if u + 1 < n_units:
            st_next = _scores(k, qm_ref, u + 1)
        elif st0_ref is not None:
            st0_ref[...] = _scores(k_next, qm_ref, 0)
        update(u, st)


def _tile_first(st, vt, u, m_ref, acc_ref, bias, shift):
    if bias is not None:
        st = st + bias
    m_tile = jnp.max(st, axis=0, keepdims=True)
    m_ref[u] = m_tile if shift is None else m_tile + shift
    acc_ref[u] = jnp.dot(vt, jnp.exp2(st - m_tile).astype(BF16), preferred_element_type=F32)


def _tile_fixed(st, vt, u, m_ref, acc_ref, bias, shift):
    if bias is not None:
        st = st + bias
    sub = m_ref[u] if shift is None else m_ref[u] - shift
    acc_ref[u] += jnp.dot(vt, jnp.exp2(st - sub).astype(BF16), preferred_element_type=F32)


def _tile_online(st, vt, u, m_ref, acc_ref, bias, shift):
    if bias is not None:
        st = st + bias
    smax = jnp.max(st, axis=0, keepdims=True)
    if shift is not None:
        smax = smax + shift
    m_old = m_ref[u]
    m_new = jnp.maximum(m_old, smax)
    sub = m_new if shift is None else m_new - shift
    p = jnp.exp2(st - sub).astype(BF16)
    acc_ref[u] = jnp.exp2(m_old - m_new) * acc_ref[u] + jnp.dot(vt, p, preferred_element_type=F32)
    m_ref[u] = m_new


def _attn_finish(o_ref, acc_ref, dv, diff_refs, lam_init):
    n_units = acc_ref.shape[0]
    outs = [acc_ref[u, :dv, :] / acc_ref[u, dv:dv + 1, :] for u in range(n_units)]
    if diff_refs:
        lam_ref, subg_ref = diff_refs
        lp = lam_ref[...]
        lam = (jnp.exp(jnp.sum(lp[0:1] * lp[1:2], keepdims=True))
               - jnp.exp(jnp.sum(lp[2:3] * lp[3:4], keepdims=True)) + lam_init)
        outs = [outs[2 * h] - lam * outs[2 * h + 1] for h in range(n_units // 2)]
    o = jnp.concatenate(outs, axis=0).T
    if diff_refs:
        o = jnp.concatenate([_rms(o[:, lo:lo + dv], subg_ref[...]) * (1.0 - lam_init)
                             for lo in range(0, o.shape[1], dv)], axis=1)
    o_ref[...] = o.astype(BF16)


def _value_rows(u, n_units, rows_in_block, dv):
    per_head = dv + ONES_ROWS
    heads_in_block = rows_in_block // per_head
    lo = (u * heads_in_block // n_units) * per_head
    return lo, lo + per_head


PLAIN, FAR_LEFT, FAR_RIGHT, NEAR = "plain", 0, 1, "near"


def _attn_kernel(*refs, dv, reach, far, bias_index, lam_init, unroll):
    it = iter(refs)
    q_ref, k_ref, vt_ref = next(it), next(it), next(it)
    bias_ref = next(it) if reach is not None else None
    far_ref = next(it) if far else None
    diff_refs = (next(it), next(it)) if lam_init is not None else None
    o_ref, qm_ref, m_ref, acc_ref, st0_ref = next(it), next(it), next(it), next(it), next(it)

    i = pl.program_id(2)
    n_units = qm_ref.shape[0]
    nk, rows, tile = vt_ref.shape
    bias_heads = bias_ref.shape[1] if bias_ref is not None else 0
    _attn_init(q_ref, qm_ref)
    lo, hi = (0, nk) if reach is None else (jnp.maximum(i - reach, 0), jnp.minimum(i + reach + 1, nk))
    p_first, p_end = (1, nk) if (reach is None or far) else (lo + 1, hi)

    def tile_at(p):
        p = jnp.minimum(p, p_end - 1)
        return jnp.clip(p - (p <= i).astype(jnp.int32), 0, nk - 1)

    def keys(j):
        return k_ref[pl.ds(pl.multiple_of(j * tile, tile), tile), :]

    def updater(tile_fn, j, side):
        def update(u, st):
            r0, r1 = _value_rows(u, n_units, rows, dv)
            h = u * bias_heads // n_units
            bias = bias_ref[bias_index(i, j - i, nk), h] if side == NEAR else None
            shift = far_ref[side * bias_heads + h:side * bias_heads + h + 1, :] if side in (FAR_LEFT, FAR_RIGHT) else None
            tile_fn(st, vt_ref[j, r0:r1, :], u, m_ref, acc_ref, bias, shift)
        return update

    def visit_rest(tile_fn, pipelined):
        def span(p_lo, p_hi, side):
            def visit(p):
                j = tile_at(p)
                if pipelined:
                    _attn_units(keys(j), qm_ref, updater(tile_fn, j, side), st0_ref, keys(tile_at(p + 1)))
                else:
                    _attn_units(keys(j), qm_ref, updater(tile_fn, j, side))

            def single(p, carry):
                visit(p)
                return carry

            if not pipelined:
                lax.fori_loop(p_lo, p_hi, single, 0)
                return

            def group(t, carry):
                for s in range(unroll):
                    visit(p_lo + t * unroll + s)
                return carry
            n_groups = (p_hi - p_lo) // unroll
            lax.fori_loop(0, n_groups, group, 0)
            lax.fori_loop(p_lo + n_groups * unroll, p_hi, single, 0)
        if reach is None:
            span(1, nk, PLAIN)
        elif far:
            span(1, lo + 1, FAR_LEFT)
            span(lo + 1, hi, NEAR)
            span(hi, nk, FAR_RIGHT)
        else:
            span(lo + 1, hi, NEAR)

    diag = PLAIN if reach is None else NEAR
    _attn_units(keys(i), qm_ref, updater(_tile_first, i, diag), st0_ref, keys(tile_at(p_first)), from_st0=False)
    visit_rest(_tile_fixed, pipelined=True)

    finite = jnp.min(jnp.where(jnp.isfinite(acc_ref[...]), 1.0, 0.0))

    @pl.when(finite < 0.5)
    def _():
        m_ref[...] = jnp.full(m_ref.shape, -jnp.inf, F32)
        acc_ref[...] = jnp.zeros(acc_ref.shape, F32)
        _attn_units(keys(i), qm_ref, updater(_tile_online, i, diag))
        visit_rest(_tile_online, pipelined=False)

    _attn_finish(o_ref, acc_ref, dv, diff_refs, lam_init)


def _attention(q, k, vt, *, t_len, tile, qcol, kcol, bias=None, reach=None, bias_index=None, far=None, diff=None,
               unroll=1):
    n = q.shape[0]
    batch, nt = n // t_len, t_len // tile
    dv = B_V_DIM if diff else HEAD_DIM
    rows = vt.shape[3]
    once = pl.Buffered(1)
    in_specs = [
        pl.BlockSpec((tile, MXU_DIM), lambda b, g, i: (b * nt + i, qcol + g)),
        pl.BlockSpec((t_len, MXU_DIM), lambda b, g, i: (b, kcol + g), pipeline_mode=once),
        pl.BlockSpec((None, None) + vt.shape[2:], lambda b, g, i: (b, g, 0, 0, 0), pipeline_mode=once),
    ]
    args = [q, k, vt]
    if bias is not None:
        in_specs.append(pl.BlockSpec((bias.shape[0], None, bias.shape[2], tile, tile),
                                     lambda b, g, i: (0, g, 0, 0, 0), pipeline_mode=once))
        args.append(bias)
    if far is not None:
        in_specs.append(pl.BlockSpec((None,) + far.shape[1:], lambda b, g, i: (g, 0, 0)))
        args.append(far)
    lam_init = None
    if diff:
        lam_p, subg, lam_init = diff
        in_specs += [pl.BlockSpec(lam_p.shape, lambda b, g, i: (0, 0)),
                     pl.BlockSpec(subg.shape, lambda b, g, i: (0, 0))]
        args += [lam_p, subg]
    return pl.pallas_call(
        functools.partial(_attn_kernel, dv=dv, reach=reach, far=far is not None, bias_index=bias_index,
                          lam_init=lam_init, unroll=unroll),
        out_shape=jax.ShapeDtypeStruct((n, GROUP_WIDTH), BF16),
        grid=(batch, GROUPS, nt),
        in_specs=in_specs,
        out_specs=pl.BlockSpec((tile, MXU_DIM), lambda b, g, i: (b * nt + i, g)),
        scratch_shapes=[pltpu.VMEM((HEADS_PER_GROUP, MXU_DIM, tile), BF16),
                        pltpu.VMEM((HEADS_PER_GROUP, 1, tile), F32),
                        pltpu.VMEM((HEADS_PER_GROUP, dv + ONES_ROWS, tile), F32),
                        pltpu.VMEM((vt.shape[4], tile), F32)],
        compiler_params=_cparams(("parallel", "parallel", "arbitrary")),
        name="attention",
    )(*args)


def _t5_bucket(rel):
    nb = T5_BUCKETS // 2
    max_exact = nb // 2
    side = (rel > 0).astype(jnp.int32) * nb
    n = jnp.abs(rel)
    large = max_exact + (jnp.log(jnp.maximum(n, 1).astype(F32) / max_exact)
                         / math.log(T5_MAX_DIST / max_exact) * (nb - max_exact)).astype(jnp.int32)
    large = jnp.minimum(large, nb - 1)
    return side + jnp.where(n < max_exact, n, large)


def _skew(v, n):
    lead = v.shape[:-1]
    r = jnp.concatenate([v[..., ::-1], jnp.zeros(lead + (1,), v.dtype)], axis=-1)
    x = jnp.broadcast_to(r[..., None, :], lead + (n, r.shape[-1]))
    k = np.arange(n)[:, None]
    for b in range(n.bit_length() - 1):
        x = jnp.where(jnp.asarray((k >> b) & 1 == 1), jnp.roll(x, 1 << b, axis=-1), x)
    return x


def _toeplitz(v, n):
    return _skew(v, n)[..., n - 1:2 * n - 1]


def _rel_tiles(table, reach, n):
    x = _skew(table, n)
    return jnp.stack([x[..., (2 * reach + 1 - o) * n - 1:(2 * reach + 2 - o) * n - 1]
                      for o in range(2 * reach + 1)])


def _group_heads(b):
    t, h = b.shape[:2]
    return b.reshape(t, GROUPS, h // GROUPS, *b.shape[2:])


def _t5_saturation():
    nb = T5_BUCKETS // 2
    max_exact = nb // 2
    return math.ceil(max_exact * (T5_MAX_DIST / max_exact) ** ((nb - 1 - max_exact) / (nb - max_exact))) + 1


def _bias_b(t5_b):
    n, reach = ATT_T, B_NEAR
    assert reach * n + 1 >= _t5_saturation()
    rel = jnp.arange(-(reach * n + n - 1), reach * n + n, dtype=jnp.int32)
    table = (t5_b[_t5_bucket(rel)].astype(F32) * LOG2E).T
    far = (t5_b[_t5_bucket(jnp.array([-T5_MAX_DIST * 2, T5_MAX_DIST * 2], jnp.int32))].astype(F32) * LOG2E)
    heads = t5_b.shape[1]
    far = far.reshape(2, GROUPS, heads // GROUPS).transpose(1, 0, 2).reshape(GROUPS, 2 * heads // GROUPS, 1)
    return _group_heads(_rel_tiles(table, reach, n)), jnp.broadcast_to(far, far.shape[:2] + (n,))


def _bias_d(t5_d):
    n = ATT_T
    reach = max(w // 2 for w, _ in D_BRANCHES) // n
    rel = jnp.arange(-(reach * n + n - 1), reach * n + n, dtype=jnp.int32)
    mult = sum(((jnp.abs(rel) <= w // 2) & (rel % d == 0)).astype(F32) for w, d in D_BRANCHES)
    b = t5_d[_t5_bucket(rel)].astype(F32).T
    table = jnp.where(mult[None] > 0, (b + jnp.log(jnp.maximum(mult, 1.0))[None]) * LOG2E, NEG_INF)
    return _group_heads(_rel_tiles(table, reach, n)), reach


def _bias_c(rpb, rows):
    heads = rpb.shape[0]
    kr = min(C_WIN_ROWS, rows)
    n_row_tiles = rows // C_ROWS_PER_TILE
    pad = GRID_W - C_WIN_COLS
    col_tiles = _toeplitz(jnp.pad(rpb.astype(F32) * LOG2E, ((0, 0), (0, 0), (pad, pad))), GRID_W)
    c = np.arange(GRID_W)[None, :]
    kc = np.arange(GRID_W)[:, None]
    cs = np.clip(c - C_WIN_COLS // 2, 0, GRID_W - C_WIN_COLS)
    col_ok = jnp.asarray((kc >= cs) & (kc < cs + C_WIN_COLS))
    col_tiles = jnp.where(col_ok, col_tiles, NEG_INF)
    masked = jnp.full((heads, GRID_W, GRID_W), NEG_INF, F32)
    tiles = []
    for it in (0, min(1, n_row_tiles - 1), n_row_tiles - 1):
        for off in (-1, 0, 1):
            key_rows = []
            for a in range(C_ROWS_PER_TILE):
                blocks = []
                for b in range(C_ROWS_PER_TILE):
                    r = it * C_ROWS_PER_TILE + b
                    key_r = (it + off) * C_ROWS_PER_TILE + a
                    rs = min(max(r - kr // 2, 0), rows - kr)
                    ok = 0 <= key_r < rows and rs <= key_r < rs + kr
                    blocks.append(col_tiles[:, key_r - r + C_WIN_ROWS - 1] if ok else masked)
                key_rows.append(jnp.concatenate(blocks, axis=2))
            tiles.append(jnp.concatenate(key_rows, axis=1))
    return _group_heads(jnp.stack(tiles))


def _c_tile_index(i, off, nt):
    variant = jnp.where(i == 0, 0, jnp.where(i == nt - 1, 2, 1))
    return variant * 3 + off + 1


def _rope_tables(t_len):
    t = jnp.arange(t_len, dtype=jnp.int32)
    n_freq = HEAD_DIM // 4
    inv_freq = ROPE_THETA ** (-jnp.arange(n_freq, dtype=F32) / n_freq)
    ang = jnp.concatenate([(t // GRID_W).astype(F32)[:, None] * inv_freq[None, :],
                           (t % GRID_W).astype(F32)[:, None] * inv_freq[None, :]], axis=-1)
    cos, sin = jnp.cos(ang), jnp.sin(ang)
    reps = LANES // HEAD_DIM
    return (jnp.tile(jnp.concatenate([cos, cos], axis=-1), (1, reps)),
            jnp.tile(jnp.concatenate([-sin, sin], axis=-1), (1, reps)))


def _deinterleave(w):
    return jnp.swapaxes(w.reshape(w.shape[:-1] + (HEAD_DIM // 2, 2)), -1, -2).reshape(w.shape)


def _swap_halves(w):
    return jnp.concatenate([w[..., HEAD_DIM // 2:], w[..., :HEAD_DIM // 2]], axis=-1)


def _prep_ffn(w_gate, w_up, w_down):
    return w_gate.astype(BF16), w_up.astype(BF16), w_down.astype(BF16)


def _prep_mixer_a(w_in, g_q, g_k):
    d = w_in.shape[0]
    qw = A_HEADS * HEAD_DIM
    kw = A_KV_HEADS * HEAD_DIM
    rep = A_HEADS // A_KV_HEADS
    wq = _deinterleave(w_in[:, :qw].reshape(d, A_HEADS, HEAD_DIM))
    wk = _deinterleave(w_in[:, qw:qw + kw].reshape(d, A_KV_HEADS, 1, HEAD_DIM))
    wk = jnp.broadcast_to(wk, (d, A_KV_HEADS, rep, HEAD_DIM)).reshape(d, A_HEADS, HEAD_DIM)
    wqk = jnp.concatenate([wq, wk], axis=1)
    wv = w_in[:, qw + kw:qw + 2 * kw]
    w = jnp.concatenate([wqk.reshape(d, -1), wv], axis=1).astype(BF16)
    gq, gk = _deinterleave(g_q) * (SCALE * LOG2E), _deinterleave(g_k)
    g = jnp.concatenate([jnp.tile(gq, A_HEADS), jnp.tile(gk, A_HEADS)])
    gsw = jnp.concatenate([jnp.tile(_swap_halves(gq), A_HEADS), jnp.tile(_swap_halves(gk), A_HEADS)])
    return w, g[None, :].astype(F32), gsw[None, :].astype(F32)


def _prep_mixers_bcd(w_in):
    a_cols = (A_HEADS + 2 * A_KV_HEADS) * HEAD_DIM
    w = w_in[:, a_cols:]
    is_q = (np.arange(w.shape[1]) // GROUP_WIDTH) % 3 == 0
    return (w * jnp.asarray(np.where(is_q, SCALE * LOG2E, 1.0), F32)[None, :]).astype(BF16)


def _head_sum_matrix():
    idx = np.arange(MXU_DIM) // HEAD_DIM
    return jnp.asarray(idx[:, None] == idx[None, :], dtype=BF16)


def _trunk(x3, layers, bias_b, far_b, bias_d, d_reach, final_norm):
    batch, t_len, d = x3.shape
    x = x3.reshape(batch * t_len, d)
    cos, sin = _rope_tables(t_len)
    e = _head_sum_matrix()
    depth = len(layers)
    for l, p in enumerate(layers):
        x = _ffn(x, p["ffn1_norm"], *p["ffn1"], final_norm, final=False)
        qa, ka, vta = _aproj(x, p["mix_norm"], p["wa"], e, p["ga"], p["ga_sw"], cos, sin, t_len)
        qk, vtb, vtc, vtd = _proj(x, p["mix_norm"], p["w_bcd"], t_len)

        oa = _attention(qa, ka, vta, t_len=t_len, tile=ATT_T, qcol=0, kcol=0, unroll=ATT_UNROLL)
        ob = _attention(qk, qk, vtb, t_len=t_len, tile=ATT_T, qcol=0, kcol=GROUPS,
                        bias=bias_b, reach=B_NEAR, bias_index=lambda i, off, nt: off + B_NEAR, far=far_b,
                        diff=(p["b_lambda"], p["b_subln"], 0.8 - 0.6 * math.exp(-0.3 * l)), unroll=ATT_UNROLL)
        oc = _attention(qk, qk, vtc, t_len=t_len, tile=C_T, qcol=2 * GROUPS, kcol=3 * GROUPS,
                        bias=_bias_c(p["c_rpb"], t_len // GRID_W), reach=1, bias_index=_c_tile_index, unroll=C_UNROLL)
        od = _attention(qk, qk, vtd, t_len=t_len, tile=ATT_T, qcol=4 * GROUPS, kcol=5 * GROUPS,
                        bias=bias_d, reach=d_reach, bias_index=lambda i, off, nt: off + d_reach, unroll=ATT_UNROLL)
        x = _outproj(x, oa, ob, oc, od, p["w_out"])
        x = _ffn(x, p["ffn2_norm"], *p["ffn2"], final_norm, final=(l == depth - 1))
    return x.reshape(batch, t_len, d)


def kernel(x_prompt, x_sample, ffn1_norm, ffn1_w_gate, ffn1_w_up, ffn1_w_down, mix_norm, w_in, a_q_norm, a_k_norm, b_lambda, b_subln, c_rpb, w_out, ffn2_norm, ffn2_w_gate, ffn2_w_up, ffn2_w_down, t5_table, final_norm):
    depth = w_in.shape[0]
    layers = []
    for l in range(depth):
        wa, ga, ga_sw = _prep_mixer_a(w_in[l], a_q_norm[l], a_k_norm[l])
        layers.append(dict(
            ffn1_norm=ffn1_norm[l][None, :], ffn2_norm=ffn2_norm[l][None, :], mix_norm=mix_norm[l][None, :],
            ffn1=_prep_ffn(ffn1_w_gate[l], ffn1_w_up[l], ffn1_w_down[l]),
            ffn2=_prep_ffn(ffn2_w_gate[l], ffn2_w_up[l], ffn2_w_down[l]),
            wa=wa, ga=ga, ga_sw=ga_sw,
            w_bcd=_prep_mixers_bcd(w_in[l]),
            b_lambda=b_lambda[l].astype(F32), b_subln=b_subln[l][None, :].astype(F32),
            c_rpb=c_rpb[l], w_out=w_out[l].astype(BF16)))
    bias_b, far_b = _bias_b(t5_table[:, :B_HEADS])
    bias_d, d_reach = _bias_d(t5_table[:, B_HEADS:])
    fn = final_norm[None, :]
    return tuple(_trunk(x3, layers, bias_b, far_b, bias_d, d_reach, fn) for x3 in (x_prompt, x_sample))
```

```python
import functools
import math

import jax
import jax.numpy as jnp
import numpy as np
from jax import lax
from jax.experimental import pallas as pl
from jax.experimental.pallas import tpu as pltpu

F32 = jnp.float32
BF16 = jnp.bfloat16

HEAD_DIM = 64
GRID_W = 64
EPS = 1e-6
NEG_INF = -1e30
SCALE = HEAD_DIM ** -0.5
LOG2E = math.log2(math.e)
ROPE_THETA = 10000.0
A_HEADS, A_KV_HEADS = 8, 2
B_HEADS, B_V_DIM = 4, 128
C_HEADS, C_WIN_ROWS, C_WIN_COLS = 8, 8, 16
D_HEADS = 8
D_BRANCHES = ((128, 1), (512, 4), (2048, 16))
T5_BUCKETS, T5_MAX_DIST = 32, 1024
GROUP_WIDTH = 512

LANES = 128
BF16_SUBLANES = 16
MXU_DIM = 256
HEADS_PER_GROUP = MXU_DIM // HEAD_DIM
GROUPS = GROUP_WIDTH // MXU_DIM
VMEM_LIMIT = 56 * 1024 * 1024
FFN_TM, FFN_TF = 1024, 512
PROJ_TM, PROJ_TN = 1024, 1536
APROJ_TM = 512
OUT_TM = 512
ATT_T = 512
C_ROWS_PER_TILE = 4
C_T = C_ROWS_PER_TILE * GRID_W
ONES_ROWS = BF16_SUBLANES
ATT_UNROLL = 4
C_UNROLL = 2
B_NEAR = 2


def _cparams(sem):
    return pltpu.CompilerParams(dimension_semantics=sem, vmem_limit_bytes=VMEM_LIMIT)


def _rms(x, g):
    return x * lax.rsqrt(jnp.mean(x * x, axis=-1, keepdims=True) + EPS) * g


def _ffn_kernel(x_ref, g_ref, wg_ref, wu_ref, wd_ref, fg_ref, o_ref, xn_ref, *, final):
    j = pl.program_id(1)

    @pl.when(j == 0)
    def _():
        x = x_ref[...]
        xn_ref[...] = _rms(x, g_ref[...]).astype(BF16)
        o_ref[...] = x

    h = jnp.dot(xn_ref[...], jnp.concatenate([wg_ref[...], wu_ref[...]], axis=1), preferred_element_type=F32)
    tf = h.shape[1] // 2
    hg, hu = h[:, :tf], h[:, tf:]
    a = (hg * jax.nn.sigmoid(hg)) * hu
    o_ref[...] += jnp.dot((0.5 * a).astype(BF16), wd_ref[...], preferred_element_type=F32)

    if final:
        @pl.when(j == pl.num_programs(1) - 1)
        def _():
            o_ref[...] = _rms(o_ref[...], fg_ref[...])


def _ffn(x, g, wg, wu, wd, fg, *, final):
    n, d = x.shape
    dff = wd.shape[0]
    tm, tf = min(FFN_TM, n), FFN_TF
    return pl.pallas_call(
        functools.partial(_ffn_kernel, final=final),
        out_shape=jax.ShapeDtypeStruct((n, d), F32),
        grid=(n // tm, dff // tf),
        in_specs=[
            pl.BlockSpec((tm, d), lambda i, j: (i, 0)),
            pl.BlockSpec((1, d), lambda i, j: (0, 0)),
            pl.BlockSpec((d, tf), lambda i, j: (0, j)),
            pl.BlockSpec((d, tf), lambda i, j: (0, j)),
            pl.BlockSpec((tf, d), lambda i, j: (j, 0)),
            pl.BlockSpec((1, d), lambda i, j: (0, 0)),
        ],
        out_specs=pl.BlockSpec((tm, d), lambda i, j: (i, 0)),
        scratch_shapes=[pltpu.VMEM((tm, d), BF16)],
        compiler_params=_cparams(("parallel", "arbitrary")),
        name="ffn",
    )(x, g, wg, wu, wd, fg)


def _store_values_t(vt_ref, v, dv):
    groups, n_tiles, rows, tile = vt_ref.shape
    per_head = dv + ONES_ROWS
    ones = jnp.ones((ONES_ROWS, tile), BF16)
    heads_per_group = rows // per_head
    for t in range(n_tiles):
        vt = v[t * tile:(t + 1) * tile, :].T
        for g in range(groups):
            for h in range(heads_per_group):
                src = (g * heads_per_group + h) * dv
                vt_ref[g, t, h * per_head:h * per_head + dv, :] = vt[src:src + dv, :].astype(BF16)
                vt_ref[g, t, h * per_head + dv:(h + 1) * per_head, :] = ones


def _proj_kernel(x_ref, g_ref, w_ref, o_ref, vtb_ref, vtc_ref, vtd_ref, xn_ref):
    j = pl.program_id(1)

    @pl.when(j == 0)
    def _():
        xn_ref[...] = _rms(x_ref[...], g_ref[...]).astype(BF16)

    y = jnp.dot(xn_ref[...], w_ref[...], preferred_element_type=F32)
    nqk = o_ref.shape[1]
    o_ref[...] = y[:, :nqk].astype(BF16)
    for m, (vt_ref, dv) in enumerate(((vtb_ref, B_V_DIM), (vtc_ref, HEAD_DIM), (vtd_ref, HEAD_DIM))):
        @pl.when(j == m)
        def _():
            _store_values_t(vt_ref, y[:, nqk:], dv)


def _values_shape(batch, t_len, heads, dv, tile):
    return (batch, GROUPS, t_len // tile, heads // GROUPS * (dv + ONES_ROWS), tile)


def _proj(x, g, w, t_len):
    n, d = x.shape
    batch = n // t_len
    tm, tn = math.gcd(PROJ_TM, t_len), PROJ_TN
    npos = t_len // tm
    nqk = 2 * GROUP_WIDTH
    vt_shapes = [_values_shape(batch, t_len, B_HEADS, B_V_DIM, ATT_T),
                 _values_shape(batch, t_len, C_HEADS, HEAD_DIM, C_T),
                 _values_shape(batch, t_len, D_HEADS, HEAD_DIM, ATT_T)]
    vt_specs = [pl.BlockSpec((None, GROUPS, tm // s[4], s[3], s[4]), lambda i, j: (i // npos, 0, i % npos, 0, 0))
                for s in vt_shapes]
    return pl.pallas_call(
        _proj_kernel,
        out_shape=[jax.ShapeDtypeStruct((n, 3 * nqk), BF16)] + [jax.ShapeDtypeStruct(s, BF16) for s in vt_shapes],
        grid=(n // tm, w.shape[1] // tn),
        in_specs=[
            pl.BlockSpec((tm, d), lambda i, j: (i, 0)),
            pl.BlockSpec((1, d), lambda i, j: (0, 0)),
            pl.BlockSpec((d, tn), lambda i, j: (0, j)),
        ],
        out_specs=[pl.BlockSpec((tm, nqk), lambda i, j: (i, j))] + vt_specs,
        scratch_shapes=[pltpu.VMEM((tm, d), BF16)],
        compiler_params=_cparams(("parallel", "arbitrary")),
        name="proj_bcd",
    )(x, g, w)


def _aproj_kernel(x_ref, g_ref, w_ref, e_ref, gc_ref, gs_ref, cos_ref, sin_ref, q_ref, k_ref, vt_ref):
    xn = _rms(x_ref[...], g_ref[...]).astype(BF16)
    y = jnp.dot(xn, w_ref[...], preferred_element_type=F32)
    nqk = gc_ref.shape[1]
    _store_values_t(vt_ref, y[:, nqk:], HEAD_DIM)
    e = e_ref[...]
    cos = jnp.concatenate([cos_ref[...]] * (MXU_DIM // LANES), axis=1)
    sin = jnp.concatenate([sin_ref[...]] * (MXU_DIM // LANES), axis=1)
    first_half = lax.broadcasted_iota(jnp.int32, (y.shape[0], MXU_DIM), 1) % HEAD_DIM < HEAD_DIM // 2
    half = nqk // 2
    for lo in range(0, nqk, MXU_DIM):
        yc = y[:, lo:lo + MXU_DIM]
        partner = jnp.where(first_half, pltpu.roll(yc, MXU_DIM - HEAD_DIM // 2, axis=1),
                            pltpu.roll(yc, HEAD_DIM // 2, axis=1))
        sq = yc * yc
        hi = sq.astype(BF16)
        rest = (sq - hi.astype(F32)).astype(BF16)
        ssum = (jnp.dot(hi, e, preferred_element_type=F32)
                + jnp.dot(rest, e, preferred_element_type=F32))
        r = lax.rsqrt(ssum * (1.0 / HEAD_DIM) + EPS)
        out = (yc * gc_ref[:, lo:lo + MXU_DIM] * cos
               + partner * gs_ref[:, lo:lo + MXU_DIM] * sin) * r
        dst, off = (q_ref, lo) if lo < half else (k_ref, lo - half)
        dst[:, off:off + MXU_DIM] = out.astype(BF16)


def _aproj(x, g, w, e, gc, gs, cos, sin, t_len):
    n, d = x.shape
    tm = math.gcd(APROJ_TM, t_len)
    npos = t_len // tm
    nqk = gc.shape[1]
    const = lambda i: (0, 0)
    vt_shape = _values_shape(n // t_len, t_len, A_KV_HEADS, HEAD_DIM, ATT_T)
    return pl.pallas_call(
        _aproj_kernel,
        out_shape=(jax.ShapeDtypeStruct((n, nqk // 2), BF16),
                   jax.ShapeDtypeStruct((n, nqk // 2), BF16),
                   jax.ShapeDtypeStruct(vt_shape, BF16)),
        grid=(n // tm,),
        in_specs=[
            pl.BlockSpec((tm, d), lambda i: (i, 0)),
            pl.BlockSpec((1, d), const),
            pl.BlockSpec(w.shape, const),
            pl.BlockSpec(e.shape, const),
            pl.BlockSpec(gc.shape, const),
            pl.BlockSpec(gs.shape, const),
            pl.BlockSpec((tm, LANES), lambda i: (i % npos, 0)),
            pl.BlockSpec((tm, LANES), lambda i: (i % npos, 0)),
        ],
        out_specs=(pl.BlockSpec((tm, nqk // 2), lambda i: (i, 0)),
                   pl.BlockSpec((tm, nqk // 2), lambda i: (i, 0)),
                   pl.BlockSpec((None, GROUPS, tm // ATT_T) + vt_shape[3:],
                                lambda i: (i // npos, 0, i % npos, 0, 0))),
        compiler_params=_cparams(("parallel",)),
        name="proj_a",
    )(x, g, w, e, gc, gs, cos, sin)


def _outproj_kernel(x_ref, oa_ref, ob_ref, oc_ref, od_ref, w_ref, o_ref):
    acc = x_ref[...]
    for m, ref in enumerate((oa_ref, ob_ref, oc_ref, od_ref)):
        acc = acc + jnp.dot(ref[...], w_ref[m * GROUP_WIDTH:(m + 1) * GROUP_WIDTH, :],
                            preferred_element_type=F32)
    o_ref[...] = acc


def _outproj(x, oa, ob, oc, od, w):
    n, d = x.shape
    tm = min(OUT_TM, n)
    mix = pl.BlockSpec((tm, GROUP_WIDTH), lambda i: (i, 0))
    return pl.pallas_call(
        _outproj_kernel,
        out_shape=jax.ShapeDtypeStruct((n, d), F32),
        grid=(n // tm,),
        in_specs=[pl.BlockSpec((tm, d), lambda i: (i, 0)), mix, mix, mix, mix,
                  pl.BlockSpec(w.shape, lambda i: (0, 0))],
        out_specs=pl.BlockSpec((tm, d), lambda i: (i, 0)),
        compiler_params=_cparams(("parallel",)),
        name="out_proj",
    )(x, oa, ob, oc, od, w)


def _attn_init(q_ref, qm_ref):
    qt = q_ref[...].astype(F32).T
    unit = lax.broadcasted_iota(jnp.int32, qt.shape, 0) // HEAD_DIM
    for u in range(qm_ref.shape[0]):
        qm_ref[u] = jnp.where(unit == u, qt, 0.0).astype(BF16)


def _scores(k, qm_ref, u):
    return jnp.dot(k, qm_ref[u], preferred_element_type=F32)


def _attn_units(k, qm_ref, update, st0_ref=None, k_next=None, from_st0=True):
    n_units = qm_ref.shape[0]
    st_next = st0_ref[...] if (st0_ref is not None and from_st0) else _scores(k, qm_ref, 0)
    for u in range(n_units):
        st = st_next
        if u + 1 < n_units:
            st_next = _scores(k, qm_ref, u + 1)
        elif st0_ref is not None:
            st0_ref[...] = _scores(k_next, qm_ref, 0)
        update(u, st)


def _tile_first(st, vt, u, m_ref, acc_ref, bias, shift, *, masked):
    if bias is not None:
        st = st + bias
    m_tile = jnp.max(st if masked else st[:8], axis=0, keepdims=True)
    m_ref[u] = m_tile if shift is None else m_tile + shift
    acc_ref[u] = jnp.dot(vt, jnp.exp2(st - m_tile).astype(BF16), preferred_element_type=F32)


def _tile_fixed(st, vt, u, m_ref, acc_ref, bias, shift):
    if bias is not None:
        st = st + bias
    sub = m_ref[u] if shift is None else m_ref[u] - shift
    acc_ref[u] += jnp.dot(vt, jnp.exp2(st - sub).astype(BF16), preferred_element_type=F32)


def _tile_online(st, vt, u, m_ref, acc_ref, bias, shift):
    if bias is not None:
        st = st + bias
    smax = jnp.max(st, axis=0, keepdims=True)
    if shift is not None:
        smax = smax + shift
    m_old = m_ref[u]
    m_new = jnp.maximum(m_old, smax)
    sub = m_new if shift is None else m_new - shift
    p = jnp.exp2(st - sub).astype(BF16)
    acc_ref[u] = jnp.exp2(m_old - m_new) * acc_ref[u] + jnp.dot(vt, p, preferred_element_type=F32)
    m_ref[u] = m_new


def _attn_finish(o_ref, acc_ref, dv, diff_refs, lam_init):
    n_units = acc_ref.shape[0]
    outs = [acc_ref[u, :dv, :] / acc_ref[u, dv:dv + 1, :] for u in range(n_units)]
    if diff_refs:
        lam_ref, subg_ref = diff_refs
        lp = lam_ref[...]
        lam = (jnp.exp(jnp.sum(lp[0:1] * lp[1:2], keepdims=True))
               - jnp.exp(jnp.sum(lp[2:3] * lp[3:4], keepdims=True)) + lam_init)
        outs = [outs[2 * h] - lam * outs[2 * h + 1] for h in range(n_units // 2)]
    o = jnp.concatenate(outs, axis=0).T
    if diff_refs:
        o = jnp.concatenate([_rms(o[:, lo:lo + dv], subg_ref[...]) * (1.0 - lam_init)
                             for lo in range(0, o.shape[1], dv)], axis=1)
    o_ref[...] = o.astype(BF16)


def _value_rows(u, n_units, rows_in_block, dv):
    per_head = dv + ONES_ROWS
    heads_in_block = rows_in_block // per_head
    lo = (u * heads_in_block // n_units) * per_head
    return lo, lo + per_head


PLAIN, FAR_LEFT, FAR_RIGHT, NEAR = "plain", 0, 1, "near"


def _attn_kernel(*refs, dv, reach, far, bias_index, lam_init, unroll):
    it = iter(refs)
    q_ref, k_ref, vt_ref = next(it), next(it), next(it)
    bias_ref = next(it) if reach is not None else None
    far_ref = next(it) if far else None
    diff_refs = (next(it), next(it)) if lam_init is not None else None
    o_ref, qm_ref, m_ref, acc_ref, st0_ref = next(it), next(it), next(it), next(it), next(it)

    i = pl.program_id(2)
    n_units = qm_ref.shape[0]
    nk, rows, tile = vt_ref.shape
    bias_heads = bias_ref.shape[1] if bias_ref is not None else 0
    _attn_init(q_ref, qm_ref)
    lo, hi = (0, nk) if reach is None else (jnp.maximum(i - reach, 0), jnp.minimum(i + reach + 1, nk))
    p_first, p_end = (1, nk) if (reach is None or far) else (lo + 1, hi)

    def tile_at(p):
        p = jnp.minimum(p, p_end - 1)
        return jnp.clip(p - (p <= i).astype(jnp.int32), 0, nk - 1)

    def keys(j):
        return k_ref[pl.ds(pl.multiple_of(j * tile, tile), tile), :]

    def updater(tile_fn, j, side):
        def update(u, st):
            r0, r1 = _value_rows(u, n_units, rows, dv)
            h = u * bias_heads // n_units
            bias = bias_ref[bias_index(i, j - i, nk), h] if side == NEAR else None
            shift = far_ref[side * bias_heads + h:side * bias_heads + h + 1, :] if side in (FAR_LEFT, FAR_RIGHT) else None
            tile_fn(st, vt_ref[j, r0:r1, :], u, m_ref, acc_ref, bias, shift)
        return update

    def visit_rest(tile_fn, pipelined):
        def span(p_lo, p_hi, side):
            def visit(p):
                j = tile_at(p)
                if pipelined:
                    _attn_units(keys(j), qm_ref, updater(tile_fn, j, side), st0_ref, keys(tile_at(p + 1)))
                else:
                    _attn_units(keys(j), qm_ref, updater(tile_fn, j, side))

            def single(p, carry):
                visit(p)
                return carry

            if not pipelined:
                lax.fori_loop(p_lo, p_hi, single, 0)
                return

            def group(t, carry):
                for s in range(unroll):
                    visit(p_lo + t * unroll + s)
                return carry
            n_groups = (p_hi - p_lo) // unroll
            lax.fori_loop(0, n_groups, group, 0)
            lax.fori_loop(p_lo + n_groups * unroll, p_hi, single, 0)
        if reach is None:
            span(1, nk, PLAIN)
        elif far:
            span(1, lo + 1, FAR_LEFT)
            span(lo + 1, hi, NEAR)
            span(hi, nk, FAR_RIGHT)
        else:
            span(lo + 1, hi, NEAR)

    diag = PLAIN if reach is None else NEAR
    first = functools.partial(_tile_first, masked=reach is not None and not far)
    _attn_units(keys(i), qm_ref, updater(first, i, diag), st0_ref, keys(tile_at(p_first)), from_st0=False)
    visit_rest(_tile_fixed, pipelined=True)

    finite = jnp.min(jnp.where(jnp.isfinite(acc_ref[...]), 1.0, 0.0))

    @pl.when(finite < 0.5)
    def _():
        m_ref[...] = jnp.full(m_ref.shape, -jnp.inf, F32)
        acc_ref[...] = jnp.zeros(acc_ref.shape, F32)
        _attn_units(keys(i), qm_ref, updater(_tile_online, i, diag))
        visit_rest(_tile_online, pipelined=False)

    _attn_finish(o_ref, acc_ref, dv, diff_refs, lam_init)


def _attention(q, k, vt, *, t_len, tile, qcol, kcol, bias=None, reach=None, bias_index=None, far=None, diff=None,
               unroll=1):
    n = q.shape[0]
    batch, nt = n // t_len, t_len // tile
    dv = B_V_DIM if diff else HEAD_DIM
    rows = vt.shape[3]
    once = pl.Buffered(1)
    in_specs = [
        pl.BlockSpec((tile, MXU_DIM), lambda b, g, i: (b * nt + i, qcol + g)),
        pl.BlockSpec((t_len, MXU_DIM), lambda b, g, i: (b, kcol + g), pipeline_mode=once),
        pl.BlockSpec((None, None) + vt.shape[2:], lambda b, g, i: (b, g, 0, 0, 0), pipeline_mode=once),
    ]
    args = [q, k, vt]
    if bias is not None:
        in_specs.append(pl.BlockSpec((bias.shape[0], None, bias.shape[2], tile, tile),
                                     lambda b, g, i: (0, g, 0, 0, 0), pipeline_mode=once))
        args.append(bias)
    if far is not None:
        in_specs.append(pl.BlockSpec((None,) + far.shape[1:], lambda b, g, i: (g, 0, 0)))
        args.append(far)
    lam_init = None
    if diff:
        lam_p, subg, lam_init = diff
        in_specs += [pl.BlockSpec(lam_p.shape, lambda b, g, i: (0, 0)),
                     pl.BlockSpec(subg.shape, lambda b, g, i: (0, 0))]
        args += [lam_p, subg]
    return pl.pallas_call(
        functools.partial(_attn_kernel, dv=dv, reach=reach, far=far is not None, bias_index=bias_index,
                          lam_init=lam_init, unroll=unroll),
        out_shape=jax.ShapeDtypeStruct((n, GROUP_WIDTH), BF16),
        grid=(batch, GROUPS, nt),
        in_specs=in_specs,
        out_specs=pl.BlockSpec((tile, MXU_DIM), lambda b, g, i: (b * nt + i, g)),
        scratch_shapes=[pltpu.VMEM((HEADS_PER_GROUP, MXU_DIM, tile), BF16),
                        pltpu.VMEM((HEADS_PER_GROUP, 1, tile), F32),
                        pltpu.VMEM((HEADS_PER_GROUP, dv + ONES_ROWS, tile), F32),
                        pltpu.VMEM((vt.shape[4], tile), F32)],
        compiler_params=_cparams(("parallel", "parallel", "arbitrary")),
        name="attention",
    )(*args)


def _t5_bucket(rel):
    nb = T5_BUCKETS // 2
    max_exact = nb // 2
    side = (rel > 0).astype(jnp.int32) * nb
    n = jnp.abs(rel)
    large = max_exact + (jnp.log(jnp.maximum(n, 1).astype(F32) / max_exact)
                         / math.log(T5_MAX_DIST / max_exact) * (nb - max_exact)).astype(jnp.int32)
    large = jnp.minimum(large, nb - 1)
    return side + jnp.where(n < max_exact, n, large)


def _skew(v, n):
    lead = v.shape[:-1]
    r = jnp.concatenate([v[..., ::-1], jnp.zeros(lead + (1,), v.dtype)], axis=-1)
    x = jnp.broadcast_to(r[..., None, :], lead + (n, r.shape[-1]))
    k = np.arange(n)[:, None]
    for b in range(n.bit_length() - 1):
        x = jnp.where(jnp.asarray((k >> b) & 1 == 1), jnp.roll(x, 1 << b, axis=-1), x)
    return x


def _toeplitz(v, n):
    return _skew(v, n)[..., n - 1:2 * n - 1]


def _rel_tiles(table, reach, n):
    x = _skew(table, n)
    return jnp.stack([x[..., (2 * reach + 1 - o) * n - 1:(2 * reach + 2 - o) * n - 1]
                      for o in range(2 * reach + 1)])


def _group_heads(b):
    t, h = b.shape[:2]
    return b.reshape(t, GROUPS, h // GROUPS, *b.shape[2:])


def _t5_saturation():
    nb = T5_BUCKETS // 2
    max_exact = nb // 2
    return math.ceil(max_exact * (T5_MAX_DIST / max_exact) ** ((nb - 1 - max_exact) / (nb - max_exact))) + 1


def _bias_b(t5_b):
    n, reach = ATT_T, B_NEAR
    assert reach * n + 1 >= _t5_saturation()
    rel = jnp.arange(-(reach * n + n - 1), reach * n + n, dtype=jnp.int32)
    table = (t5_b[_t5_bucket(rel)].astype(F32) * LOG2E).T
    far = (t5_b[_t5_bucket(jnp.array([-T5_MAX_DIST * 2, T5_MAX_DIST * 2], jnp.int32))].astype(F32) * LOG2E)
    heads = t5_b.shape[1]
    far = far.reshape(2, GROUPS, heads // GROUPS).transpose(1, 0, 2).reshape(GROUPS, 2 * heads // GROUPS, 1)
    return _group_heads(_rel_tiles(table, reach, n)), jnp.broadcast_to(far, far.shape[:2] + (n,))


def _bias_d(t5_d):
    n = ATT_T
    reach = max(w // 2 for w, _ in D_BRANCHES) // n
    rel = jnp.arange(-(reach * n + n - 1), reach * n + n, dtype=jnp.int32)
    mult = sum(((jnp.abs(rel) <= w // 2) & (rel % d == 0)).astype(F32) for w, d in D_BRANCHES)
    b = t5_d[_t5_bucket(rel)].astype(F32).T
    table = jnp.where(mult[None] > 0, (b + jnp.log(jnp.maximum(mult, 1.0))[None]) * LOG2E, NEG_INF)
    return _group_heads(_rel_tiles(table, reach, n)), reach


def _bias_c(rpb, rows):
    heads = rpb.shape[0]
    kr = min(C_WIN_ROWS, rows)
    n_row_tiles = rows // C_ROWS_PER_TILE
    pad = GRID_W - C_WIN_COLS
    col_tiles = _toeplitz(jnp.pad(rpb.astype(F32) * LOG2E, ((0, 0), (0, 0), (pad, pad))), GRID_W)
    c = np.arange(GRID_W)[None, :]
    kc = np.arange(GRID_W)[:, None]
    cs = np.clip(c - C_WIN_COLS // 2, 0, GRID_W - C_WIN_COLS)
    col_ok = jnp.asarray((kc >= cs) & (kc < cs + C_WIN_COLS))
    col_tiles = jnp.where(col_ok, col_tiles, NEG_INF)
    masked = jnp.full((heads, GRID_W, GRID_W), NEG_INF, F32)
    tiles = []
    for it in (0, min(1, n_row_tiles - 1), n_row_tiles - 1):
        for off in (-1, 0, 1):
            key_rows = []
            for a in range(C_ROWS_PER_TILE):
                blocks = []
                for b in range(C_ROWS_PER_TILE):
                    r = it * C_ROWS_PER_TILE + b
                    key_r = (it + off) * C_ROWS_PER_TILE + a
                    rs = min(max(r - kr // 2, 0), rows - kr)
                    ok = 0 <= key_r < rows and rs <= key_r < rs + kr
                    blocks.append(col_tiles[:, key_r - r + C_WIN_ROWS - 1] if ok else masked)
                key_rows.append(jnp.concatenate(blocks, axis=2))
            tiles.append(jnp.concatenate(key_rows, axis=1))
    return _group_heads(jnp.stack(tiles))


def _c_tile_index(i, off, nt):
    variant = jnp.where(i == 0, 0, jnp.where(i == nt - 1, 2, 1))
    return variant * 3 + off + 1


def _rope_tables(t_len):
    t = jnp.arange(t_len, dtype=jnp.int32)
    n_freq = HEAD_DIM // 4
    inv_freq = ROPE_THETA ** (-jnp.arange(n_freq, dtype=F32) / n_freq)
    ang = jnp.concatenate([(t // GRID_W).astype(F32)[:, None] * inv_freq[None, :],
                           (t % GRID_W).astype(F32)[:, None] * inv_freq[None, :]], axis=-1)
    cos, sin = jnp.cos(ang), jnp.sin(ang)
    reps = LANES // HEAD_DIM
    return (jnp.tile(jnp.concatenate([cos, cos], axis=-1), (1, reps)),
            jnp.tile(jnp.concatenate([-sin, sin], axis=-1), (1, reps)))


def _deinterleave(w):
    return jnp.swapaxes(w.reshape(w.shape[:-1] + (HEAD_DIM // 2, 2)), -1, -2).reshape(w.shape)


def _swap_halves(w):
    return jnp.concatenate([w[..., HEAD_DIM // 2:], w[..., :HEAD_DIM // 2]], axis=-1)


def _prep_ffn(w_gate, w_up, w_down):
    return w_gate.astype(BF16), w_up.astype(BF16), w_down.astype(BF16)


def _prep_mixer_a(w_in, g_q, g_k):
    d = w_in.shape[0]
    qw = A_HEADS * HEAD_DIM
    kw = A_KV_HEADS * HEAD_DIM
    rep = A_HEADS // A_KV_HEADS
    wq = _deinterleave(w_in[:, :qw].reshape(d, A_HEADS, HEAD_DIM))
    wk = _deinterleave(w_in[:, qw:qw + kw].reshape(d, A_KV_HEADS, 1, HEAD_DIM))
    wk = jnp.broadcast_to(wk, (d, A_KV_HEADS, rep, HEAD_DIM)).reshape(d, A_HEADS, HEAD_DIM)
    wqk = jnp.concatenate([wq, wk], axis=1)
    wv = w_in[:, qw + kw:qw + 2 * kw]
    w = jnp.concatenate([wqk.reshape(d, -1), wv], axis=1).astype(BF16)
    gq, gk = _deinterleave(g_q) * (SCALE * LOG2E), _deinterleave(g_k)
    g = jnp.concatenate([jnp.tile(gq, A_HEADS), jnp.tile(gk, A_HEADS)])
    gsw = jnp.concatenate([jnp.tile(_swap_halves(gq), A_HEADS), jnp.tile(_swap_halves(gk), A_HEADS)])
    return w, g[None, :].astype(F32), gsw[None, :].astype(F32)


def _prep_mixers_bcd(w_in):
    a_cols = (A_HEADS + 2 * A_KV_HEADS) * HEAD_DIM
    w = w_in[:, a_cols:]
    is_q = (np.arange(w.shape[1]) // GROUP_WIDTH) % 3 == 0
    return (w * jnp.asarray(np.where(is_q, SCALE * LOG2E, 1.0), F32)[None, :]).astype(BF16)


def _head_sum_matrix():
    idx = np.arange(MXU_DIM) // HEAD_DIM
    return jnp.asarray(idx[:, None] == idx[None, :], dtype=BF16)


def _trunk(x3, layers, bias_b, far_b, bias_d, d_reach, final_norm):
    batch, t_len, d = x3.shape
    x = x3.reshape(batch * t_len, d)
    cos, sin = _rope_tables(t_len)
    e = _head_sum_matrix()
    depth = len(layers)
    for l, p in enumerate(layers):
        x = _ffn(x, p["ffn1_norm"], *p["ffn1"], final_norm, final=False)
        qa, ka, vta = _aproj(x, p["mix_norm"], p["wa"], e, p["ga"], p["ga_sw"], cos, sin, t_len)
        qk, vtb, vtc, vtd = _proj(x, p["mix_norm"], p["w_bcd"], t_len)

        oa = _attention(qa, ka, vta, t_len=t_len, tile=ATT_T, qcol=0, kcol=0, unroll=ATT_UNROLL)
        ob = _attention(qk, qk, vtb, t_len=t_len, tile=ATT_T, qcol=0, kcol=GROUPS,
                        bias=bias_b, reach=B_NEAR, bias_index=lambda i, off, nt: off + B_NEAR, far=far_b,
                        diff=(p["b_lambda"], p["b_subln"], 0.8 - 0.6 * math.exp(-0.3 * l)), unroll=ATT_UNROLL)
        oc = _attention(qk, qk, vtc, t_len=t_len, tile=C_T, qcol=2 * GROUPS, kcol=3 * GROUPS,
                        bias=_bias_c(p["c_rpb"], t_len // GRID_W), reach=1, bias_index=_c_tile_index, unroll=C_UNROLL)
        od = _attention(qk, qk, vtd, t_len=t_len, tile=ATT_T, qcol=4 * GROUPS, kcol=5 * GROUPS,
                        bias=bias_d, reach=d_reach, bias_index=lambda i, off, nt: off + d_reach, unroll=ATT_UNROLL)
        x = _outproj(x, oa, ob, oc, od, p["w_out"])
        x = _ffn(x, p["ffn2_norm"], *p["ffn2"], final_norm, final=(l == depth - 1))
    return x.reshape(batch, t_len, d)


def kernel(x_prompt, x_sample, ffn1_norm, ffn1_w_gate, ffn1_w_up, ffn1_w_down, mix_norm, w_in, a_q_norm, a_k_norm, b_lambda, b_subln, c_rpb, w_out, ffn2_norm, ffn2_w_gate, ffn2_w_up, ffn2_w_down, t5_table, final_norm):
    depth = w_in.shape[0]
    layers = []
    for l in range(depth):
        wa, ga, ga_sw = _prep_mixer_a(w_in[l], a_q_norm[l], a_k_norm[l])
        layers.append(dict(
            ffn1_norm=ffn1_norm[l][None, :], ffn2_norm=ffn2_norm[l][None, :], mix_norm=mix_norm[l][None, :],
            ffn1=_prep_ffn(ffn1_w_gate[l], ffn1_w_up[l], ffn1_w_down[l]),
            ffn2=_prep_ffn(ffn2_w_gate[l], ffn2_w_up[l], ffn2_w_down[l]),
            wa=wa, ga=ga, ga_sw=ga_sw,
            w_bcd=_prep_mixers_bcd(w_in[l]),
            b_lambda=b_lambda[l].astype(F32), b_subln=b_subln[l][None, :].astype(F32),
            c_rpb=c_rpb[l], w_out=w_out[l].astype(BF16)))
    bias_b, far_b = _bias_b(t5_table[:, :B_HEADS])
    bias_d, d_reach = _bias_d(t5_table[:, B_HEADS:])
    fn = final_norm[None, :]
    return tuple(_trunk(x3, layers, bias_b, far_b, bias_d, d_reach, fn) for x3 in (x_prompt, x_sample))
```

```python
import functools
import math

import jax
import jax.numpy as jnp
import numpy as np
from jax import lax
from jax.experimental import pallas as pl
from jax.experimental.pallas import tpu as pltpu

F32 = jnp.float32
BF16 = jnp.bfloat16

HEAD_DIM = 64
GRID_W = 64
EPS = 1e-6
NEG_INF = -1e30
SCALE = HEAD_DIM ** -0.5
LOG2E = math.log2(math.e)
ROPE_THETA = 10000.0
A_HEADS, A_KV_HEADS = 8, 2
B_HEADS, B_V_DIM = 4, 128
C_HEADS, C_WIN_ROWS, C_WIN_COLS = 8, 8, 16
D_HEADS = 8
D_BRANCHES = ((128, 1), (512, 4), (2048, 16))
T5_BUCKETS, T5_MAX_DIST = 32, 1024
GROUP_WIDTH = 512

LANES = 128
BF16_SUBLANES = 16
MXU_DIM = 256
HEADS_PER_GROUP = MXU_DIM // HEAD_DIM
GROUPS = GROUP_WIDTH // MXU_DIM
VMEM_LIMIT = 56 * 1024 * 1024
FFN_TM, FFN_TF = 1024, 512
PROJ_TM, PROJ_TN = 1024, 1536
APROJ_TM = 512
OUT_TM = 512
ATT_T = 512
C_ROWS_PER_TILE = 4
C_T = C_ROWS_PER_TILE * GRID_W
ONES_ROWS = BF16_SUBLANES
ATT_UNROLL = 4
C_UNROLL = 2
B_NEAR = 2


def _cparams(sem):
    return pltpu.CompilerParams(dimension_semantics=sem, vmem_limit_bytes=VMEM_LIMIT)


def _rms(x, g):
    return x * lax.rsqrt(jnp.mean(x * x, axis=-1, keepdims=True) + EPS) * g


def _ffn_kernel(x_ref, g_ref, wg_ref, wu_ref, wd_ref, fg_ref, o_ref, xn_ref, *, final):
    j = pl.program_id(1)

    @pl.when(j == 0)
    def _():
        x = x_ref[...]
        xn_ref[...] = _rms(x, g_ref[...]).astype(BF16)
        o_ref[...] = x

    h = jnp.dot(xn_ref[...], jnp.concatenate([wg_ref[...], wu_ref[...]], axis=1), preferred_element_type=F32)
    tf = h.shape[1] // 2
    hg, hu = h[:, :tf], h[:, tf:]
    a = (hg * jax.nn.sigmoid(hg)) * hu
    o_ref[...] += jnp.dot((0.5 * a).astype(BF16), wd_ref[...], preferred_element_type=F32)

    if final:
        @pl.when(j == pl.num_programs(1) - 1)
        def _():
            o_ref[...] = _rms(o_ref[...], fg_ref[...])


def _ffn(x, g, wg, wu, wd, fg, *, final):
    n, d = x.shape
    dff = wd.shape[0]
    tm, tf = min(FFN_TM, n), FFN_TF
    return pl.pallas_call(
        functools.partial(_ffn_kernel, final=final),
        out_shape=jax.ShapeDtypeStruct((n, d), F32),
        grid=(n // tm, dff // tf),
        in_specs=[
            pl.BlockSpec((tm, d), lambda i, j: (i, 0)),
            pl.BlockSpec((1, d), lambda i, j: (0, 0)),
            pl.BlockSpec((d, tf), lambda i, j: (0, j)),
            pl.BlockSpec((d, tf), lambda i, j: (0, j)),
            pl.BlockSpec((tf, d), lambda i, j: (j, 0)),
            pl.BlockSpec((1, d), lambda i, j: (0, 0)),
        ],
        out_specs=pl.BlockSpec((tm, d), lambda i, j: (i, 0)),
        scratch_shapes=[pltpu.VMEM((tm, d), BF16)],
        compiler_params=_cparams(("parallel", "arbitrary")),
        name="ffn",
    )(x, g, wg, wu, wd, fg)


def _store_values_t(vt_ref, v, dv):
    groups, n_tiles, rows, tile = vt_ref.shape
    per_head = dv + ONES_ROWS
    ones = jnp.ones((ONES_ROWS, tile), BF16)
    heads_per_group = rows // per_head
    for t in range(n_tiles):
        vt = v[t * tile:(t + 1) * tile, :].T
        for g in range(groups):
            for h in range(heads_per_group):
                src = (g * heads_per_group + h) * dv
                vt_ref[g, t, h * per_head:h * per_head + dv, :] = vt[src:src + dv, :].astype(BF16)
                vt_ref[g, t, h * per_head + dv:(h + 1) * per_head, :] = ones


def _proj_kernel(x_ref, g_ref, w_ref, o_ref, vtb_ref, vtc_ref, vtd_ref, xn_ref):
    j = pl.program_id(1)

    @pl.when(j == 0)
    def _():
        xn_ref[...] = _rms(x_ref[...], g_ref[...]).astype(BF16)

    y = jnp.dot(xn_ref[...], w_ref[...], preferred_element_type=F32)
    nqk = o_ref.shape[1]
    o_ref[...] = y[:, :nqk].astype(BF16)
    for m, (vt_ref, dv) in enumerate(((vtb_ref, B_V_DIM), (vtc_ref, HEAD_DIM), (vtd_ref, HEAD_DIM))):
        @pl.when(j == m)
        def _():
            _store_values_t(vt_ref, y[:, nqk:], dv)


def _values_shape(batch, t_len, heads, dv, tile):
    return (batch, GROUPS, t_len // tile, heads // GROUPS * (dv + ONES_ROWS), tile)


def _proj(x, g, w, t_len):
    n, d = x.shape
    batch = n // t_len
    tm, tn = math.gcd(PROJ_TM, t_len), PROJ_TN
    npos = t_len // tm
    nqk = 2 * GROUP_WIDTH
    vt_shapes = [_values_shape(batch, t_len, B_HEADS, B_V_DIM, ATT_T),
                 _values_shape(batch, t_len, C_HEADS, HEAD_DIM, C_T),
                 _values_shape(batch, t_len, D_HEADS, HEAD_DIM, ATT_T)]
    vt_specs = [pl.BlockSpec((None, GROUPS, tm // s[4], s[3], s[4]), lambda i, j: (i // npos, 0, i % npos, 0, 0))
                for s in vt_shapes]
    return pl.pallas_call(
        _proj_kernel,
        out_shape=[jax.ShapeDtypeStruct((n, 3 * nqk), BF16)] + [jax.ShapeDtypeStruct(s, BF16) for s in vt_shapes],
        grid=(n // tm, w.shape[1] // tn),
        in_specs=[
            pl.BlockSpec((tm, d), lambda i, j: (i, 0)),
            pl.BlockSpec((1, d), lambda i, j: (0, 0)),
            pl.BlockSpec((d, tn), lambda i, j: (0, j)),
        ],
        out_specs=[pl.BlockSpec((tm, nqk), lambda i, j: (i, j))] + vt_specs,
        scratch_shapes=[pltpu.VMEM((tm, d), BF16)],
        compiler_params=_cparams(("parallel", "arbitrary")),
        name="proj_bcd",
    )(x, g, w)


def _aproj_kernel(x_ref, g_ref, w_ref, e_ref, gc_ref, gs_ref, cos_ref, sin_ref, q_ref, k_ref, vt_ref):
    xn = _rms(x_ref[...], g_ref[...]).astype(BF16)
    y = jnp.dot(xn, w_ref[...], preferred_element_type=F32)
    nqk = gc_ref.shape[1]
    _store_values_t(vt_ref, y[:, nqk:], HEAD_DIM)
    e = e_ref[...]
    cos = jnp.concatenate([cos_ref[...]] * (MXU_DIM // LANES), axis=1)
    sin = jnp.concatenate([sin_ref[...]] * (MXU_DIM // LANES), axis=1)
    first_half = lax.broadcasted_iota(jnp.int32, (y.shape[0], MXU_DIM), 1) % HEAD_DIM < HEAD_DIM // 2
    half = nqk // 2
    for lo in range(0, nqk, MXU_DIM):
        yc = y[:, lo:lo + MXU_DIM]
        partner = jnp.where(first_half, pltpu.roll(yc, MXU_DIM - HEAD_DIM // 2, axis=1),
                            pltpu.roll(yc, HEAD_DIM // 2, axis=1))
        sq = yc * yc
        hi = sq.astype(BF16)
        rest = (sq - hi.astype(F32)).astype(BF16)
        ssum = (jnp.dot(hi, e, preferred_element_type=F32)
                + jnp.dot(rest, e, preferred_element_type=F32))
        r = lax.rsqrt(ssum * (1.0 / HEAD_DIM) + EPS)
        out = (yc * gc_ref[:, lo:lo + MXU_DIM] * cos
               + partner * gs_ref[:, lo:lo + MXU_DIM] * sin) * r
        dst, off = (q_ref, lo) if lo < half else (k_ref, lo - half)
        dst[:, off:off + MXU_DIM] = out.astype(BF16)


def _aproj(x, g, w, e, gc, gs, cos, sin, t_len):
    n, d = x.shape
    tm = math.gcd(APROJ_TM, t_len)
    npos = t_len // tm
    nqk = gc.shape[1]
    const = lambda i: (0, 0)
    vt_shape = _values_shape(n // t_len, t_len, A_KV_HEADS, HEAD_DIM, ATT_T)
    return pl.pallas_call(
        _aproj_kernel,
        out_shape=(jax.ShapeDtypeStruct((n, nqk // 2), BF16),
                   jax.ShapeDtypeStruct((n, nqk // 2), BF16),
                   jax.ShapeDtypeStruct(vt_shape, BF16)),
        grid=(n // tm,),
        in_specs=[
            pl.BlockSpec((tm, d), lambda i: (i, 0)),
            pl.BlockSpec((1, d), const),
            pl.BlockSpec(w.shape, const),
            pl.BlockSpec(e.shape, const),
            pl.BlockSpec(gc.shape, const),
            pl.BlockSpec(gs.shape, const),
            pl.BlockSpec((tm, LANES), lambda i: (i % npos, 0)),
            pl.BlockSpec((tm, LANES), lambda i: (i % npos, 0)),
        ],
        out_specs=(pl.BlockSpec((tm, nqk // 2), lambda i: (i, 0)),
                   pl.BlockSpec((tm, nqk // 2), lambda i: (i, 0)),
                   pl.BlockSpec((None, GROUPS, tm // ATT_T) + vt_shape[3:],
                                lambda i: (i // npos, 0, i % npos, 0, 0))),
        compiler_params=_cparams(("parallel",)),
        name="proj_a",
    )(x, g, w, e, gc, gs, cos, sin)


def _outproj_kernel(x_ref, oa_ref, ob_ref, oc_ref, od_ref, w_ref, o_ref):
    acc = x_ref[...]
    for m, ref in enumerate((oa_ref, ob_ref, oc_ref, od_ref)):
        acc = acc + jnp.dot(ref[...], w_ref[m * GROUP_WIDTH:(m + 1) * GROUP_WIDTH, :],
                            preferred_element_type=F32)
    o_ref[...] = acc


def _outproj(x, oa, ob, oc, od, w):
    n, d = x.shape
    tm = min(OUT_TM, n)
    mix = pl.BlockSpec((tm, GROUP_WIDTH), lambda i: (i, 0))
    return pl.pallas_call(
        _outproj_kernel,
        out_shape=jax.ShapeDtypeStruct((n, d), F32),
        grid=(n // tm,),
        in_specs=[pl.BlockSpec((tm, d), lambda i: (i, 0)), mix, mix, mix, mix,
                  pl.BlockSpec(w.shape, lambda i: (0, 0))],
        out_specs=pl.BlockSpec((tm, d), lambda i: (i, 0)),
        compiler_params=_cparams(("parallel",)),
        name="out_proj",
    )(x, oa, ob, oc, od, w)


def _attn_init(q_ref, qm_ref):
    qt = q_ref[...].astype(F32).T
    unit = lax.broadcasted_iota(jnp.int32, qt.shape, 0) // HEAD_DIM
    for u in range(qm_ref.shape[0]):
        qm_ref[u] = jnp.where(unit == u, qt, 0.0).astype(BF16)


def _scores(k, qm_ref, u):
    return jnp.dot(k, qm_ref[u], preferred_element_type=F32)


def _attn_units(k, qm_ref, update, st0_ref=None, k_next=None, from_st0=True):
    n_units = qm_ref.shape[0]
    st_next = st0_ref[...] if (st0_ref is not None and from_st0) else _scores(k, qm_ref, 0)
    for u in range(n_units):
        st = st_next
        if u + 1 < n_units:
            st_next = _scores(k, qm_ref, u + 1)
        elif st0_ref is not None:
            st0_ref[...] = _scores(k_next, qm_ref, 0)
        update(u, st)


def _tile_first(st, vt, u, m_ref, acc_ref, bias, shift, *, masked):
    if bias is not None:
        st = st + bias
    m_tile = jnp.max(st if masked else st[:8], axis=0, keepdims=True)
    m_ref[u] = m_tile if shift is None else m_tile + shift
    acc_ref[u] = jnp.dot(vt, jnp.exp2(st - m_tile).astype(BF16), preferred_element_type=F32)


def _tile_fixed(st, vt, u, m_ref, acc_ref, bias, shift):
    if bias is not None:
        st = st + bias
    sub = m_ref[u] if shift is None else m_ref[u] - shift
    acc_ref[u] += jnp.dot(vt, jnp.exp2(st - sub).astype(BF16), preferred_element_type=F32)


def _tile_online(st, vt, u, m_ref, acc_ref, bias, shift):
    if bias is not None:
        st = st + bias
    smax = jnp.max(st, axis=0, keepdims=True)
    if shift is not None:
        smax = smax + shift
    m_old = m_ref[u]
    m_new = jnp.maximum(m_old, smax)
    sub = m_new if shift is None else m_new - shift
    p = jnp.exp2(st - sub).astype(BF16)
    acc_ref[u] = jnp.exp2(m_old - m_new) * acc_ref[u] + jnp.dot(vt, p, preferred_element_type=F32)
    m_ref[u] = m_new


def _attn_finish(o_ref, acc_ref, dv, diff_refs, lam_init):
    n_units = acc_ref.shape[0]
    outs = [acc_ref[u, :dv, :] / acc_ref[u, dv:dv + 1, :] for u in range(n_units)]
    if diff_refs:
        lam_ref, subg_ref = diff_refs
        lp = lam_ref[...]
        lam = (jnp.exp(jnp.sum(lp[0:1] * lp[1:2], keepdims=True))
               - jnp.exp(jnp.sum(lp[2:3] * lp[3:4], keepdims=True)) + lam_init)
        outs = [outs[2 * h] - lam * outs[2 * h + 1] for h in range(n_units // 2)]
    o = jnp.concatenate(outs, axis=0).T
    if diff_refs:
        o = jnp.concatenate([_rms(o[:, lo:lo + dv], subg_ref[...]) * (1.0 - lam_init)
                             for lo in range(0, o.shape[1], dv)], axis=1)
    o_ref[...] = o.astype(BF16)


def _value_rows(u, n_units, rows_in_block, dv):
    per_head = dv + ONES_ROWS
    heads_in_block = rows_in_block // per_head
    lo = (u * heads_in_block // n_units) * per_head
    return lo, lo + per_head


PLAIN, FAR_LEFT, FAR_RIGHT, NEAR = "plain", 0, 1, "near"


def _attn_kernel(*refs, dv, reach, far, bias_index, lam_init, unroll):
    it = iter(refs)
    q_ref, k_ref, vt_ref = next(it), next(it), next(it)
    bias_ref = next(it) if reach is not None else None
    far_ref = next(it) if far else None
    diff_refs = (next(it), next(it)) if lam_init is not None else None
    o_ref, qm_ref, m_ref, acc_ref, st0_ref = next(it), next(it), next(it), next(it), next(it)

    i = pl.program_id(2)
    n_units = qm_ref.shape[0]
    nk, rows, tile = vt_ref.shape
    bias_heads = bias_ref.shape[1] if bias_ref is not None else 0
    _attn_init(q_ref, qm_ref)
    lo, hi = (0, nk) if reach is None else (jnp.maximum(i - reach, 0), jnp.minimum(i + reach + 1, nk))
    p_first, p_end = (1, nk) if (reach is None or far) else (lo + 1, hi)

    def tile_at(p):
        p = jnp.minimum(p, p_end - 1)
        return jnp.clip(p - (p <= i).astype(jnp.int32), 0, nk - 1)

    def keys(j):
        return k_ref[pl.ds(pl.multiple_of(j * tile, tile), tile), :]

    def updater(tile_fn, j, side):
        def update(u, st):
            r0, r1 = _value_rows(u, n_units, rows, dv)
            h = u * bias_heads // n_units
            bias = bias_ref[bias_index(i, j - i, nk), h] if side == NEAR else None
            shift = far_ref[side * bias_heads + h:side * bias_heads + h + 1, :] if side in (FAR_LEFT, FAR_RIGHT) else None
            tile_fn(st, vt_ref[j, r0:r1, :], u, m_ref, acc_ref, bias, shift)
        return update

    def visit_rest(tile_fn, pipelined):
        def span(p_lo, p_hi, side):
            def visit(p):
                j = tile_at(p)
                if pipelined:
                    _attn_units(keys(j), qm_ref, updater(tile_fn, j, side), st0_ref, keys(tile_at(p + 1)))
                else:
                    _attn_units(keys(j), qm_ref, updater(tile_fn, j, side))

            def single(p, carry):
                visit(p)
                return carry

            if not pipelined:
                lax.fori_loop(p_lo, p_hi, single, 0)
                return

            def group(t, carry):
                for s in range(unroll):
                    visit(p_lo + t * unroll + s)
                return carry
            n_groups = (p_hi - p_lo) // unroll
            lax.fori_loop(0, n_groups, group, 0)
            if isinstance(p_hi - p_lo, int):
                for p in range(p_lo + n_groups * unroll, p_hi):
                    visit(p)
            else:
                lax.fori_loop(p_lo + n_groups * unroll, p_hi, single, 0)
        if reach is None:
            span(1, nk, PLAIN)
        elif far:
            span(1, lo + 1, FAR_LEFT)
            span(lo + 1, hi, NEAR)
            span(hi, nk, FAR_RIGHT)
        else:
            span(lo + 1, hi, NEAR)

    diag = PLAIN if reach is None else NEAR
    first = functools.partial(_tile_first, masked=reach is not None and not far)
    _attn_units(keys(i), qm_ref, updater(first, i, diag), st0_ref, keys(tile_at(p_first)), from_st0=False)
    visit_rest(_tile_fixed, pipelined=True)

    finite = jnp.min(jnp.where(jnp.isfinite(acc_ref[...]), 1.0, 0.0))

    @pl.when(finite < 0.5)
    def _():
        m_ref[...] = jnp.full(m_ref.shape, -jnp.inf, F32)
        acc_ref[...] = jnp.zeros(acc_ref.shape, F32)
        _attn_units(keys(i), qm_ref, updater(_tile_online, i, diag))
        visit_rest(_tile_online, pipelined=False)

    _attn_finish(o_ref, acc_ref, dv, diff_refs, lam_init)


def _attention(q, k, vt, *, t_len, tile, qcol, kcol, bias=None, reach=None, bias_index=None, far=None, diff=None,
               unroll=1):
    n = q.shape[0]
    batch, nt = n // t_len, t_len // tile
    dv = B_V_DIM if diff else HEAD_DIM
    rows = vt.shape[3]
    once = pl.Buffered(1)
    in_specs = [
        pl.BlockSpec((tile, MXU_DIM), lambda b, g, i: (b * nt + i, qcol + g)),
        pl.BlockSpec((t_len, MXU_DIM), lambda b, g, i: (b, kcol + g), pipeline_mode=once),
        pl.BlockSpec((None, None) + vt.shape[2:], lambda b, g, i: (b, g, 0, 0, 0), pipeline_mode=once),
    ]
    args = [q, k, vt]
    if bias is not None:
        in_specs.append(pl.BlockSpec((bias.shape[0], None, bias.shape[2], tile, tile),
                                     lambda b, g, i: (0, g, 0, 0, 0), pipeline_mode=once))
        args.append(bias)
    if far is not None:
        in_specs.append(pl.BlockSpec((None,) + far.shape[1:], lambda b, g, i: (g, 0, 0)))
        args.append(far)
    lam_init = None
    if diff:
        lam_p, subg, lam_init = diff
        in_specs += [pl.BlockSpec(lam_p.shape, lambda b, g, i: (0, 0)),
                     pl.BlockSpec(subg.shape, lambda b, g, i: (0, 0))]
        args += [lam_p, subg]
    return pl.pallas_call(
        functools.partial(_attn_kernel, dv=dv, reach=reach, far=far is not None, bias_index=bias_index,
                          lam_init=lam_init, unroll=unroll),
        out_shape=jax.ShapeDtypeStruct((n, GROUP_WIDTH), BF16),
        grid=(batch, GROUPS, nt),
        in_specs=in_specs,
        out_specs=pl.BlockSpec((tile, MXU_DIM), lambda b, g, i: (b * nt + i, g)),
        scratch_shapes=[pltpu.VMEM((HEADS_PER_GROUP, MXU_DIM, tile), BF16),
                        pltpu.VMEM((HEADS_PER_GROUP, 1, tile), F32),
                        pltpu.VMEM((HEADS_PER_GROUP, dv + ONES_ROWS, tile), F32),
                        pltpu.VMEM((vt.shape[4], tile), F32)],
        compiler_params=_cparams(("parallel", "parallel", "arbitrary")),
        name="attention",
    )(*args)


def _t5_bucket(rel):
    nb = T5_BUCKETS // 2
    max_exact = nb // 2
    side = (rel > 0).astype(jnp.int32) * nb
    n = jnp.abs(rel)
    large = max_exact + (jnp.log(jnp.maximum(n, 1).astype(F32) / max_exact)
                         / math.log(T5_MAX_DIST / max_exact) * (nb - max_exact)).astype(jnp.int32)
    large = jnp.minimum(large, nb - 1)
    return side + jnp.where(n < max_exact, n, large)


def _skew(v, n):
    lead = v.shape[:-1]
    r = jnp.concatenate([v[..., ::-1], jnp.zeros(lead + (1,), v.dtype)], axis=-1)
    x = jnp.broadcast_to(r[..., None, :], lead + (n, r.shape[-1]))
    k = np.arange(n)[:, None]
    for b in range(n.bit_length() - 1):
        x = jnp.where(jnp.asarray((k >> b) & 1 == 1), jnp.roll(x, 1 << b, axis=-1), x)
    return x


def _toeplitz(v, n):
    return _skew(v, n)[..., n - 1:2 * n - 1]


def _rel_tiles(table, reach, n):
    x = _skew(table, n)
    return jnp.stack([x[..., (2 * reach + 1 - o) * n - 1:(2 * reach + 2 - o) * n - 1]
                      for o in range(2 * reach + 1)])


def _group_heads(b):
    t, h = b.shape[:2]
    return b.reshape(t, GROUPS, h // GROUPS, *b.shape[2:])


def _t5_saturation():
    nb = T5_BUCKETS // 2
    max_exact = nb // 2
    return math.ceil(max_exact * (T5_MAX_DIST / max_exact) ** ((nb - 1 - max_exact) / (nb - max_exact))) + 1


def _bias_b(t5_b):
    n, reach = ATT_T, B_NEAR
    assert reach * n + 1 >= _t5_saturation()
    rel = jnp.arange(-(reach * n + n - 1), reach * n + n, dtype=jnp.int32)
    table = (t5_b[_t5_bucket(rel)].astype(F32) * LOG2E).T
    far = (t5_b[_t5_bucket(jnp.array([-T5_MAX_DIST * 2, T5_MAX_DIST * 2], jnp.int32))].astype(F32) * LOG2E)
    heads = t5_b.shape[1]
    far = far.reshape(2, GROUPS, heads // GROUPS).transpose(1, 0, 2).reshape(GROUPS, 2 * heads // GROUPS, 1)
    return _group_heads(_rel_tiles(table, reach, n)), jnp.broadcast_to(far, far.shape[:2] + (n,))


def _bias_d(t5_d):
    n = ATT_T
    reach = max(w // 2 for w, _ in D_BRANCHES) // n
    rel = jnp.arange(-(reach * n + n - 1), reach * n + n, dtype=jnp.int32)
    mult = sum(((jnp.abs(rel) <= w // 2) & (rel % d == 0)).astype(F32) for w, d in D_BRANCHES)
    b = t5_d[_t5_bucket(rel)].astype(F32).T
    table = jnp.where(mult[None] > 0, (b + jnp.log(jnp.maximum(mult, 1.0))[None]) * LOG2E, NEG_INF)
    return _group_heads(_rel_tiles(table, reach, n)), reach


def _bias_c(rpb, rows):
    heads = rpb.shape[0]
    kr = min(C_WIN_ROWS, rows)
    n_row_tiles = rows // C_ROWS_PER_TILE
    pad = GRID_W - C_WIN_COLS
    col_tiles = _toeplitz(jnp.pad(rpb.astype(F32) * LOG2E, ((0, 0), (0, 0), (pad, pad))), GRID_W)
    c = np.arange(GRID_W)[None, :]
    kc = np.arange(GRID_W)[:, None]
    cs = np.clip(c - C_WIN_COLS // 2, 0, GRID_W - C_WIN_COLS)
    col_ok = jnp.asarray((kc >= cs) & (kc < cs + C_WIN_COLS))
    col_tiles = jnp.where(col_ok, col_tiles, NEG_INF)
    masked = jnp.full((heads, GRID_W, GRID_W), NEG_INF, F32)
    tiles = []
    for it in (0, min(1, n_row_tiles - 1), n_row_tiles - 1):
        for off in (-1, 0, 1):
            key_rows = []
            for a in range(C_ROWS_PER_TILE):
                blocks = []
                for b in range(C_ROWS_PER_TILE):
                    r = it * C_ROWS_PER_TILE + b
                    key_r = (it + off) * C_ROWS_PER_TILE + a
                    rs = min(max(r - kr // 2, 0), rows - kr)
                    ok = 0 <= key_r < rows and rs <= key_r < rs + kr
                    blocks.append(col_tiles[:, key_r - r + C_WIN_ROWS - 1] if ok else masked)
                key_rows.append(jnp.concatenate(blocks, axis=2))
            tiles.append(jnp.concatenate(key_rows, axis=1))
    return _group_heads(jnp.stack(tiles))


def _c_tile_index(i, off, nt):
    variant = jnp.where(i == 0, 0, jnp.where(i == nt - 1, 2, 1))
    return variant * 3 + off + 1


def _rope_tables(t_len):
    t = jnp.arange(t_len, dtype=jnp.int32)
    n_freq = HEAD_DIM // 4
    inv_freq = ROPE_THETA ** (-jnp.arange(n_freq, dtype=F32) / n_freq)
    ang = jnp.concatenate([(t // GRID_W).astype(F32)[:, None] * inv_freq[None, :],
                           (t % GRID_W).astype(F32)[:, None] * inv_freq[None, :]], axis=-1)
    cos, sin = jnp.cos(ang), jnp.sin(ang)
    reps = LANES // HEAD_DIM
    return (jnp.tile(jnp.concatenate([cos, cos], axis=-1), (1, reps)),
            jnp.tile(jnp.concatenate([-sin, sin], axis=-1), (1, reps)))


def _deinterleave(w):
    return jnp.swapaxes(w.reshape(w.shape[:-1] + (HEAD_DIM // 2, 2)), -1, -2).reshape(w.shape)


def _swap_halves(w):
    return jnp.concatenate([w[..., HEAD_DIM // 2:], w[..., :HEAD_DIM // 2]], axis=-1)


def _prep_ffn(w_gate, w_up, w_down):
    return w_gate.astype(BF16), w_up.astype(BF16), w_down.astype(BF16)


def _prep_mixer_a(w_in, g_q, g_k):
    d = w_in.shape[0]
    qw = A_HEADS * HEAD_DIM
    kw = A_KV_HEADS * HEAD_DIM
    rep = A_HEADS // A_KV_HEADS
    wq = _deinterleave(w_in[:, :qw].reshape(d, A_HEADS, HEAD_DIM))
    wk = _deinterleave(w_in[:, qw:qw + kw].reshape(d, A_KV_HEADS, 1, HEAD_DIM))
    wk = jnp.broadcast_to(wk, (d, A_KV_HEADS, rep, HEAD_DIM)).reshape(d, A_HEADS, HEAD_DIM)
    wqk = jnp.concatenate([wq, wk], axis=1)
    wv = w_in[:, qw + kw:qw + 2 * kw]
    w = jnp.concatenate([wqk.reshape(d, -1), wv], axis=1).astype(BF16)
    gq, gk = _deinterleave(g_q) * (SCALE * LOG2E), _deinterleave(g_k)
    g = jnp.concatenate([jnp.tile(gq, A_HEADS), jnp.tile(gk, A_HEADS)])
    gsw = jnp.concatenate([jnp.tile(_swap_halves(gq), A_HEADS), jnp.tile(_swap_halves(gk), A_HEADS)])
    return w, g[None, :].astype(F32), gsw[None, :].astype(F32)


def _prep_mixers_bcd(w_in):
    a_cols = (A_HEADS + 2 * A_KV_HEADS) * HEAD_DIM
    w = w_in[:, a_cols:]
    is_q = (np.arange(w.shape[1]) // GROUP_WIDTH) % 3 == 0
    return (w * jnp.asarray(np.where(is_q, SCALE * LOG2E, 1.0), F32)[None, :]).astype(BF16)


def _head_sum_matrix():
    idx = np.arange(MXU_DIM) // HEAD_DIM
    return jnp.asarray(idx[:, None] == idx[None, :], dtype=BF16)


def _trunk(x3, layers, bias_b, far_b, bias_d, d_reach, final_norm):
    batch, t_len, d = x3.shape
    x = x3.reshape(batch * t_len, d)
    cos, sin = _rope_tables(t_len)
    e = _head_sum_matrix()
    depth = len(layers)
    for l, p in enumerate(layers):
        x = _ffn(x, p["ffn1_norm"], *p["ffn1"], final_norm, final=False)
        qa, ka, vta = _aproj(x, p["mix_norm"], p["wa"], e, p["ga"], p["ga_sw"], cos, sin, t_len)
        qk, vtb, vtc, vtd = _proj(x, p["mix_norm"], p["w_bcd"], t_len)

        oa = _attention(qa, ka, vta, t_len=t_len, tile=ATT_T, qcol=0, kcol=0, unroll=ATT_UNROLL)
        ob = _attention(qk, qk, vtb, t_len=t_len, tile=ATT_T, qcol=0, kcol=GROUPS,
                        bias=bias_b, reach=B_NEAR, bias_index=lambda i, off, nt: off + B_NEAR, far=far_b,
                        diff=(p["b_lambda"], p["b_subln"], 0.8 - 0.6 * math.exp(-0.3 * l)), unroll=ATT_UNROLL)
        oc = _attention(qk, qk, vtc, t_len=t_len, tile=C_T, qcol=2 * GROUPS, kcol=3 * GROUPS,
                        bias=_bias_c(p["c_rpb"], t_len // GRID_W), reach=1, bias_index=_c_tile_index, unroll=C_UNROLL)
        od = _attention(qk, qk, vtd, t_len=t_len, tile=ATT_T, qcol=4 * GROUPS, kcol=5 * GROUPS,
                        bias=bias_d, reach=d_reach, bias_index=lambda i, off, nt: off + d_reach, unroll=ATT_UNROLL)
        x = _outproj(x, oa, ob, oc, od, p["w_out"])
        x = _ffn(x, p["ffn2_norm"], *p["ffn2"], final_norm, final=(l == depth - 1))
    return x.reshape(batch, t_len, d)


def kernel(x_prompt, x_sample, ffn1_norm, ffn1_w_gate, ffn1_w_up, ffn1_w_down, mix_norm, w_in, a_q_norm, a_k_norm, b_lambda, b_subln, c_rpb, w_out, ffn2_norm, ffn2_w_gate, ffn2_w_up, ffn2_w_down, t5_table, final_norm):
    depth = w_in.shape[0]
    layers = []
    for l in range(depth):
        wa, ga, ga_sw = _prep_mixer_a(w_in[l], a_q_norm[l], a_k_norm[l])
        layers.append(dict(
            ffn1_norm=ffn1_norm[l][None, :], ffn2_norm=ffn2_norm[l][None, :], mix_norm=mix_norm[l][None, :],
            ffn1=_prep_ffn(ffn1_w_gate[l], ffn1_w_up[l], ffn1_w_down[l]),
            ffn2=_prep_ffn(ffn2_w_gate[l], ffn2_w_up[l], ffn2_w_down[l]),
            wa=wa, ga=ga, ga_sw=ga_sw,
            w_bcd=_prep_mixers_bcd(w_in[l]),
            b_lambda=b_lambda[l].astype(F32), b_subln=b_subln[l][None, :].astype(F32),
            c_rpb=c_rpb[l], w_out=w_out[l].astype(BF16)))
    bias_b, far_b = _bias_b(t5_table[:, :B_HEADS])
    bias_d, d_reach = _bias_d(t5_table[:, B_HEADS:])
    fn = final_norm[None, :]
    return tuple(_trunk(x3, layers, bias_b, far_b, bias_d, d_reach, fn) for x3 in (x_prompt, x_sample))
```

```python
import functools
import math

import jax
import jax.numpy as jnp
import numpy as np
from jax import lax
from jax.experimental import pallas as pl
from jax.experimental.pallas import tpu as pltpu

F32 = jnp.float32
BF16 = jnp.bfloat16

HEAD_DIM = 64
GRID_W = 64
EPS = 1e-6
NEG_INF = -1e30
SCALE = HEAD_DIM ** -0.5
LOG2E = math.log2(math.e)
ROPE_THETA = 10000.0
A_HEADS, A_KV_HEADS = 8, 2
B_HEADS, B_V_DIM = 4, 128
C_HEADS, C_WIN_ROWS, C_WIN_COLS = 8, 8, 16
D_HEADS = 8
D_BRANCHES = ((128, 1), (512, 4), (2048, 16))
T5_BUCKETS, T5_MAX_DIST = 32, 1024
GROUP_WIDTH = 512

LANES = 128
BF16_SUBLANES = 16
MXU_DIM = 256
HEADS_PER_GROUP = MXU_DIM // HEAD_DIM
GROUPS = GROUP_WIDTH // MXU_DIM
VMEM_LIMIT = 56 * 1024 * 1024
FFN_TM, FFN_TF = 1024, 512
PROJ_TM, PROJ_TN = 1024, 1536
APROJ_TM = 512
OUT_TM = 512
ATT_T = 512
C_ROWS_PER_TILE = 4
C_T = C_ROWS_PER_TILE * GRID_W
ONES_ROWS = BF16_SUBLANES
ATT_UNROLL = 4
C_UNROLL = 2
B_NEAR = 2


def _cparams(sem):
    return pltpu.CompilerParams(dimension_semantics=sem, vmem_limit_bytes=VMEM_LIMIT)


def _rms(x, g):
    return x * lax.rsqrt(jnp.mean(x * x, axis=-1, keepdims=True) + EPS) * g


def _ffn_kernel(x_ref, g_ref, wg_ref, wu_ref, wd_ref, fg_ref, o_ref, xn_ref, *, final):
    j = pl.program_id(1)

    @pl.when(j == 0)
    def _():
        x = x_ref[...]
        xn_ref[...] = _rms(x, g_ref[...]).astype(BF16)
        o_ref[...] = x

    h = jnp.dot(xn_ref[...], jnp.concatenate([wg_ref[...], wu_ref[...]], axis=1), preferred_element_type=F32)
    tf = h.shape[1] // 2
    hg, hu = h[:, :tf], h[:, tf:]
    a = (hg * jax.nn.sigmoid(hg)) * hu
    o_ref[...] += jnp.dot((0.5 * a).astype(BF16), wd_ref[...], preferred_element_type=F32)

    if final:
        @pl.when(j == pl.num_programs(1) - 1)
        def _():
            o_ref[...] = _rms(o_ref[...], fg_ref[...])


def _ffn(x, g, wg, wu, wd, fg, *, final):
    n, d = x.shape
    dff = wd.shape[0]
    tm, tf = min(FFN_TM, n), FFN_TF
    return pl.pallas_call(
        functools.partial(_ffn_kernel, final=final),
        out_shape=jax.ShapeDtypeStruct((n, d), F32),
        grid=(n // tm, dff // tf),
        in_specs=[
            pl.BlockSpec((tm, d), lambda i, j: (i, 0)),
            pl.BlockSpec((1, d), lambda i, j: (0, 0)),
            pl.BlockSpec((d, tf), lambda i, j: (0, j)),
            pl.BlockSpec((d, tf), lambda i, j: (0, j)),
            pl.BlockSpec((tf, d), lambda i, j: (j, 0)),
            pl.BlockSpec((1, d), lambda i, j: (0, 0)),
        ],
        out_specs=pl.BlockSpec((tm, d), lambda i, j: (i, 0)),
        scratch_shapes=[pltpu.VMEM((tm, d), BF16)],
        compiler_params=_cparams(("parallel", "arbitrary")),
        name="ffn",
    )(x, g, wg, wu, wd, fg)


def _store_values_t(vt_ref, v, dv):
    groups, n_tiles, rows, tile = vt_ref.shape
    per_head = dv + ONES_ROWS
    ones = jnp.ones((ONES_ROWS, tile), BF16)
    heads_per_group = rows // per_head
    for t in range(n_tiles):
        vt = v[t * tile:(t + 1) * tile, :].T
        for g in range(groups):
            for h in range(heads_per_group):
                src = (g * heads_per_group + h) * dv
                vt_ref[g, t, h * per_head:h * per_head + dv, :] = vt[src:src + dv, :].astype(BF16)
                vt_ref[g, t, h * per_head + dv:(h + 1) * per_head, :] = ones


def _proj_kernel(x_ref, g_ref, w_ref, o_ref, vtb_ref, vtc_ref, vtd_ref, xn_ref):
    j = pl.program_id(1)

    @pl.when(j == 0)
    def _():
        xn_ref[...] = _rms(x_ref[...], g_ref[...]).astype(BF16)

    y = jnp.dot(xn_ref[...], w_ref[...], preferred_element_type=F32)
    nqk = o_ref.shape[1]
    o_ref[...] = y[:, :nqk].astype(BF16)
    for m, (vt_ref, dv) in enumerate(((vtb_ref, B_V_DIM), (vtc_ref, HEAD_DIM), (vtd_ref, HEAD_DIM))):
        @pl.when(j == m)
        def _():
            _store_values_t(vt_ref, y[:, nqk:], dv)


def _values_shape(batch, t_len, heads, dv, tile):
    return (batch, GROUPS, t_len // tile, heads // GROUPS * (dv + ONES_ROWS), tile)


def _proj(x, g, w, t_len):
    n, d = x.shape
    batch = n // t_len
    tm, tn = math.gcd(PROJ_TM, t_len), PROJ_TN
    npos = t_len // tm
    nqk = 2 * GROUP_WIDTH
    vt_shapes = [_values_shape(batch, t_len, B_HEADS, B_V_DIM, ATT_T),
                 _values_shape(batch, t_len, C_HEADS, HEAD_DIM, C_T),
                 _values_shape(batch, t_len, D_HEADS, HEAD_DIM, ATT_T)]
    vt_specs = [pl.BlockSpec((None, GROUPS, tm // s[4], s[3], s[4]), lambda i, j: (i // npos, 0, i % npos, 0, 0))
                for s in vt_shapes]
    return pl.pallas_call(
        _proj_kernel,
        out_shape=[jax.ShapeDtypeStruct((n, 3 * nqk), BF16)] + [jax.ShapeDtypeStruct(s, BF16) for s in vt_shapes],
        grid=(n // tm, w.shape[1] // tn),
        in_specs=[
            pl.BlockSpec((tm, d), lambda i, j: (i, 0)),
            pl.BlockSpec((1, d), lambda i, j: (0, 0)),
            pl.BlockSpec((d, tn), lambda i, j: (0, j)),
        ],
        out_specs=[pl.BlockSpec((tm, nqk), lambda i, j: (i, j))] + vt_specs,
        scratch_shapes=[pltpu.VMEM((tm, d), BF16)],
        compiler_params=_cparams(("parallel", "arbitrary")),
        name="proj_bcd",
    )(x, g, w)


def _aproj_kernel(x_ref, g_ref, w_ref, e_ref, gc_ref, gs_ref, cos_ref, sin_ref, q_ref, k_ref, vt_ref):
    xn = _rms(x_ref[...], g_ref[...]).astype(BF16)
    y = jnp.dot(xn, w_ref[...], preferred_element_type=F32)
    nqk = gc_ref.shape[1]
    _store_values_t(vt_ref, y[:, nqk:], HEAD_DIM)
    e = e_ref[...]
    cos = jnp.concatenate([cos_ref[...]] * (MXU_DIM // LANES), axis=1)
    sin = jnp.concatenate([sin_ref[...]] * (MXU_DIM // LANES), axis=1)
    first_half = lax.broadcasted_iota(jnp.int32, (y.shape[0], MXU_DIM), 1) % HEAD_DIM < HEAD_DIM // 2
    half = nqk // 2
    for lo in range(0, nqk, MXU_DIM):
        yc = y[:, lo:lo + MXU_DIM]
        partner = jnp.where(first_half, pltpu.roll(yc, MXU_DIM - HEAD_DIM // 2, axis=1),
                            pltpu.roll(yc, HEAD_DIM // 2, axis=1))
        sq = yc * yc
        hi = sq.astype(BF16)
        rest = (sq - hi.astype(F32)).astype(BF16)
        ssum = (jnp.dot(hi, e, preferred_element_type=F32)
                + jnp.dot(rest, e, preferred_element_type=F32))
        r = lax.rsqrt(ssum * (1.0 / HEAD_DIM) + EPS)
        out = (yc * gc_ref[:, lo:lo + MXU_DIM] * cos
               + partner * gs_ref[:, lo:lo + MXU_DIM] * sin) * r
        dst, off = (q_ref, lo) if lo < half else (k_ref, lo - half)
        dst[:, off:off + MXU_DIM] = out.astype(BF16)


def _aproj(x, g, w, e, gc, gs, cos, sin, t_len):
    n, d = x.shape
    tm = math.gcd(APROJ_TM, t_len)
    npos = t_len // tm
    nqk = gc.shape[1]
    const = lambda i: (0, 0)
    vt_shape = _values_shape(n // t_len, t_len, A_KV_HEADS, HEAD_DIM, ATT_T)
    return pl.pallas_call(
        _aproj_kernel,
        out_shape=(jax.ShapeDtypeStruct((n, nqk // 2), BF16),
                   jax.ShapeDtypeStruct((n, nqk // 2), BF16),
                   jax.ShapeDtypeStruct(vt_shape, BF16)),
        grid=(n // tm,),
        in_specs=[
            pl.BlockSpec((tm, d), lambda i: (i, 0)),
            pl.BlockSpec((1, d), const),
            pl.BlockSpec(w.shape, const),
            pl.BlockSpec(e.shape, const),
            pl.BlockSpec(gc.shape, const),
            pl.BlockSpec(gs.shape, const),
            pl.BlockSpec((tm, LANES), lambda i: (i % npos, 0)),
            pl.BlockSpec((tm, LANES), lambda i: (i % npos, 0)),
        ],
        out_specs=(pl.BlockSpec((tm, nqk // 2), lambda i: (i, 0)),
                   pl.BlockSpec((tm, nqk // 2), lambda i: (i, 0)),
                   pl.BlockSpec((None, GROUPS, tm // ATT_T) + vt_shape[3:],
                                lambda i: (i // npos, 0, i % npos, 0, 0))),
        compiler_params=_cparams(("parallel",)),
        name="proj_a",
    )(x, g, w, e, gc, gs, cos, sin)


def _outproj_kernel(x_ref, oa_ref, ob_ref, oc_ref, od_ref, w_ref, o_ref):
    acc = x_ref[...]
    for m, ref in enumerate((oa_ref, ob_ref, oc_ref, od_ref)):
        acc = acc + jnp.dot(ref[...], w_ref[m * GROUP_WIDTH:(m + 1) * GROUP_WIDTH, :],
                            preferred_element_type=F32)
    o_ref[...] = acc


def _outproj(x, oa, ob, oc, od, w):
    n, d = x.shape
    tm = min(OUT_TM, n)
    mix = pl.BlockSpec((tm, GROUP_WIDTH), lambda i: (i, 0))
    return pl.pallas_call(
        _outproj_kernel,
        out_shape=jax.ShapeDtypeStruct((n, d), F32),
        grid=(n // tm,),
        in_specs=[pl.BlockSpec((tm, d), lambda i: (i, 0)), mix, mix, mix, mix,
                  pl.BlockSpec(w.shape, lambda i: (0, 0))],
        out_specs=pl.BlockSpec((tm, d), lambda i: (i, 0)),
        compiler_params=_cparams(("parallel",)),
        name="out_proj",
    )(x, oa, ob, oc, od, w)


def _attn_init(q_ref, qm_ref):
    qt = q_ref[...].astype(F32).T
    unit = lax.broadcasted_iota(jnp.int32, qt.shape, 0) // HEAD_DIM
    for u in range(qm_ref.shape[0]):
        qm_ref[u] = jnp.where(unit == u, qt, 0.0).astype(BF16)


def _scores(k, qm_ref, u):
    return jnp.dot(k, qm_ref[u], preferred_element_type=F32)


def _attn_units(k, qm_ref, update, st0_ref=None, k_next=None, from_st0=True):
    n_units = qm_ref.shape[0]
    st_next = st0_ref[...] if (st0_ref is not None and from_st0) else _scores(k, qm_ref, 0)
    for u in range(n_units):
        st = st_next
        if u + 1 < n_units:
            st_next = _scores(k, qm_ref, u + 1)
        elif st0_ref is not None:
            st0_ref[...] = _scores(k_next, qm_ref, 0)
        update(u, st)


def _tile_first(st, vt, u, m_ref, acc_ref, bias, shift, *, masked):
    if bias is not None:
        st = st + bias
    m_tile = jnp.max(st if masked else st[:8], axis=0, keepdims=True)
    m_ref[u] = m_tile if shift is None else m_tile + shift
    acc_ref[u] = jnp.dot(vt, jnp.exp2(st - m_tile).astype(BF16), preferred_element_type=F32)


def _tile_fixed(st, vt, u, m_ref, acc_ref, bias, shift):
    if bias is not None:
        st = st + bias
    sub = m_ref[u] if shift is None else m_ref[u] - shift
    acc_ref[u] += jnp.dot(vt, jnp.exp2(st - sub).astype(BF16), preferred_element_type=F32)


def _tile_online(st, vt, u, m_ref, acc_ref, bias, shift):
    if bias is not None:
        st = st + bias
    smax = jnp.max(st, axis=0, keepdims=True)
    if shift is not None:
        smax = smax + shift
    m_old = m_ref[u]
    m_new = jnp.maximum(m_old, smax)
    sub = m_new if shift is None else m_new - shift
    p = jnp.exp2(st - sub).astype(BF16)
    acc_ref[u] = jnp.exp2(m_old - m_new) * acc_ref[u] + jnp.dot(vt, p, preferred_element_type=F32)
    m_ref[u] = m_new


def _attn_finish(o_ref, acc_ref, dv, diff_refs, lam_init):
    n_units = acc_ref.shape[0]
    outs = [acc_ref[u, :dv, :] / acc_ref[u, dv:dv + 1, :] for u in range(n_units)]
    if diff_refs:
        lam_ref, subg_ref = diff_refs
        lp = lam_ref[...]
        lam = (jnp.exp(jnp.sum(lp[0:1] * lp[1:2], keepdims=True))
               - jnp.exp(jnp.sum(lp[2:3] * lp[3:4], keepdims=True)) + lam_init)
        outs = [outs[2 * h] - lam * outs[2 * h + 1] for h in range(n_units // 2)]
    o = jnp.concatenate(outs, axis=0).T
    if diff_refs:
        o = jnp.concatenate([_rms(o[:, lo:lo + dv], subg_ref[...]) * (1.0 - lam_init)
                             for lo in range(0, o.shape[1], dv)], axis=1)
    o_ref[...] = o.astype(BF16)


def _value_rows(u, n_units, rows_in_block, dv):
    per_head = dv + ONES_ROWS
    heads_in_block = rows_in_block // per_head
    lo = (u * heads_in_block // n_units) * per_head
    return lo, lo + per_head


PLAIN, FAR_LEFT, FAR_RIGHT, NEAR = "plain", 0, 1, "near"


def _attn_kernel(*refs, dv, reach, far, bias_index, lam_init, unroll):
    it = iter(refs)
    q_ref, k_ref, vt_ref = next(it), next(it), next(it)
    bias_ref = next(it) if reach is not None else None
    far_ref = next(it) if far else None
    diff_refs = (next(it), next(it)) if lam_init is not None else None
    o_ref, qm_ref, m_ref, acc_ref, st0_ref = next(it), next(it), next(it), next(it), next(it)

    i = pl.program_id(2)
    n_units = qm_ref.shape[0]
    nk, rows, tile = vt_ref.shape
    bias_heads = bias_ref.shape[1] if bias_ref is not None else 0
    _attn_init(q_ref, qm_ref)
    lo, hi = (0, nk) if reach is None else (jnp.maximum(i - reach, 0), jnp.minimum(i + reach + 1, nk))
    p_first, p_end = (1, nk) if (reach is None or far) else (lo + 1, hi)

    def tile_at(p):
        p = jnp.minimum(p, p_end - 1)
        return jnp.clip(p - (p <= i).astype(jnp.int32), 0, nk - 1)

    def keys(j):
        return k_ref[pl.ds(pl.multiple_of(j * tile, tile), tile), :]

    def updater(tile_fn, j, side):
        def update(u, st):
            r0, r1 = _value_rows(u, n_units, rows, dv)
            h = u * bias_heads // n_units
            bias = bias_ref[bias_index(i, j - i, nk), h] if side == NEAR else None
            shift = far_ref[side * bias_heads + h:side * bias_heads + h + 1, :] if side in (FAR_LEFT, FAR_RIGHT) else None
            tile_fn(st, vt_ref[j, r0:r1, :], u, m_ref, acc_ref, bias, shift)
        return update

    def visit_rest(tile_fn, pipelined):
        def span(p_lo, p_hi, side):
            def visit(p):
                j = tile_at(p)
                if pipelined:
                    _attn_units(keys(j), qm_ref, updater(tile_fn, j, side), st0_ref, keys(tile_at(p + 1)))
                else:
                    _attn_units(keys(j), qm_ref, updater(tile_fn, j, side))

            def single(p, carry):
                visit(p)
                return carry

            if not pipelined:
                lax.fori_loop(p_lo, p_hi, single, 0)
                return

            def group(t, carry):
                for s in range(unroll):
                    visit(p_lo + t * unroll + s)
                return carry
            n_groups = (p_hi - p_lo) // unroll
            lax.fori_loop(0, n_groups, group, 0)
            base = p_lo + n_groups * unroll
            if isinstance(p_hi - p_lo, int):
                for p in range(base, p_hi):
                    visit(p)
                return
            rem = p_hi - base
            size = unroll // 2
            while size >= 1:
                take = (rem & size) != 0

                @pl.when(take)
                def _(base=base, size=size):
                    for s in range(size):
                        visit(base + s)
                base = base + jnp.where(take, size, 0)
                size //= 2
        if reach is None:
            span(1, nk, PLAIN)
        elif far:
            span(1, lo + 1, FAR_LEFT)
            span(lo + 1, hi, NEAR)
            span(hi, nk, FAR_RIGHT)
        else:
            span(lo + 1, hi, NEAR)

    diag = PLAIN if reach is None else NEAR
    first = functools.partial(_tile_first, masked=reach is not None and not far)
    _attn_units(keys(i), qm_ref, updater(first, i, diag), st0_ref, keys(tile_at(p_first)), from_st0=False)
    visit_rest(_tile_fixed, pipelined=True)

    finite = jnp.min(jnp.where(jnp.isfinite(acc_ref[...]), 1.0, 0.0))

    @pl.when(finite < 0.5)
    def _():
        m_ref[...] = jnp.full(m_ref.shape, -jnp.inf, F32)
        acc_ref[...] = jnp.zeros(acc_ref.shape, F32)
        _attn_units(keys(i), qm_ref, updater(_tile_online, i, diag))
        visit_rest(_tile_online, pipelined=False)

    _attn_finish(o_ref, acc_ref, dv, diff_refs, lam_init)


def _attention(q, k, vt, *, t_len, tile, qcol, kcol, bias=None, reach=None, bias_index=None, far=None, diff=None,
               unroll=1):
    n = q.shape[0]
    batch, nt = n // t_len, t_len // tile
    dv = B_V_DIM if diff else HEAD_DIM
    rows = vt.shape[3]
    once = pl.Buffered(1)
    in_specs = [
        pl.BlockSpec((tile, MXU_DIM), lambda b, g, i: (b * nt + i, qcol + g)),
        pl.BlockSpec((t_len, MXU_DIM), lambda b, g, i: (b, kcol + g), pipeline_mode=once),
        pl.BlockSpec((None, None) + vt.shape[2:], lambda b, g, i: (b, g, 0, 0, 0), pipeline_mode=once),
    ]
    args = [q, k, vt]
    if bias is not None:
        in_specs.append(pl.BlockSpec((bias.shape[0], None, bias.shape[2], tile, tile),
                                     lambda b, g, i: (0, g, 0, 0, 0), pipeline_mode=once))
        args.append(bias)
    if far is not None:
        in_specs.append(pl.BlockSpec((None,) + far.shape[1:], lambda b, g, i: (g, 0, 0)))
        args.append(far)
    lam_init = None
    if diff:
        lam_p, subg, lam_init = diff
        in_specs += [pl.BlockSpec(lam_p.shape, lambda b, g, i: (0, 0)),
                     pl.BlockSpec(subg.shape, lambda b, g, i: (0, 0))]
        args += [lam_p, subg]
    return pl.pallas_call(
        functools.partial(_attn_kernel, dv=dv, reach=reach, far=far is not None, bias_index=bias_index,
                          lam_init=lam_init, unroll=unroll),
        out_shape=jax.ShapeDtypeStruct((n, GROUP_WIDTH), BF16),
        grid=(batch, GROUPS, nt),
        in_specs=in_specs,
        out_specs=pl.BlockSpec((tile, MXU_DIM), lambda b, g, i: (b * nt + i, g)),
        scratch_shapes=[pltpu.VMEM((HEADS_PER_GROUP, MXU_DIM, tile), BF16),
                        pltpu.VMEM((HEADS_PER_GROUP, 1, tile), F32),
                        pltpu.VMEM((HEADS_PER_GROUP, dv + ONES_ROWS, tile), F32),
                        pltpu.VMEM((vt.shape[4], tile), F32)],
        compiler_params=_cparams(("parallel", "parallel", "arbitrary")),
        name="attention",
    )(*args)


def _t5_bucket(rel):
    nb = T5_BUCKETS // 2
    max_exact = nb // 2
    side = (rel > 0).astype(jnp.int32) * nb
    n = jnp.abs(rel)
    large = max_exact + (jnp.log(jnp.maximum(n, 1).astype(F32) / max_exact)
                         / math.log(T5_MAX_DIST / max_exact) * (nb - max_exact)).astype(jnp.int32)
    large = jnp.minimum(large, nb - 1)
    return side + jnp.where(n < max_exact, n, large)


def _skew(v, n):
    lead = v.shape[:-1]
    r = jnp.concatenate([v[..., ::-1], jnp.zeros(lead + (1,), v.dtype)], axis=-1)
    x = jnp.broadcast_to(r[..., None, :], lead + (n, r.shape[-1]))
    k = np.arange(n)[:, None]
    for b in range(n.bit_length() - 1):
        x = jnp.where(jnp.asarray((k >> b) & 1 == 1), jnp.roll(x, 1 << b, axis=-1), x)
    return x


def _toeplitz(v, n):
    return _skew(v, n)[..., n - 1:2 * n - 1]


def _rel_tiles(table, reach, n):
    x = _skew(table, n)
    return jnp.stack([x[..., (2 * reach + 1 - o) * n - 1:(2 * reach + 2 - o) * n - 1]
                      for o in range(2 * reach + 1)])


def _group_heads(b):
    t, h = b.shape[:2]
    return b.reshape(t, GROUPS, h // GROUPS, *b.shape[2:])


def _t5_saturation():
    nb = T5_BUCKETS // 2
    max_exact = nb // 2
    return math.ceil(max_exact * (T5_MAX_DIST / max_exact) ** ((nb - 1 - max_exact) / (nb - max_exact))) + 1


def _bias_b(t5_b):
    n, reach = ATT_T, B_NEAR
    assert reach * n + 1 >= _t5_saturation()
    rel = jnp.arange(-(reach * n + n - 1), reach * n + n, dtype=jnp.int32)
    table = (t5_b[_t5_bucket(rel)].astype(F32) * LOG2E).T
    far = (t5_b[_t5_bucket(jnp.array([-T5_MAX_DIST * 2, T5_MAX_DIST * 2], jnp.int32))].astype(F32) * LOG2E)
    heads = t5_b.shape[1]
    far = far.reshape(2, GROUPS, heads // GROUPS).transpose(1, 0, 2).reshape(GROUPS, 2 * heads // GROUPS, 1)
    return _group_heads(_rel_tiles(table, reach, n)), jnp.broadcast_to(far, far.shape[:2] + (n,))


def _bias_d(t5_d):
    n = ATT_T
    reach = max(w // 2 for w, _ in D_BRANCHES) // n
    rel = jnp.arange(-(reach * n + n - 1), reach * n + n, dtype=jnp.int32)
    mult = sum(((jnp.abs(rel) <= w // 2) & (rel % d == 0)).astype(F32) for w, d in D_BRANCHES)
    b = t5_d[_t5_bucket(rel)].astype(F32).T
    table = jnp.where(mult[None] > 0, (b + jnp.log(jnp.maximum(mult, 1.0))[None]) * LOG2E, NEG_INF)
    return _group_heads(_rel_tiles(table, reach, n)), reach


def _bias_c(rpb, rows):
    heads = rpb.shape[0]
    kr = min(C_WIN_ROWS, rows)
    n_row_tiles = rows // C_ROWS_PER_TILE
    pad = GRID_W - C_WIN_COLS
    col_tiles = _toeplitz(jnp.pad(rpb.astype(F32) * LOG2E, ((0, 0), (0, 0), (pad, pad))), GRID_W)
    c = np.arange(GRID_W)[None, :]
    kc = np.arange(GRID_W)[:, None]
    cs = np.clip(c - C_WIN_COLS // 2, 0, GRID_W - C_WIN_COLS)
    col_ok = jnp.asarray((kc >= cs) & (kc < cs + C_WIN_COLS))
    col_tiles = jnp.where(col_ok, col_tiles, NEG_INF)
    masked = jnp.full((heads, GRID_W, GRID_W), NEG_INF, F32)
    tiles = []
    for it in (0, min(1, n_row_tiles - 1), n_row_tiles - 1):
        for off in (-1, 0, 1):
            key_rows = []
            for a in range(C_ROWS_PER_TILE):
                blocks = []
                for b in range(C_ROWS_PER_TILE):
                    r = it * C_ROWS_PER_TILE + b
                    key_r = (it + off) * C_ROWS_PER_TILE + a
                    rs = min(max(r - kr // 2, 0), rows - kr)
                    ok = 0 <= key_r < rows and rs <= key_r < rs + kr
                    blocks.append(col_tiles[:, key_r - r + C_WIN_ROWS - 1] if ok else masked)
                key_rows.append(jnp.concatenate(blocks, axis=2))
            tiles.append(jnp.concatenate(key_rows, axis=1))
    return _group_heads(jnp.stack(tiles))


def _c_tile_index(i, off, nt):
    variant = jnp.where(i == 0, 0, jnp.where(i == nt - 1, 2, 1))
    return variant * 3 + off + 1


def _rope_tables(t_len):
    t = jnp.arange(t_len, dtype=jnp.int32)
    n_freq = HEAD_DIM // 4
    inv_freq = ROPE_THETA ** (-jnp.arange(n_freq, dtype=F32) / n_freq)
    ang = jnp.concatenate([(t // GRID_W).astype(F32)[:, None] * inv_freq[None, :],
                           (t % GRID_W).astype(F32)[:, None] * inv_freq[None, :]], axis=-1)
    cos, sin = jnp.cos(ang), jnp.sin(ang)
    reps = LANES // HEAD_DIM
    return (jnp.tile(jnp.concatenate([cos, cos], axis=-1), (1, reps)),
            jnp.tile(jnp.concatenate([-sin, sin], axis=-1), (1, reps)))


def _deinterleave(w):
    return jnp.swapaxes(w.reshape(w.shape[:-1] + (HEAD_DIM // 2, 2)), -1, -2).reshape(w.shape)


def _swap_halves(w):
    return jnp.concatenate([w[..., HEAD_DIM // 2:], w[..., :HEAD_DIM // 2]], axis=-1)


def _prep_ffn(w_gate, w_up, w_down):
    return w_gate.astype(BF16), w_up.astype(BF16), w_down.astype(BF16)


def _prep_mixer_a(w_in, g_q, g_k):
    d = w_in.shape[0]
    qw = A_HEADS * HEAD_DIM
    kw = A_KV_HEADS * HEAD_DIM
    rep = A_HEADS // A_KV_HEADS
    wq = _deinterleave(w_in[:, :qw].reshape(d, A_HEADS, HEAD_DIM))
    wk = _deinterleave(w_in[:, qw:qw + kw].reshape(d, A_KV_HEADS, 1, HEAD_DIM))
    wk = jnp.broadcast_to(wk, (d, A_KV_HEADS, rep, HEAD_DIM)).reshape(d, A_HEADS, HEAD_DIM)
    wqk = jnp.concatenate([wq, wk], axis=1)
    wv = w_in[:, qw + kw:qw + 2 * kw]
    w = jnp.concatenate([wqk.reshape(d, -1), wv], axis=1).astype(BF16)
    gq, gk = _deinterleave(g_q) * (SCALE * LOG2E), _deinterleave(g_k)
    g = jnp.concatenate([jnp.tile(gq, A_HEADS), jnp.tile(gk, A_HEADS)])
    gsw = jnp.concatenate([jnp.tile(_swap_halves(gq), A_HEADS), jnp.tile(_swap_halves(gk), A_HEADS)])
    return w, g[None, :].astype(F32), gsw[None, :].astype(F32)


def _prep_mixers_bcd(w_in):
    a_cols = (A_HEADS + 2 * A_KV_HEADS) * HEAD_DIM
    w = w_in[:, a_cols:]
    is_q = (np.arange(w.shape[1]) // GROUP_WIDTH) % 3 == 0
    return (w * jnp.asarray(np.where(is_q, SCALE * LOG2E, 1.0), F32)[None, :]).astype(BF16)


def _head_sum_matrix():
    idx = np.arange(MXU_DIM) // HEAD_DIM
    return jnp.asarray(idx[:, None] == idx[None, :], dtype=BF16)


def _trunk(x3, layers, bias_b, far_b, bias_d, d_reach, final_norm):
    batch, t_len, d = x3.shape
    x = x3.reshape(batch * t_len, d)
    cos, sin = _rope_tables(t_len)
    e = _head_sum_matrix()
    depth = len(layers)
    for l, p in enumerate(layers):
        x = _ffn(x, p["ffn1_norm"], *p["ffn1"], final_norm, final=False)
        qa, ka, vta = _aproj(x, p["mix_norm"], p["wa"], e, p["ga"], p["ga_sw"], cos, sin, t_len)
        qk, vtb, vtc, vtd = _proj(x, p["mix_norm"], p["w_bcd"], t_len)

        oa = _attention(qa, ka, vta, t_len=t_len, tile=ATT_T, qcol=0, kcol=0, unroll=ATT_UNROLL)
        ob = _attention(qk, qk, vtb, t_len=t_len, tile=ATT_T, qcol=0, kcol=GROUPS,
                        bias=bias_b, reach=B_NEAR, bias_index=lambda i, off, nt: off + B_NEAR, far=far_b,
                        diff=(p["b_lambda"], p["b_subln"], 0.8 - 0.6 * math.exp(-0.3 * l)), unroll=ATT_UNROLL)
        oc = _attention(qk, qk, vtc, t_len=t_len, tile=C_T, qcol=2 * GROUPS, kcol=3 * GROUPS,
                        bias=_bias_c(p["c_rpb"], t_len // GRID_W), reach=1, bias_index=_c_tile_index, unroll=C_UNROLL)
        od = _attention(qk, qk, vtd, t_len=t_len, tile=ATT_T, qcol=4 * GROUPS, kcol=5 * GROUPS,
                        bias=bias_d, reach=d_reach, bias_index=lambda i, off, nt: off + d_reach, unroll=ATT_UNROLL)
        x = _outproj(x, oa, ob, oc, od, p["w_out"])
        x = _ffn(x, p["ffn2_norm"], *p["ffn2"], final_norm, final=(l == depth - 1))
    return x.reshape(batch, t_len, d)


def kernel(x_prompt, x_sample, ffn1_norm, ffn1_w_gate, ffn1_w_up, ffn1_w_down, mix_norm, w_in, a_q_norm, a_k_norm, b_lambda, b_subln, c_rpb, w_out, ffn2_norm, ffn2_w_gate, ffn2_w_up, ffn2_w_down, t5_table, final_norm):
    depth = w_in.shape[0]
    layers = []
    for l in range(depth):
        wa, ga, ga_sw = _prep_mixer_a(w_in[l], a_q_norm[l], a_k_norm[l])
        layers.append(dict(
            ffn1_norm=ffn1_norm[l][None, :], ffn2_norm=ffn2_norm[l][None, :], mix_norm=mix_norm[l][None, :],
            ffn1=_prep_ffn(ffn1_w_gate[l], ffn1_w_up[l], ffn1_w_down[l]),
            ffn2=_prep_ffn(ffn2_w_gate[l], ffn2_w_up[l], ffn2_w_down[l]),
            wa=wa, ga=ga, ga_sw=ga_sw,
            w_bcd=_prep_mixers_bcd(w_in[l]),
            b_lambda=b_lambda[l].astype(F32), b_subln=b_subln[l][None, :].astype(F32),
            c_rpb=c_rpb[l], w_out=w_out[l].astype(BF16)))
    bias_b, far_b = _bias_b(t5_table[:, :B_HEADS])
    bias_d, d_reach = _bias_d(t5_table[:, B_HEADS:])
    fn = final_norm[None, :]
    return tuple(_trunk(x3, layers, bias_b, far_b, bias_d, d_reach, fn) for x3 in (x_prompt, x_sample))
```
